```python
import math
import jax, jax.numpy as jnp
from jax import lax
import numpy as np

D_MODEL = 1024
BATCH = 4
SEQ = 8192
DEPTH = 2
DEC_BATCH = 8
DEC_SEQ = 32
PAST_LEN = 2048

CHUNK = 64
EPS = 1e-6
N_A_LAYERS = DEPTH // 2
N_B_LAYERS = DEPTH - N_A_LAYERS
N_DENSE = (DEPTH + 1) // 2
N_MOE = DEPTH // 2
SSM_EXPAND = 2
D_INNER = SSM_EXPAND * D_MODEL
SSM_HEAD_DIM = 64
N_SSM_HEADS = D_INNER // SSM_HEAD_DIM
SSM_GROUPS = 4
D_STATE = 128
CONV_WIDTH = 4
CONV_DIM = D_INNER + 2 * SSM_GROUPS * D_STATE
IN_PROJ_DIM = D_INNER + CONV_DIM + N_SSM_HEADS
ATTN_HEAD_DIM = 64
N_Q_HEADS = D_MODEL // ATTN_HEAD_DIM
N_KV_HEADS = 4
GQA = N_Q_HEADS // N_KV_HEADS
WINDOW = 128
ATTN_SCALE = 1.0 / math.sqrt(ATTN_HEAD_DIM)
NEG_INF = -1e30
D_FF = 256 * ((8 * D_MODEL // 3 + 255) // 256)
N_EXPERTS = 8
TOP_K = 2
D_FF_EXPERT = D_FF

kernel_name = 'yoco_ssd_window_sink_stream_step'


def rmsnorm(x, g):
    xf = x.astype(jnp.float32)
    y = xf * lax.rsqrt(jnp.mean(xf * xf, axis=-1, keepdims=True) + EPS)
    return (y * g.astype(jnp.float32)).astype(x.dtype)


def alibi_slopes():
    h = np.arange(1, N_Q_HEADS + 1, dtype=np.float32)
    return jnp.asarray(2.0 ** (-8.0 * h / N_Q_HEADS), dtype=jnp.float32)


def ssd_scan(x, dt, a, bm, cm, h0, block):
    f32 = jnp.float32
    bsz, seq, nh, hp = x.shape
    ng, ns = bm.shape[2], bm.shape[3]
    hpg = nh // ng
    nb = seq // block

    def to_blocks(t):
        return jnp.moveaxis(t.reshape((bsz, nb, block) + t.shape[2:]), 1, 0)

    xs = (to_blocks(x.astype(f32).reshape(bsz, seq, ng, hpg, hp)),
          to_blocks(dt.reshape(bsz, seq, ng, hpg)),
          to_blocks(bm.astype(f32)),
          to_blocks(cm.astype(f32)))
    a_g = a.reshape(ng, hpg)
    causal = jnp.tril(jnp.ones((block, block), dtype=bool))[None, :, :, None, None]

    def step(h, blk):
        xb, dtb, bb, cb = blk
        a_cs = jnp.cumsum(dtb * a_g, axis=1)
        seg = a_cs[:, :, None] - a_cs[:, None, :]
        decay = jnp.exp(jnp.where(causal, seg, -jnp.inf))
        cbm = jnp.einsum('btgn,bsgn->btsg', cb, bb)
        y_in = jnp.einsum('btsg,btsgh,bsgh,bsghp->btghp', cbm, decay, dtb, xb)
        y_off = jnp.einsum('btgn,bghpn->btghp', cb, h) * jnp.exp(a_cs)[..., None]
        last = a_cs[:, -1]
        w = jnp.exp(last[:, None] - a_cs) * dtb
        h_new = h * jnp.exp(last)[..., None, None] + jnp.einsum('bsgn,bsgh,bsghp->bghpn', bb, w, xb)
        return h_new, y_in + y_off

    h_fin, ys = lax.scan(step, h0.astype(f32).reshape(bsz, ng, hpg, hp, ns), xs)
    y = jnp.moveaxis(ys, 0, 1).reshape(bsz, seq, nh, hp)
    return y, h_fin.reshape(bsz, nh, hp, ns)


def mamba2_mixer(u, conv_buf, h0, in_w, conv_w, conv_b, dt_bias, a_log, d_skip, gnorm, out_w):
    bsz, seq, _ = u.shape
    zxbcdt = u @ in_w
    z = zxbcdt[..., :D_INNER]
    xbc = zxbcdt[..., D_INNER:D_INNER + CONV_DIM]
    dt_raw = zxbcdt[..., D_INNER + CONV_DIM:]
    xpad = jnp.concatenate([conv_buf.astype(xbc.dtype), xbc], axis=1)
    conv = conv_b
    for k in range(CONV_WIDTH):
        conv = conv + xpad[:, k:k + seq] * conv_w[k]
    new_buf = xpad[:, seq:]
    xbc = jax.nn.silu(conv)
    xs = xbc[..., :D_INNER].reshape(bsz, seq, N_SSM_HEADS, SSM_HEAD_DIM)
    bm = xbc[..., D_INNER:D_INNER + SSM_GROUPS * D_STATE].reshape(bsz, seq, SSM_GROUPS, D_STATE)
    cm = xbc[..., D_INNER + SSM_GROUPS * D_STATE:].reshape(bsz, seq, SSM_GROUPS, D_STATE)
    dt = jax.nn.softplus(dt_raw.astype(jnp.float32) + dt_bias.astype(jnp.float32))
    a = -jnp.exp(a_log.astype(jnp.float32))
    y, h_fin = ssd_scan(xs, dt, a, bm, cm, h0, min(CHUNK, seq))
    y = y + d_skip.astype(jnp.float32)[:, None] * xs.astype(jnp.float32)
    y = y.reshape(bsz, seq, D_INNER).astype(u.dtype)
    y = rmsnorm(y * jax.nn.silu(z), gnorm)
    return y @ out_w, new_buf, h_fin.astype(u.dtype)


def window_sink_attention(q, k_past, v_past, k_new, v_new, sinks, past_valid):
    bsz, seq = q.shape[:2]
    n_past = k_past.shape[1]
    lb = min(CHUNK, seq)
    nb = seq // lb
    k_all = jnp.concatenate([k_past.astype(k_new.dtype), k_new], axis=1)
    v_all = jnp.concatenate([v_past.astype(v_new.dtype), v_new], axis=1)
    rows = jnp.arange(nb)[:, None] * lb + jnp.arange(n_past + lb)[None, :]
    kb = k_all[:, rows]
    vb = v_all[:, rows]
    qb = q.reshape(bsz, nb, lb, N_KV_HEADS, GQA, ATTN_HEAD_DIM)
    s = jnp.einsum('bnqkgd,bnskd->bnkgqs', qb, kb, preferred_element_type=jnp.float32) * ATTN_SCALE
    dist = jnp.abs(n_past + jnp.arange(lb)[:, None] - jnp.arange(n_past + lb)[None, :]).astype(jnp.float32)
    slopes = alibi_slopes().reshape(N_KV_HEADS, GQA)
    s = s - slopes[None, None, :, :, None, None] * dist[None, None, None, None]
    valid = jnp.ones_like(rows, dtype=bool) if past_valid else rows >= n_past
    s = jnp.where(valid[None, :, None, None, None, :], s, NEG_INF)
    sink = sinks.astype(jnp.float32).reshape(N_KV_HEADS, GQA)[None, None, :, :, None, None]
    m = jnp.maximum(jnp.max(s, axis=-1, keepdims=True), sink)
    p = jnp.exp(s - m)
    denom = jnp.sum(p, axis=-1, keepdims=True) + jnp.exp(sink - m)
    o = jnp.einsum('bnkgqs,bnskd->bnqkgd', (p / denom).astype(vb.dtype), vb)
    return o.reshape(bsz, seq, N_Q_HEADS * ATTN_HEAD_DIM)


def swiglu(t, w_gu, w_down):
    g, u = jnp.split(t @ w_gu, 2, axis=-1)
    return (jax.nn.silu(g) * u) @ w_down


def moe_swiglu(h, router_w, w_gu, w_down):
    bsz, seq, d = h.shape
    t = h.reshape(bsz * seq, d)
    logits = (t @ router_w).astype(jnp.float32)
    top_vals, top_idx = lax.top_k(logits, TOP_K)
    top_w = jax.nn.softmax(top_vals, axis=-1)
    gates = jnp.sum(jax.nn.one_hot(top_idx, N_EXPERTS, dtype=jnp.float32) * top_w[..., None], axis=1)
    out = jnp.zeros_like(t)
    for e in range(N_EXPERTS):
        out = out + gates[:, e:e + 1].astype(t.dtype) * swiglu(t, w_gu[e], w_down[e])
    return out.reshape(bsz, seq, d)


def trunk(x, conv_bufs, ssm_states, k_past, v_past, past_valid, p):
    h = x
    new_conv, new_ssm = [], []
    k_new = v_new = None
    for layer in range(DEPTH):
        u = rmsnorm(h, p['norm_mix'][layer])
        if layer < N_A_LAYERS:
            a = layer
            mix, cb, hs = mamba2_mixer(u, conv_bufs[a], ssm_states[a], p['ssm_in_w'][a], p['ssm_conv_w'][a],
                                       p['ssm_conv_b'][a], p['ssm_dt_bias'][a], p['ssm_a_log'][a],
                                       p['ssm_d'][a], p['ssm_norm'][a], p['ssm_out_w'][a])
            new_conv.append(cb)
            new_ssm.append(hs)
        else:
            if layer == N_A_LAYERS:
                kv = rmsnorm(h, p['kv_norm']) @ p['w_kv']
                k_new, v_new = jnp.split(kv.reshape(h.shape[0], h.shape[1], 2 * N_KV_HEADS, ATTN_HEAD_DIM), 2, axis=2)
            b = layer - N_A_LAYERS
            o = window_sink_attention(u @ p['w_q'][b], k_past, v_past, k_new, v_new, p['attn_sinks'][b], past_valid)
            mix = o @ p['w_o'][b]
        h = h + mix
        f = rmsnorm(h, p['norm_ffn'][layer])
        if layer % 2 == 0:
            h = h + swiglu(f, p['ffn_w_gate_up'][layer // 2], p['ffn_w_down'][layer // 2])
        else:
            i = layer // 2
            h = h + moe_swiglu(f, p['moe_router'][i], p['moe_w_gate_up'][i], p['moe_w_down'][i])
    return rmsnorm(h, p['final_norm']), jnp.stack(new_conv, axis=0), jnp.stack(new_ssm, axis=0), k_new, v_new


def setup_inputs(seed: int = 0) -> dict:
    key = jax.random.key(seed)
    ks = jax.random.split(key, 28)
    f32 = jnp.float32

    def nrm(k, shape, scale):
        return jax.random.normal(k, shape, f32) * scale

    cache_rows = min(WINDOW, PAST_LEN)
    dt0 = jnp.exp(jax.random.uniform(ks[11], (N_A_LAYERS, N_SSM_HEADS), f32, math.log(1e-3), math.log(1e-1)))
    hd_q = N_Q_HEADS * ATTN_HEAD_DIM
    return {
        'x_prompt': nrm(ks[0], (BATCH, SEQ, D_MODEL), 1.0),
        'x_sample': nrm(ks[1], (DEC_BATCH, DEC_SEQ, D_MODEL), 1.0),
        'state_conv': nrm(ks[2], (N_A_LAYERS, DEC_BATCH, CONV_WIDTH - 1, CONV_DIM), 1.0),
        'state_ssm': nrm(ks[3], (N_A_LAYERS, DEC_BATCH, N_SSM_HEADS, SSM_HEAD_DIM, D_STATE), 0.1),
        'cache_k': nrm(ks[4], (DEC_BATCH, cache_rows, N_KV_HEADS, ATTN_HEAD_DIM), 1.0),
        'cache_v': nrm(ks[5], (DEC_BATCH, cache_rows, N_KV_HEADS, ATTN_HEAD_DIM), 1.0),
        'norm_mix': 1.0 + nrm(ks[6], (DEPTH, D_MODEL), 0.02),
        'norm_ffn': 1.0 + nrm(ks[7], (DEPTH, D_MODEL), 0.02),
        'ssm_in_w': nrm(ks[8], (N_A_LAYERS, D_MODEL, IN_PROJ_DIM), D_MODEL ** -0.5),
        'ssm_conv_w': nrm(ks[9], (N_A_LAYERS, CONV_WIDTH, CONV_DIM), CONV_WIDTH ** -0.5),
        'ssm_conv_b': nrm(ks[10], (N_A_LAYERS, CONV_DIM), 0.02),
        'ssm_dt_bias': dt0 + jnp.log(-jnp.expm1(-dt0)),
        'ssm_a_log': jnp.log(jax.random.uniform(ks[12], (N_A_LAYERS, N_SSM_HEADS), f32, 1.0, 16.0)),
        'ssm_d': 1.0 + nrm(ks[13], (N_A_LAYERS, N_SSM_HEADS), 0.1),
        'ssm_norm': 1.0 + nrm(ks[14], (N_A_LAYERS, D_INNER), 0.02),
        'ssm_out_w': nrm(ks[15], (N_A_LAYERS, D_INNER, D_MODEL), D_INNER ** -0.5),
        'kv_norm': 1.0 + nrm(ks[16], (D_MODEL,), 0.02),
        'w_kv': nrm(ks[17], (D_MODEL, 2 * N_KV_HEADS * ATTN_HEAD_DIM), D_MODEL ** -0.5),
        'w_q': nrm(ks[18], (N_B_LAYERS, D_MODEL, hd_q), D_MODEL ** -0.5),
        'attn_sinks': nrm(ks[19], (N_B_LAYERS, N_Q_HEADS), 0.5),
        'w_o': nrm(ks[20], (N_B_LAYERS, hd_q, D_MODEL), hd_q ** -0.5),
        'ffn_w_gate_up': nrm(ks[21], (N_DENSE, D_MODEL, 2 * D_FF), D_MODEL ** -0.5),
        'ffn_w_down': nrm(ks[22], (N_DENSE, D_FF, D_MODEL), D_FF ** -0.5),
        'moe_router': nrm(ks[23], (N_MOE, D_MODEL, N_EXPERTS), D_MODEL ** -0.5),
        'moe_w_gate_up': nrm(ks[24], (N_MOE, N_EXPERTS, D_MODEL, 2 * D_FF_EXPERT), D_MODEL ** -0.5),
        'moe_w_down': nrm(ks[25], (N_MOE, N_EXPERTS, D_FF_EXPERT, D_MODEL), D_FF_EXPERT ** -0.5),
        'final_norm': 1.0 + nrm(ks[26], (D_MODEL,), 0.02),
    }


def reference(x_prompt, x_sample, state_conv, state_ssm, cache_k, cache_v,
              norm_mix, norm_ffn, ssm_in_w, ssm_conv_w, ssm_conv_b, ssm_dt_bias, ssm_a_log, ssm_d,
              ssm_norm, ssm_out_w, kv_norm, w_kv, w_q, attn_sinks, w_o, ffn_w_gate_up, ffn_w_down,
              moe_router, moe_w_gate_up, moe_w_down, final_norm):
    p = {'norm_mix': norm_mix, 'norm_ffn': norm_ffn, 'ssm_in_w': ssm_in_w, 'ssm_conv_w': ssm_conv_w,
         'ssm_conv_b': ssm_conv_b, 'ssm_dt_bias': ssm_dt_bias, 'ssm_a_log': ssm_a_log, 'ssm_d': ssm_d,
         'ssm_norm': ssm_norm, 'ssm_out_w': ssm_out_w, 'kv_norm': kv_norm, 'w_kv': w_kv, 'w_q': w_q,
         'attn_sinks': attn_sinks, 'w_o': w_o, 'ffn_w_gate_up': ffn_w_gate_up, 'ffn_w_down': ffn_w_down,
         'moe_router': moe_router, 'moe_w_gate_up': moe_w_gate_up, 'moe_w_down': moe_w_down,
         'final_norm': final_norm}
    bsz, seq_p = x_prompt.shape[:2]
    dt = x_prompt.dtype
    zero_conv = jnp.zeros((N_A_LAYERS, bsz, CONV_WIDTH - 1, CONV_DIM), dt)
    zero_ssm = jnp.zeros((N_A_LAYERS, bsz, N_SSM_HEADS, SSM_HEAD_DIM, D_STATE), dt)
    zero_kv = jnp.zeros((bsz, WINDOW, N_KV_HEADS, ATTN_HEAD_DIM), dt)
    y_prompt, conv_p, ssm_p, k_p, v_p = trunk(x_prompt, zero_conv, zero_ssm, zero_kv, zero_kv, False, p)
    keep = min(WINDOW, seq_p)
    k_p_rows = k_p[:, seq_p - keep:]
    v_p_rows = v_p[:, seq_p - keep:]
    y_sample, conv_s, ssm_s, k_s, v_s = trunk(x_sample, state_conv, state_ssm, cache_k, cache_v, True, p)
    return (y_prompt, y_sample, conv_p, ssm_p, k_p_rows, v_p_rows, conv_s, ssm_s, k_s, v_s)
```

```python
import functools
import math

import jax
import jax.numpy as jnp
import numpy as np
from jax import lax
from jax.experimental import pallas as pl
from jax.experimental.pallas import tpu as pltpu

F32 = jnp.float32
BF16 = jnp.bfloat16

D_MODEL = 1024
EPS = 1e-6
D_INNER = 2048
SSM_HEAD_DIM = 64
N_SSM_HEADS = 32
SSM_GROUPS = 4
HEADS_PER_GROUP = N_SSM_HEADS // SSM_GROUPS
D_STATE = 128
CONV_WIDTH = 4
CONV_DIM = D_INNER + 2 * SSM_GROUPS * D_STATE
ATTN_HEAD_DIM = 64
N_Q_HEADS = 16
N_KV_HEADS = 4
KV_DIM = N_KV_HEADS * ATTN_HEAD_DIM
WINDOW = 128
ATTN_CHUNK = 64
ATTN_SCALE = 1.0 / math.sqrt(ATTN_HEAD_DIM)
NEG_INF = -1e30
D_FF = 2816
N_EXPERTS = 8

LANES = 128
SUBLANES = 8
VMEM_LIMIT_BYTES = 56 * 1024 * 1024

SSD_BLOCK = 128
GROUP_COLS = HEADS_PER_GROUP * SSM_HEAD_DIM


def _cparams(*sem):
    return pltpu.CompilerParams(dimension_semantics=sem, vmem_limit_bytes=VMEM_LIMIT_BYTES)


def _row_tile(t):
    return 512 if t % 512 == 0 else t


def _rms_scale(x):
    return lax.rsqrt(jnp.mean(x * x, axis=-1, keepdims=True) + EPS)


def _silu(x):
    return x / (1.0 + jnp.exp(-x))


def _norm_matmul_kernel(x_ref, g_ref, w_ref, o_ref):
    x = x_ref[...]
    xn = (x * _rms_scale(x) * g_ref[...]).astype(BF16)
    o_ref[...] = jnp.dot(xn, w_ref[...], preferred_element_type=F32).astype(o_ref.dtype)


def norm_matmul(x, g, w, out_dtype=F32):
    t, k = x.shape
    n = w.shape[1]
    tm = _row_tile(t)
    return pl.pallas_call(
        _norm_matmul_kernel,
        grid=(t // tm,),
        in_specs=[pl.BlockSpec((tm, k), lambda i: (i, 0)),
                  pl.BlockSpec((1, k), lambda i: (0, 0)),
                  pl.BlockSpec((k, n), lambda i: (0, 0))],
        out_specs=pl.BlockSpec((tm, n), lambda i: (i, 0)),
        out_shape=jax.ShapeDtypeStruct((t, n), out_dtype),
        compiler_params=_cparams("parallel"),
        name="norm_matmul",
    )(x, g.reshape(1, k), w)


def _norm_matmul_pairs_kernel(x_ref, g_ref, w_ref, o_ref):
    x = x_ref[...]
    xn = (x * _rms_scale(x) * g_ref[...]).astype(BF16)
    res = jnp.dot(xn, w_ref[...], preferred_element_type=F32).astype(o_ref.dtype)
    for p in range(o_ref.shape[0]):
        o_ref[p] = res[:, p * LANES:(p + 1) * LANES]


def norm_matmul_pairs(x, g, w):
    t, k = x.shape
    n = w.shape[1]
    tm = _row_tile(t)
    return pl.pallas_call(
        _norm_matmul_pairs_kernel,
        grid=(t // tm,),
        in_specs=[pl.BlockSpec((tm, k), lambda i: (i, 0)),
                  pl.BlockSpec((1, k), lambda i: (0, 0)),
                  pl.BlockSpec((k, n), lambda i: (0, 0))],
        out_specs=pl.BlockSpec((n // LANES, tm, LANES), lambda i: (0, i, 0)),
        out_shape=jax.ShapeDtypeStruct((n // LANES, t, LANES), BF16),
        compiler_params=_cparams("parallel"),
        name="norm_matmul_pairs",
    )(x, g.reshape(1, k), w)


def _matmul_residual_kernel(a_ref, w_ref, r_ref, o_ref):
    o_ref[...] = r_ref[...] + jnp.dot(a_ref[...], w_ref[...], preferred_element_type=F32)


def matmul_residual(a, w, r):
    t, k = a.shape
    n = w.shape[1]
    tm = _row_tile(t)
    return pl.pallas_call(
        _matmul_residual_kernel,
        grid=(t // tm,),
        in_specs=[pl.BlockSpec((tm, k), lambda i: (i, 0)),
                  pl.BlockSpec((k, n), lambda i: (0, 0)),
                  pl.BlockSpec((tm, n), lambda i: (i, 0))],
        out_specs=pl.BlockSpec((tm, n), lambda i: (i, 0)),
        out_shape=jax.ShapeDtypeStruct((t, n), F32),
        compiler_params=_cparams("parallel"),
        name="matmul_residual",
    )(a, w, r)


FF_TILE = D_FF // 2


def _ffn_kernel(x_ref, g_ref, gate_ref, wg_ref, wu_ref, wd_ref, fg_ref, o_ref, xn_ref, *, final_norm):
    e = pl.program_id(1)
    j = pl.program_id(2)
    first = jnp.logical_and(e == 0, j == 0)
    last = jnp.logical_and(e == pl.num_programs(1) - 1, j == pl.num_programs(2) - 1)

    @pl.when(first)
    def _():
        x = x_ref[...]
        xn_ref[...] = (x * _rms_scale(x) * g_ref[...]).astype(BF16)
        o_ref[...] = x

    xn = xn_ref[...]
    gg = jnp.dot(xn, wg_ref[0], preferred_element_type=F32)
    uu = jnp.dot(xn, wu_ref[0], preferred_element_type=F32)
    lane = lax.broadcasted_iota(jnp.int32, gate_ref.shape, 1)
    gate_col = jnp.sum(jnp.where(lane == e, gate_ref[...], 0.0), axis=-1, keepdims=True)
    act = (_silu(gg) * uu * gate_col).astype(BF16)
    o_ref[...] += jnp.dot(act, wd_ref[0], preferred_element_type=F32)

    if final_norm:
        @pl.when(last)
        def _():
            y = o_ref[...]
            o_ref[...] = y * _rms_scale(y) * fg_ref[...]


def ffn(x, g, gates, w_gu, w_down, final_gain=None):
    t, d = x.shape
    n_exp = w_gu.shape[0]
    tm = _row_tile(t)
    n_ff = D_FF // FF_TILE
    final_norm = final_gain is not None
    fg = (final_gain if final_norm else jnp.ones((d,), F32)).reshape(1, d)
    return pl.pallas_call(
        functools.partial(_ffn_kernel, final_norm=final_norm),
        grid=(t // tm, n_exp, n_ff),
        in_specs=[pl.BlockSpec((tm, d), lambda i, e, j: (i, 0)),
                  pl.BlockSpec((1, d), lambda i, e, j: (0, 0)),
                  pl.BlockSpec((tm, LANES), lambda i, e, j: (i, 0)),
                  pl.BlockSpec((1, d, FF_TILE), lambda i, e, j: (e, 0, j)),
                  pl.BlockSpec((1, d, FF_TILE), lambda i, e, j: (e, 0, j + n_ff)),
                  pl.BlockSpec((1, FF_TILE, d), lambda i, e, j: (e, j, 0)),
                  pl.BlockSpec((1, d), lambda i, e, j: (0, 0))],
        out_specs=pl.BlockSpec((tm, d), lambda i, e, j: (i, 0)),
        out_shape=jax.ShapeDtypeStruct((t, d), F32),
        scratch_shapes=[pltpu.VMEM((tm, d), BF16)],
        compiler_params=_cparams("parallel", "arbitrary", "arbitrary"),
        name="ffn",
    )(x, g.reshape(1, d), gates, w_gu, w_gu, w_down, fg)


def _router_kernel(x_ref, g_ref, whi_ref, wlo_ref, gate_ref):
    x = x_ref[...]
    xn = x * _rms_scale(x) * g_ref[...]
    x_hi = xn.astype(BF16)
    x_lo = (xn - x_hi.astype(F32)).astype(BF16)
    logits = (jnp.dot(x_hi, whi_ref[...], preferred_element_type=F32)
              + jnp.dot(x_lo, whi_ref[...], preferred_element_type=F32)
              + jnp.dot(x_hi, wlo_ref[...], preferred_element_type=F32))
    lane = lax.broadcasted_iota(jnp.int32, logits.shape, 1)
    logits = jnp.where(lane < N_EXPERTS, logits, -jnp.inf)
    m1 = jnp.max(logits, axis=-1, keepdims=True)
    i1 = jnp.min(jnp.where(logits == m1, lane, LANES), axis=-1, keepdims=True)
    rest = jnp.where(lane == i1, -jnp.inf, logits)
    m2 = jnp.max(rest, axis=-1, keepdims=True)
    i2 = jnp.min(jnp.where(rest == m2, lane, LANES), axis=-1, keepdims=True)
    e2 = jnp.exp(m2 - m1)
    w1 = 1.0 / (1.0 + e2)
    w2 = e2 / (1.0 + e2)
    gate_ref[...] = jnp.where(lane == i1, w1, 0.0) + jnp.where(lane == i2, w2, 0.0)


def router(x, g, w_router):
    t, d = x.shape
    tm = _row_tile(t)
    w_hi = w_router.astype(BF16)
    w_lo = (w_router - w_hi.astype(F32)).astype(BF16)
    return pl.pallas_call(
        _router_kernel,
        grid=(t // tm,),
        in_specs=[pl.BlockSpec((tm, d), lambda i: (i, 0)),
                  pl.BlockSpec((1, d), lambda i: (0, 0)),
                  pl.BlockSpec((d, LANES), lambda i: (0, 0)),
                  pl.BlockSpec((d, LANES), lambda i: (0, 0))],
        out_specs=pl.BlockSpec((tm, LANES), lambda i: (i, 0)),
        out_shape=jax.ShapeDtypeStruct((t, LANES), F32),
        compiler_params=_cparams("parallel"),
        name="router",
    )(x, g.reshape(1, d), w_hi, w_lo)


def _split3(v):
    hi = v.astype(BF16)
    r1 = v - hi.astype(F32)
    mid = r1.astype(BF16)
    lo = (r1 - mid.astype(F32)).astype(BF16)
    return hi, mid, lo


def _ssd_kernel(xbc_ref, dt_ref, z_ref, cbuf_ref, h0_ref, cw_ref, cb_ref, dtb_ref, alog_ref,
                dskip_ref, gn_ref, expand_ref, yn_ref, hfin_ref, h_scr, xext_scr, *, seq_valid):
    q = SSD_BLOCK
    c = pl.program_id(1)
    halo = SUBLANES

    @pl.when(c == 0)
    def _():
        h_scr[...] = h0_ref[0]
        xext_scr[0:halo, :] = cbuf_ref[0]

    xext_scr[halo:halo + q, :] = xbc_ref[0]
    conv = cb_ref[...]
    for k in range(CONV_WIDTH):
        start = halo - (CONV_WIDTH - 1) + k
        conv = conv + xext_scr[start:start + q, :] * cw_ref[k:k + 1, :]
    xext_scr[0:halo, :] = xext_scr[q:q + halo, :]
    xbc = _silu(conv)
    xs = xbc[:, :D_INNER]
    b_all = xbc[:, D_INNER:D_INNER + SSM_GROUPS * D_STATE]
    c_all = xbc[:, D_INNER + SSM_GROUPS * D_STATE:]

    v = dt_ref[0] + dtb_ref[...]
    dt = jnp.maximum(v, 0.0) + jnp.log1p(jnp.exp(-jnp.abs(v)))
    row = lax.broadcasted_iota(jnp.int32, (q, LANES), 0)
    if seq_valid is not None:
        dt = jnp.where(row + c * q < seq_valid, dt, 0.0)
    a_neg = -jnp.exp(alog_ref[...])
    dta = dt * a_neg

    ti = lax.broadcasted_iota(jnp.int32, (q, q), 0)
    si = lax.broadcasted_iota(jnp.int32, (q, q), 1)
    causal = ti >= si
    tri = jnp.where(causal, 1.0, 0.0).astype(BF16)
    a_cs = sum(jnp.dot(tri, part, preferred_element_type=F32) for part in _split3(dta))
    a_last = a_cs[q - 1:q, :]
    a_cs_t = a_cs.T

    stacked = jnp.concatenate([dt, jnp.exp(a_cs), jnp.exp(a_last - a_cs),
                               jnp.broadcast_to(jnp.exp(a_last), (SUBLANES, LANES))], axis=0)
    s_hi = stacked.astype(BF16)
    s_lo = (stacked - s_hi.astype(F32)).astype(BF16)
    expand = expand_ref[...]
    expanded = (jnp.dot(s_hi, expand, preferred_element_type=F32)
                + jnp.dot(s_lo, expand, preferred_element_type=F32))
    dt_e = expanded[0:q]
    decay_in_e = expanded[q:2 * q]
    decay_out_e = expanded[2 * q:3 * q]
    decay_blk_e = expanded[3 * q:3 * q + 1]

    xdt = xs * dt_e
    lane_d = lax.broadcasted_iota(jnp.int32, (q, D_INNER), 1)
    low_half = (lane_d % LANES) < SSM_HEAD_DIM
    xdt_lo = jnp.where(low_half, xdt, 0.0).astype(BF16)
    xdt_hi = jnp.where(low_half, 0.0, xdt).astype(BF16)
    wx = (xdt * decay_out_e).astype(BF16)

    y_groups = []
    for g in range(SSM_GROUPS):
        bg = b_all[:, g * D_STATE:(g + 1) * D_STATE]
        cg = c_all[:, g * D_STATE:(g + 1) * D_STATE].astype(BF16)
        bg_bf = bg.astype(BF16)
        cbm = lax.dot_general(cg, bg_bf, (((1,), (1,)), ((), ())), preferred_element_type=F32)
        cols = slice(g * GROUP_COLS, (g + 1) * GROUP_COLS)
        h_g = h_scr[g]
        y_g = jnp.dot(cg, h_g.astype(BF16), preferred_element_type=F32) * decay_in_e[:, cols]
        pair_out = []
        for pr in range(HEADS_PER_GROUP // 2):
            acc = None
            for par in range(2):
                h = g * HEADS_PER_GROUP + 2 * pr + par
                seg = (jnp.broadcast_to(a_cs[:, h:h + 1], (q, q))
                       - jnp.broadcast_to(a_cs_t[h:h + 1, :], (q, q)))
                m = (cbm * jnp.exp(jnp.where(causal, seg, -jnp.inf))).astype(BF16)
                pcols = slice((h // 2) * LANES, (h // 2 + 1) * LANES)
                src = xdt_lo if par == 0 else xdt_hi
                part = jnp.dot(m, src[:, pcols], preferred_element_type=F32)
                acc = part if acc is None else acc + part
            pair_out.append(acc)
        y_groups.append(y_g + jnp.concatenate(pair_out, axis=1))
        h_scr[g] = (h_g * decay_blk_e[:, cols]
                    + jnp.dot(bg.T.astype(BF16), wx[:, cols], preferred_element_type=F32))

    y = jnp.concatenate(y_groups, axis=1) + dskip_ref[...] * xs
    yz = y * _silu(z_ref[0])
    yn_ref[0] = (yz * _rms_scale(yz) * gn_ref[...]).astype(yn_ref.dtype)

    @pl.when(c == pl.num_programs(1) - 1)
    def _():
        hfin_ref[0] = h_scr[...]


def _state_to_kernel_layout(h):
    b = h.shape[0]
    h = h.reshape(b, SSM_GROUPS, HEADS_PER_GROUP, SSM_HEAD_DIM, D_STATE)
    return h.transpose(0, 1, 4, 2, 3).reshape(b, SSM_GROUPS, D_STATE, GROUP_COLS)


def _state_from_kernel_layout(h):
    b = h.shape[0]
    h = h.reshape(b, SSM_GROUPS, D_STATE, HEADS_PER_GROUP, SSM_HEAD_DIM)
    return h.transpose(0, 1, 3, 4, 2).reshape(b, N_SSM_HEADS, SSM_HEAD_DIM, D_STATE)


def _pad_lanes(v):
    return jnp.pad(v.astype(F32), (0, LANES - v.shape[0])).reshape(1, LANES)


def ssd_mixer(xbc, dt_raw, z, conv_buf, h0, conv_w, conv_b, dt_bias, a_log, d_skip, gnorm):
    bsz, seq, _ = xbc.shape
    q = SSD_BLOCK
    seq_pad = -(-seq // q) * q
    seq_valid = None
    if seq_pad != seq:
        pad = ((0, 0), (0, seq_pad - seq), (0, 0))
        xbc, dt_raw, z = jnp.pad(xbc, pad), jnp.pad(dt_raw, pad), jnp.pad(z, pad)
        seq_valid = seq
    cbuf = jnp.pad(conv_buf, ((0, 0), (SUBLANES - (CONV_WIDTH - 1), 0), (0, 0)))
    expand = (jnp.arange(D_INNER)[None, :] // SSM_HEAD_DIM == jnp.arange(LANES)[:, None]).astype(BF16)
    d_e = jnp.repeat(d_skip.astype(F32), SSM_HEAD_DIM).reshape(1, D_INNER)
    state_spec = pl.BlockSpec((1, SSM_GROUPS, D_STATE, GROUP_COLS), lambda b, c: (b, 0, 0, 0))

    def const(shape):
        return pl.BlockSpec(shape, lambda b, c: (0,) * len(shape))

    yn, h_fin = pl.pallas_call(
        functools.partial(_ssd_kernel, seq_valid=seq_valid),
        grid=(bsz, seq_pad // q),
        in_specs=[pl.BlockSpec((1, q, CONV_DIM), lambda b, c: (b, c, 0)),
                  pl.BlockSpec((1, q, LANES), lambda b, c: (b, c, 0)),
                  pl.BlockSpec((1, q, D_INNER), lambda b, c: (b, c, 0)),
                  pl.BlockSpec((1, SUBLANES, CONV_DIM), lambda b, c: (b, 0, 0)),
                  state_spec,
                  const((CONV_WIDTH, CONV_DIM)), const((1, CONV_DIM)), const((1, LANES)), const((1, LANES)),
                  const((1, D_INNER)), const((1, D_INNER)), const((LANES, D_INNER))],
        out_specs=[pl.BlockSpec((1, q, D_INNER), lambda b, c: (b, c, 0)), state_spec],
        out_shape=[jax.ShapeDtypeStruct((bsz, seq_pad, D_INNER), BF16),
                   jax.ShapeDtypeStruct((bsz, SSM_GROUPS, D_STATE, GROUP_COLS), F32)],
        scratch_shapes=[pltpu.VMEM((SSM_GROUPS, D_STATE, GROUP_COLS), F32),
                        pltpu.VMEM((q + SUBLANES, CONV_DIM), F32)],
        compiler_params=_cparams("parallel", "arbitrary"),
        name="ssd_mixer",
    )(xbc, dt_raw, z, cbuf, _state_to_kernel_layout(h0.astype(F32)), conv_w.astype(F32),
      conv_b.astype(F32).reshape(1, CONV_DIM), _pad_lanes(dt_bias), _pad_lanes(a_log), d_e,
      gnorm.astype(F32).reshape(1, D_INNER), expand)
    return yn[:, :seq], _state_from_kernel_layout(h_fin)


_ALIBI_SLOPES = (2.0 ** (-8.0 * np.arange(1, N_Q_HEADS + 1, dtype=np.float32) / N_Q_HEADS)).astype(np.float32)


def _alibi_slope(h):
    return float(_ALIBI_SLOPES[h])


def _attn_kernel(sink_ref, q_ref, k_ref, v_ref, o_ref, *, lbq, chunk, past_valid):
    n = pl.program_id(1)
    w = WINDOW + lbq
    base = pl.multiple_of(n * lbq, lbq)
    kband = k_ref[0, pl.ds(base, w), :].astype(F32)
    vband = v_ref[0, pl.ds(base, w), :].astype(F32)

    qi = lax.broadcasted_iota(jnp.int32, (2 * lbq, w), 0) % lbq
    si = lax.broadcasted_iota(jnp.int32, (2 * lbq, w), 1)
    dist = jnp.abs(WINDOW + qi - si).astype(F32)
    lo = (qi // chunk) * chunk
    valid = jnp.logical_and(si >= lo, si < lo + WINDOW + chunk)
    if not past_valid:
        valid = jnp.logical_and(valid, si + base >= WINDOW)
    first_pair = lax.broadcasted_iota(jnp.int32, (2 * lbq, 1), 0) < lbq
    lane = lax.broadcasted_iota(jnp.int32, (w, LANES), 1)
    low_half = lane < ATTN_HEAD_DIM

    for kv in range(N_KV_HEADS):
        pcols = slice((kv // 2) * LANES, (kv // 2 + 1) * LANES)
        kpair, vpair = kband[:, pcols], vband[:, pcols]
        kswap = pltpu.roll(kpair, ATTN_HEAD_DIM, axis=1)
        vswap = pltpu.roll(vpair, ATTN_HEAD_DIM, axis=1)
        in_low = kv % 2 == 0
        k_lo = jnp.where(low_half, kpair if in_low else kswap, 0.0).astype(BF16)
        k_hi = jnp.where(low_half, 0.0, kswap if in_low else kpair).astype(BF16)
        v_lo = jnp.where(low_half, vpair if in_low else vswap, 0.0).astype(BF16)
        v_hi = jnp.where(low_half, 0.0, vswap if in_low else vpair).astype(BF16)
        qs = q_ref[2 * kv:2 * kv + 2, 0].reshape(2 * lbq, LANES)
        out = None
        for par, (kx, vx) in enumerate(((k_lo, v_lo), (k_hi, v_hi))):
            h0, h1 = 4 * kv + par, 4 * kv + 2 + par
            slope = jnp.where(first_pair, _alibi_slope(h0), _alibi_slope(h1))
            sink = jnp.where(first_pair, sink_ref[h0], sink_ref[h1])
            s = lax.dot_general(qs, kx, (((1,), (1,)), ((), ())), preferred_element_type=F32)
            s = s * ATTN_SCALE - slope * dist
            s = jnp.where(valid, s, NEG_INF)
            m = jnp.maximum(jnp.max(s, axis=-1, keepdims=True), sink)
            p = jnp.exp(s - m)
            denom = jnp.sum(p, axis=-1, keepdims=True) + jnp.exp(sink - m)
            pn = (p / denom).astype(BF16)
            part = jnp.dot(pn, vx, preferred_element_type=F32)
            out = part if out is None else out + part
        o_ref[0, :, (2 * kv) * LANES:(2 * kv + 1) * LANES] = out[:lbq].astype(o_ref.dtype)
        o_ref[0, :, (2 * kv + 1) * LANES:(2 * kv + 2) * LANES] = out[lbq:].astype(o_ref.dtype)


def window_attention(q_pairs, k_all, v_all, sinks, *, lbq, chunk, past_valid):
    n_pairs, bsz, seq, _ = q_pairs.shape
    rows = k_all.shape[1]
    return pl.pallas_call(
        functools.partial(_attn_kernel, lbq=lbq, chunk=chunk, past_valid=past_valid),
        grid_spec=pltpu.PrefetchScalarGridSpec(
            num_scalar_prefetch=1,
            grid=(bsz, seq // lbq),
            in_specs=[pl.BlockSpec((n_pairs, 1, lbq, LANES), lambda b, n, s: (0, b, n, 0)),
                      pl.BlockSpec((1, rows, KV_DIM), lambda b, n, s: (b, 0, 0)),
                      pl.BlockSpec((1, rows, KV_DIM), lambda b, n, s: (b, 0, 0))],
            out_specs=pl.BlockSpec((1, lbq, D_MODEL), lambda b, n, s: (b, n, 0))),
        out_shape=jax.ShapeDtypeStruct((bsz, seq, D_MODEL), BF16),
        compiler_params=_cparams("parallel", "arbitrary"),
        name="window_attention",
    )(sinks.astype(F32), q_pairs, k_all, v_all)


def _trunk(x, conv_buf, ssm_state, k_past, v_past, past_valid, p):
    bsz, seq, d = x.shape
    t = bsz * seq
    x2 = x.reshape(t, d)

    g0 = p['norm_mix'][0]
    z = norm_matmul(x2, g0, p['w_z'])
    xbc = norm_matmul(x2, g0, p['w_xbc'])
    dt_raw = norm_matmul(x2, g0, p['w_dt'])
    xbc3 = xbc.reshape(bsz, seq, CONV_DIM)
    new_conv = jnp.concatenate([conv_buf.astype(F32), xbc3], axis=1)[:, seq:]
    yn, new_ssm = ssd_mixer(xbc3, dt_raw.reshape(bsz, seq, LANES), z.reshape(bsz, seq, D_INNER),
                            conv_buf.astype(F32), ssm_state, p['ssm_conv_w'], p['ssm_conv_b'],
                            p['ssm_dt_bias'], p['ssm_a_log'], p['ssm_d'], p['ssm_norm'])
    h = matmul_residual(yn.reshape(t, D_INNER), p['ssm_out_w'], x2)

    ones_gate = jnp.zeros((t, LANES), F32).at[:, 0].set(1.0)
    h = ffn(h, p['norm_ffn'][0], ones_gate, p['ffn_w_gate_up'], p['ffn_w_down'])

    kv = norm_matmul(h, p['kv_norm'], p['w_kv']).reshape(bsz, seq, 2 * KV_DIM)
    k_new, v_new = kv[..., :KV_DIM], kv[..., KV_DIM:]
    k_all = jnp.concatenate([k_past.reshape(bsz, WINDOW, KV_DIM), k_new], axis=1).astype(BF16)
    v_all = jnp.concatenate([v_past.reshape(bsz, WINDOW, KV_DIM), v_new], axis=1).astype(BF16)

    chunk = min(ATTN_CHUNK, seq)
    lbq = 2 * chunk if seq % (2 * chunk) == 0 else chunk
    q_pairs = norm_matmul_pairs(h, p['norm_mix'][1], p['w_q']).reshape(D_MODEL // LANES, bsz, seq, LANES)
    o = window_attention(q_pairs, k_all, v_all, p['attn_sinks'], lbq=lbq, chunk=chunk, past_valid=past_valid)
    h = matmul_residual(o.reshape(t, D_MODEL), p['w_o'], h)

    gates = router(h, p['norm_ffn'][1], p['moe_router'])
    y = ffn(h, p['norm_ffn'][1], gates, p['moe_w_gate_up'], p['moe_w_down'], final_gain=p['final_norm'])

    k4 = k_new.reshape(bsz, seq, N_KV_HEADS, ATTN_HEAD_DIM)
    v4 = v_new.reshape(bsz, seq, N_KV_HEADS, ATTN_HEAD_DIM)
    return y.reshape(bsz, seq, d), new_conv[None], new_ssm[None], k4, v4


def kernel(x_prompt, x_sample, state_conv, state_ssm, cache_k, cache_v, norm_mix, norm_ffn, ssm_in_w, ssm_conv_w,
           ssm_conv_b, ssm_dt_bias, ssm_a_log, ssm_d, ssm_norm, ssm_out_w, kv_norm, w_kv, w_q, attn_sinks, w_o,
           ffn_w_gate_up, ffn_w_down, moe_router, moe_w_gate_up, moe_w_down, final_norm):
    in_w = ssm_in_w[0]
    n_dt = N_SSM_HEADS
    p = {
        'norm_mix': norm_mix, 'norm_ffn': norm_ffn,
        'w_z': in_w[:, :D_INNER].astype(BF16),
        'w_xbc': in_w[:, D_INNER:D_INNER + CONV_DIM].astype(BF16),
        'w_dt': jnp.pad(in_w[:, D_INNER + CONV_DIM:], ((0, 0), (0, LANES - n_dt))).astype(BF16),
        'ssm_conv_w': ssm_conv_w[0], 'ssm_conv_b': ssm_conv_b[0], 'ssm_dt_bias': ssm_dt_bias[0],
        'ssm_a_log': ssm_a_log[0], 'ssm_d': ssm_d[0], 'ssm_norm': ssm_norm[0],
        'ssm_out_w': ssm_out_w[0].astype(BF16),
        'kv_norm': kv_norm, 'w_kv': w_kv.astype(BF16), 'w_q': w_q[0].astype(BF16),
        'attn_sinks': attn_sinks[0], 'w_o': w_o[0].astype(BF16),
        'ffn_w_gate_up': ffn_w_gate_up.astype(BF16), 'ffn_w_down': ffn_w_down.astype(BF16),
        'moe_router': jnp.pad(moe_router[0].astype(F32), ((0, 0), (0, LANES - N_EXPERTS))),
        'moe_w_gate_up': moe_w_gate_up[0].astype(BF16), 'moe_w_down': moe_w_down[0].astype(BF16),
        'final_norm': final_norm,
    }
    bsz, seq_p = x_prompt.shape[:2]
    dt = x_prompt.dtype
    zero_conv = jnp.zeros((bsz, CONV_WIDTH - 1, CONV_DIM), dt)
    zero_ssm = jnp.zeros((bsz, N_SSM_HEADS, SSM_HEAD_DIM, D_STATE), dt)
    zero_kv = jnp.zeros((bsz, WINDOW, N_KV_HEADS, ATTN_HEAD_DIM), dt)
    y_p, conv_p, ssm_p, k_p, v_p = _trunk(x_prompt, zero_conv, zero_ssm, zero_kv, zero_kv, False, p)
    keep = min(WINDOW, seq_p)
    y_s, conv_s, ssm_s, k_s, v_s = _trunk(x_sample, state_conv[0], state_ssm[0], cache_k, cache_v, True, p)
    return (y_p, y_s, conv_p, ssm_p, k_p[:, seq_p - keep:], v_p[:, seq_p - keep:], conv_s, ssm_s, k_s, v_s)
```

```python
import functools
import math

import jax
import jax.numpy as jnp
import numpy as np
from jax import lax
from jax.experimental import pallas as pl
from jax.experimental.pallas import tpu as pltpu

F32 = jnp.float32
BF16 = jnp.bfloat16
I32 = jnp.int32

D_MODEL = 1024
EPS = 1e-6
D_INNER = 2048
SSM_HEAD_DIM = 64
N_SSM_HEADS = 32
SSM_GROUPS = 4
HEADS_PER_GROUP = N_SSM_HEADS // SSM_GROUPS
D_STATE = 128
CONV_WIDTH = 4
CONV_DIM = D_INNER + 2 * SSM_GROUPS * D_STATE
ATTN_HEAD_DIM = 64
N_Q_HEADS = 16
N_KV_HEADS = 4
KV_DIM = N_KV_HEADS * ATTN_HEAD_DIM
WINDOW = 128
ATTN_CHUNK = 64
ATTN_SCALE = 1.0 / math.sqrt(ATTN_HEAD_DIM)
NEG_INF = -1e30
D_FF = 2816
N_EXPERTS = 8
TOP_K = 2

LANES = 128
SUBLANES = 8
VMEM_LIMIT_BYTES = 56 * 1024 * 1024
SMEM_I32_CHUNK = 1024

SSD_BLOCK = 128
GROUP_COLS = HEADS_PER_GROUP * SSM_HEAD_DIM
FF_TILE = D_FF // 2


def _cparams(*sem):
    return pltpu.CompilerParams(dimension_semantics=sem, vmem_limit_bytes=VMEM_LIMIT_BYTES)


def _row_tile(t, pref=512):
    return pref if t % pref == 0 else t


def _rms_scale(x):
    return lax.rsqrt(jnp.mean(x * x, axis=-1, keepdims=True) + EPS)


def _silu(x):
    return x / (1.0 + jnp.exp(-x))


def _in_proj_kernel(x_ref, g_ref, wz_ref, wx_ref, wd_ref, z_ref, xbc_ref, dt_ref):
    x = x_ref[...]
    xn = (x * _rms_scale(x) * g_ref[...]).astype(BF16)
    z_ref[...] = jnp.dot(xn, wz_ref[...], preferred_element_type=F32)
    xbc_ref[...] = jnp.dot(xn, wx_ref[...], preferred_element_type=F32)
    dt_ref[...] = jnp.dot(xn, wd_ref[...], preferred_element_type=F32)


def in_proj(x, g, w_z, w_xbc, w_dt):
    t, k = x.shape
    tm = _row_tile(t, 256)
    ws = (w_z, w_xbc, w_dt)
    return pl.pallas_call(
        _in_proj_kernel,
        grid=(t // tm,),
        in_specs=[pl.BlockSpec((tm, k), lambda i: (i, 0)), pl.BlockSpec((1, k), lambda i: (0, 0))]
        + [pl.BlockSpec(w.shape, lambda i: (0, 0)) for w in ws],
        out_specs=[pl.BlockSpec((tm, w.shape[1]), lambda i: (i, 0)) for w in ws],
        out_shape=[jax.ShapeDtypeStruct((t, w.shape[1]), F32) for w in ws],
        compiler_params=_cparams("parallel"),
        name="in_proj",
    )(x, g.reshape(1, k), *ws)


def _kvq_kernel(x_ref, gkv_ref, gq_ref, wkv_ref, wq_ref, kv_ref, q_ref):
    x = x_ref[...]
    xs = x * _rms_scale(x)
    kv_ref[...] = jnp.dot((xs * gkv_ref[...]).astype(BF16), wkv_ref[...], preferred_element_type=F32)
    q = jnp.dot((xs * gq_ref[...]).astype(BF16), wq_ref[...], preferred_element_type=F32).astype(q_ref.dtype)
    for p in range(q_ref.shape[0]):
        q_ref[p] = q[:, p * LANES:(p + 1) * LANES]


def kv_q_proj(x, g_kv, g_q, w_kv, w_q):
    t, k = x.shape
    tm = _row_tile(t)
    n_pairs = w_q.shape[1] // LANES
    return pl.pallas_call(
        _kvq_kernel,
        grid=(t // tm,),
        in_specs=[pl.BlockSpec((tm, k), lambda i: (i, 0)),
                  pl.BlockSpec((1, k), lambda i: (0, 0)), pl.BlockSpec((1, k), lambda i: (0, 0)),
                  pl.BlockSpec(w_kv.shape, lambda i: (0, 0)), pl.BlockSpec(w_q.shape, lambda i: (0, 0))],
        out_specs=[pl.BlockSpec((tm, w_kv.shape[1]), lambda i: (i, 0)),
                   pl.BlockSpec((n_pairs, tm, LANES), lambda i: (0, i, 0))],
        out_shape=[jax.ShapeDtypeStruct((t, w_kv.shape[1]), F32),
                   jax.ShapeDtypeStruct((n_pairs, t, LANES), BF16)],
        compiler_params=_cparams("parallel"),
        name="kv_q_proj",
    )(x, g_kv.reshape(1, k), g_q.reshape(1, k), w_kv, w_q)


def _matmul_residual_kernel(a_ref, w_ref, r_ref, o_ref):
    o_ref[...] = r_ref[...] + jnp.dot(a_ref[...], w_ref[...], preferred_element_type=F32)


def matmul_residual(a, w, r):
    t, k = a.shape
    n = w.shape[1]
    tm = _row_tile(t)
    return pl.pallas_call(
        _matmul_residual_kernel,
        grid=(t // tm,),
        in_specs=[pl.BlockSpec((tm, k), lambda i: (i, 0)),
                  pl.BlockSpec((k, n), lambda i: (0, 0)),
                  pl.BlockSpec((tm, n), lambda i: (i, 0))],
        out_specs=pl.BlockSpec((tm, n), lambda i: (i, 0)),
        out_shape=jax.ShapeDtypeStruct((t, n), F32),
        compiler_params=_cparams("parallel"),
        name="matmul_residual",
    )(a, w, r)


def _swiglu_accumulate(xb, wg_ref, wu_ref, wd_ref, o_ref):
    gg = jnp.dot(xb, wg_ref[0], preferred_element_type=F32)
    uu = jnp.dot(xb, wu_ref[0], preferred_element_type=F32)
    act = (_silu(gg) * uu).astype(BF16)
    o_ref[...] += jnp.dot(act, wd_ref[0], preferred_element_type=F32)


def _ffn_kernel(x_ref, g_ref, wg_ref, wu_ref, wd_ref, o_ref, xb_ref):
    @pl.when(pl.program_id(1) == 0)
    def _():
        x = x_ref[...]
        xb_ref[...] = (x * _rms_scale(x) * g_ref[...]).astype(BF16)
        o_ref[...] = x

    _swiglu_accumulate(xb_ref[...], wg_ref, wu_ref, wd_ref, o_ref)


def ffn(x, g, w_gu, w_down):
    t, d = x.shape
    tm = _row_tile(t)
    n_ff = D_FF // FF_TILE
    return pl.pallas_call(
        _ffn_kernel,
        grid=(t // tm, n_ff),
        in_specs=[pl.BlockSpec((tm, d), lambda i, j: (i, 0)),
                  pl.BlockSpec((1, d), lambda i, j: (0, 0)),
                  pl.BlockSpec((1, d, FF_TILE), lambda i, j: (0, 0, j)),
                  pl.BlockSpec((1, d, FF_TILE), lambda i, j: (0, 0, j + n_ff)),
                  pl.BlockSpec((1, FF_TILE, d), lambda i, j: (0, j, 0))],
        out_specs=pl.BlockSpec((tm, d), lambda i, j: (i, 0)),
        out_shape=jax.ShapeDtypeStruct((t, d), F32),
        scratch_shapes=[pltpu.VMEM((tm, d), BF16)],
        compiler_params=_cparams("parallel", "arbitrary"),
        name="ffn",
    )(x, g.reshape(1, d), w_gu, w_gu, w_down)


def _expert_ffn_kernel(te_ref, na_ref, x_ref, wg_ref, wu_ref, wd_ref, o_ref, xb_ref):
    @pl.when(pl.program_id(1) == 0)
    def _():
        xb_ref[...] = x_ref[...].astype(BF16)
        o_ref[...] = jnp.zeros_like(o_ref)

    @pl.when(pl.program_id(0) < na_ref[0])
    def _():
        _swiglu_accumulate(xb_ref[...], wg_ref, wu_ref, wd_ref, o_ref)


def expert_ffn(xs, tile_expert, n_active, w_gu, w_down, tm):
    rows, d = xs.shape
    n_ff = D_FF // FF_TILE

    def row_map(i, j, te, na):
        return (jnp.minimum(i, na[0] - 1), 0)

    def ff_index(i, j, na):
        return jnp.where(i < na[0], j, n_ff - 1)

    return pl.pallas_call(
        _expert_ffn_kernel,
        grid_spec=pltpu.PrefetchScalarGridSpec(
            num_scalar_prefetch=2,
            grid=(rows // tm, n_ff),
            in_specs=[pl.BlockSpec((tm, d), row_map),
                      pl.BlockSpec((1, d, FF_TILE), lambda i, j, te, na: (te[i], 0, ff_index(i, j, na))),
                      pl.BlockSpec((1, d, FF_TILE), lambda i, j, te, na: (te[i], 0, ff_index(i, j, na) + n_ff)),
                      pl.BlockSpec((1, FF_TILE, d), lambda i, j, te, na: (te[i], ff_index(i, j, na), 0))],
            out_specs=pl.BlockSpec((tm, d), lambda i, j, te, na: (i, 0)),
            scratch_shapes=[pltpu.VMEM((tm, d), BF16)]),
        out_shape=jax.ShapeDtypeStruct((rows, d), F32),
        compiler_params=_cparams("arbitrary", "arbitrary"),
        name="expert_ffn",
    )(tile_expert, n_active, xs, w_gu, w_gu, w_down)


def _router_kernel(x_ref, g_ref, whi_ref, wlo_ref, gate_ref, ids_ref, cnt_ref):
    x = x_ref[...]
    xn = x * _rms_scale(x) * g_ref[...]
    x_hi = xn.astype(BF16)
    x_lo = (xn - x_hi.astype(F32)).astype(BF16)
    logits = (jnp.dot(x_hi, whi_ref[...], preferred_element_type=F32)
              + jnp.dot(x_lo, whi_ref[...], preferred_element_type=F32)
              + jnp.dot(x_hi, wlo_ref[...], preferred_element_type=F32))
    lane = lax.broadcasted_iota(I32, logits.shape, 1)
    logits = jnp.where(lane < N_EXPERTS, logits, -jnp.inf)
    m1 = jnp.max(logits, axis=-1, keepdims=True)
    i1 = jnp.min(jnp.where(logits == m1, lane, LANES), axis=-1, keepdims=True)
    rest = jnp.where(lane == i1, -jnp.inf, logits)
    m2 = jnp.max(rest, axis=-1, keepdims=True)
    i2 = jnp.min(jnp.where(rest == m2, lane, LANES), axis=-1, keepdims=True)
    e2 = jnp.exp(m2 - m1)
    w1 = 1.0 / (1.0 + e2)
    w2 = e2 / (1.0 + e2)
    gate_ref[...] = jnp.where(lane == 0, w1, jnp.where(lane == 1, w2, 0.0))
    ids_ref[...] = jnp.where(lane == 0, i1, jnp.where(lane == 1, i2, 0))
    chosen = jnp.where(jnp.logical_or(lane == i1, lane == i2), 1.0, 0.0)

    @pl.when(pl.program_id(0) == 0)
    def _():
        cnt_ref[...] = jnp.zeros_like(cnt_ref)

    cnt_ref[...] += jnp.broadcast_to(jnp.sum(chosen, axis=0, keepdims=True), cnt_ref.shape)


def router(x, g, w_router):
    t, d = x.shape
    tm = _row_tile(t)
    w_hi = w_router.astype(BF16)
    w_lo = (w_router - w_hi.astype(F32)).astype(BF16)
    return pl.pallas_call(
        _router_kernel,
        grid=(t // tm,),
        in_specs=[pl.BlockSpec((tm, d), lambda i: (i, 0)),
                  pl.BlockSpec((1, d), lambda i: (0, 0)),
                  pl.BlockSpec((d, LANES), lambda i: (0, 0)),
                  pl.BlockSpec((d, LANES), lambda i: (0, 0))],
        out_specs=[pl.BlockSpec((tm, LANES), lambda i: (i, 0)),
                   pl.BlockSpec((tm, LANES), lambda i: (i, 0)),
                   pl.BlockSpec((SUBLANES, LANES), lambda i: (0, 0))],
        out_shape=[jax.ShapeDtypeStruct((t, LANES), F32),
                   jax.ShapeDtypeStruct((t, LANES), I32),
                   jax.ShapeDtypeStruct((SUBLANES, LANES), F32)],
        compiler_params=_cparams("arbitrary"),
        name="router",
    )(x, g.reshape(1, d), w_hi, w_lo)


def _plan_kernel(ids_ref, base_ref, pos_ref, carry_ref):
    @pl.when(pl.program_id(0) == 0)
    def _():
        carry_ref[...] = jnp.zeros_like(carry_ref)

    ids = ids_ref[...]
    tt = ids.shape[0]
    lane = lax.broadcasted_iota(I32, ids.shape, 1)
    pick0 = lane == ids[:, 0:1]
    pick1 = lane == ids[:, 1:2]
    chosen = jnp.where(jnp.logical_or(pick0, pick1), 1.0, 0.0)
    ri = lax.broadcasted_iota(I32, (tt, tt), 0)
    ci = lax.broadcasted_iota(I32, (tt, tt), 1)
    before = jnp.where(ri > ci, 1.0, 0.0).astype(BF16)
    rank = jnp.dot(before, chosen.astype(BF16), preferred_element_type=F32) + carry_ref[0:1, :]
    carry_ref[...] += jnp.broadcast_to(jnp.sum(chosen, axis=0, keepdims=True), carry_ref.shape)
    dest = base_ref[0:1, :] + rank
    p0 = jnp.sum(jnp.where(pick0, dest, 0.0), axis=-1, keepdims=True)
    p1 = jnp.sum(jnp.where(pick1, dest, 0.0), axis=-1, keepdims=True)
    pos_ref[...] = jnp.where(lane == 0, p0, jnp.where(lane == 1, p1, 0.0)).astype(I32)


def dispatch_plan(ids, base):
    t = ids.shape[0]
    tt = _row_tile(t)
    return pl.pallas_call(
        _plan_kernel,
        grid=(t // tt,),
        in_specs=[pl.BlockSpec((tt, LANES), lambda i: (i, 0)),
                  pl.BlockSpec((SUBLANES, LANES), lambda i: (0, 0))],
        out_specs=pl.BlockSpec((tt, LANES), lambda i: (i, 0)),
        out_shape=jax.ShapeDtypeStruct((t, LANES), I32),
        scratch_shapes=[pltpu.VMEM((SUBLANES, LANES), F32)],
        compiler_params=_cparams("arbitrary"),
        name="dispatch_plan",
    )(ids, base)


def _load_positions(pos_hbm, pos_smem, sem):
    chunk = pos_smem.shape[0]
    start = pl.multiple_of(pl.program_id(0) * chunk, chunk)
    copy = pltpu.make_async_copy(pos_hbm.at[pl.ds(start, chunk)], pos_smem, sem)
    copy.start()
    return copy


def _scatter_kernel(x_ref, g_ref, pos_hbm, init_hbm, xs_hbm, xn_scr, pos_smem, sems):
    del init_hbm
    tt = x_ref.shape[0]
    pos_copy = _load_positions(pos_hbm, pos_smem, sems.at[0])
    x = x_ref[...]
    xn_scr[...] = x * _rms_scale(x) * g_ref[...]
    pos_copy.wait()

    def issue(t, carry):
        for k in range(TOP_K):
            row = pos_smem[TOP_K * t + k]
            pltpu.make_async_copy(xn_scr.at[pl.ds(t, 1)], xs_hbm.at[pl.ds(row, 1)], sems.at[1]).start()
        return carry

    lax.fori_loop(0, tt, issue, 0, unroll=8)
    done = xs_hbm.at[pl.ds(0, TOP_K * tt)]
    pltpu.make_async_copy(done, done, sems.at[1]).wait()


def scatter_rows(x, g, pos_flat, n_rows):
    t, d = x.shape
    tt = _row_tile(t)
    chunk = pos_flat.shape[0] // (t // tt)
    return pl.pallas_call(
        _scatter_kernel,
        grid=(t // tt,),
        in_specs=[pl.BlockSpec((tt, d), lambda i: (i, 0)),
                  pl.BlockSpec((1, d), lambda i: (0, 0)),
                  pl.BlockSpec(memory_space=pl.ANY),
                  pl.BlockSpec(memory_space=pl.ANY)],
        out_specs=pl.BlockSpec(memory_space=pl.ANY),
        out_shape=jax.ShapeDtypeStruct((n_rows, d), F32),
        scratch_shapes=[pltpu.VMEM((tt, d), F32), pltpu.SMEM((chunk,), I32),
                        pltpu.SemaphoreType.DMA((2,))],
        input_output_aliases={3: 0},
        compiler_params=_cparams("arbitrary"),
        name="scatter_rows",
    )(x, g.reshape(1, d), pos_flat, jnp.zeros((n_rows, d), F32))


def _combine_kernel(h_ref, gate_ref, fg_ref, pos_hbm, ys_hbm, o_ref, ybuf, pos_smem, sems):
    tt = h_ref.shape[0]
    _load_positions(pos_hbm, pos_smem, sems.at[0]).wait()

    def issue(t, carry):
        for k in range(TOP_K):
            row = pos_smem[TOP_K * t + k]
            pltpu.make_async_copy(ys_hbm.at[pl.ds(row, 1)], ybuf.at[k, pl.ds(t, 1)], sems.at[1]).start()
        return carry

    lax.fori_loop(0, tt, issue, 0, unroll=8)
    pltpu.make_async_copy(ybuf, ybuf, sems.at[1]).wait()
    gate = gate_ref[...]
    y = h_ref[...] + gate[:, 0:1] * ybuf[0] + gate[:, 1:2] * ybuf[1]
    o_ref[...] = y * _rms_scale(y) * fg_ref[...]


def combine_rows(h, gates, final_gain, pos_flat, ys):
    t, d = h.shape
    tt = _row_tile(t)
    chunk = pos_flat.shape[0] // (t // tt)
    return pl.pallas_call(
        _combine_kernel,
        grid=(t // tt,),
        in_specs=[pl.BlockSpec((tt, d), lambda i: (i, 0)),
                  pl.BlockSpec((tt, LANES), lambda i: (i, 0)),
                  pl.BlockSpec((1, d), lambda i: (0, 0)),
                  pl.BlockSpec(memory_space=pl.ANY),
                  pl.BlockSpec(memory_space=pl.ANY)],
        out_specs=pl.BlockSpec((tt, d), lambda i: (i, 0)),
        out_shape=jax.ShapeDtypeStruct((t, d), F32),
        scratch_shapes=[pltpu.VMEM((TOP_K, tt, d), F32), pltpu.SMEM((chunk,), I32),
                        pltpu.SemaphoreType.DMA((2,))],
        compiler_params=_cparams("arbitrary"),
        name="combine_rows",
    )(h, gates, final_gain.reshape(1, d), pos_flat, ys)


def moe_and_final_norm(h, g, w_router, w_gu, w_down, final_gain):
    t, d = h.shape
    tt = _row_tile(t)
    tm = 512 if t >= 4096 else 128
    n_rows = TOP_K * t + N_EXPERTS * tm
    gates, ids, counts = router(h, g, w_router)

    cnt = counts[0, :N_EXPERTS].astype(I32)
    tiles = (cnt + tm - 1) // tm
    ends = jnp.cumsum(tiles)
    base = jnp.zeros((SUBLANES, LANES), F32).at[:, :N_EXPERTS].set(((ends - tiles) * tm).astype(F32))
    n_active = ends[-1:]
    tile_idx = jnp.arange(n_rows // tm, dtype=I32)
    owner = jnp.sum(tile_idx[:, None] >= ends[None, :], axis=1).astype(I32)
    last_owner = jnp.sum(n_active - 1 >= ends).astype(I32)
    tile_expert = jnp.where(tile_idx < n_active, owner, last_owner)

    pos = dispatch_plan(ids, base)
    chunk = max(TOP_K * tt, SMEM_I32_CHUNK)
    pos_flat = pos[:, :TOP_K].reshape(t // tt, TOP_K * tt)
    pos_flat = jnp.pad(pos_flat, ((0, 0), (0, chunk - TOP_K * tt))).reshape(-1)
    xs = scatter_rows(h, g, pos_flat, n_rows)
    ys = expert_ffn(xs, tile_expert, n_active, w_gu, w_down, tm)
    return combine_rows(h, gates, final_gain, pos_flat, ys)


def _split3(v):
    hi = v.astype(BF16)
    r1 = v - hi.astype(F32)
    mid = r1.astype(BF16)
    lo = (r1 - mid.astype(F32)).astype(BF16)
    return hi, mid, lo


def _ssd_kernel(xbc_ref, dt_ref, z_ref, cbuf_ref, h0_ref, cw_ref, cb_ref, dtb_ref, alog_ref,
                dskip_ref, gn_ref, expand_ref, yn_ref, hfin_ref, h_scr, xext_scr, *, seq_valid):
    q = SSD_BLOCK
    c = pl.program_id(1)
    halo = SUBLANES

    @pl.when(c == 0)
    def _():
        h_scr[...] = h0_ref[0]
        xext_scr[0:halo, :] = cbuf_ref[0]

    xext_scr[halo:halo + q, :] = xbc_ref[0]
    conv = cb_ref[...]
    for k in range(CONV_WIDTH):
        start = halo - (CONV_WIDTH - 1) + k
        conv = conv + xext_scr[start:start + q, :] * cw_ref[k:k + 1, :]
    xext_scr[0:halo, :] = xext_scr[q:q + halo, :]
    xbc = _silu(conv)
    xs = xbc[:, :D_INNER]
    b_all = xbc[:, D_INNER:D_INNER + SSM_GROUPS * D_STATE]
    c_all = xbc[:, D_INNER + SSM_GROUPS * D_STATE:]

    v = dt_ref[0] + dtb_ref[...]
    dt = jnp.maximum(v, 0.0) + jnp.log1p(jnp.exp(-jnp.abs(v)))
    row = lax.broadcasted_iota(I32, (q, LANES), 0)
    if seq_valid is not None:
        dt = jnp.where(row + c * q < seq_valid, dt, 0.0)
    a_neg = -jnp.exp(alog_ref[...])
    dta = dt * a_neg

    ti = lax.broadcasted_iota(I32, (q, q), 0)
    si = lax.broadcasted_iota(I32, (q, q), 1)
    causal = ti >= si
    tri = jnp.where(causal, 1.0, 0.0).astype(BF16)
    a_cs = sum(jnp.dot(tri, part, preferred_element_type=F32) for part in _split3(dta))
    a_last = a_cs[q - 1:q, :]
    a_cs_t = a_cs.T

    stacked = jnp.concatenate([dt, jnp.exp(a_cs), jnp.exp(a_last - a_cs),
                               jnp.broadcast_to(jnp.exp(a_last), (SUBLANES, LANES))], axis=0)
    s_hi = stacked.astype(BF16)
    s_lo = (stacked - s_hi.astype(F32)).astype(BF16)
    expand = expand_ref[...]
    expanded = (jnp.dot(s_hi, expand, preferred_element_type=F32)
                + jnp.dot(s_lo, expand, preferred_element_type=F32))
    dt_e = expanded[0:q]
    decay_in_e = expanded[q:2 * q]
    decay_out_e = expanded[2 * q:3 * q]
    decay_blk_e = expanded[3 * q:3 * q + 1]

    xdt = xs * dt_e
    lane_d = lax.broadcasted_iota(I32, (q, D_INNER), 1)
    low_half = (lane_d % LANES) < SSM_HEAD_DIM
    xdt_lo = jnp.where(low_half, xdt, 0.0).astype(BF16)
    xdt_hi = jnp.where(low_half, 0.0, xdt).astype(BF16)
    wx = (xdt * decay_out_e).astype(BF16)

    y_groups = []
    for g in range(SSM_GROUPS):
        bg = b_all[:, g * D_STATE:(g + 1) * D_STATE]
        cg = c_all[:, g * D_STATE:(g + 1) * D_STATE].astype(BF16)
        bg_bf = bg.astype(BF16)
        cbm = lax.dot_general(cg, bg_bf, (((1,), (1,)), ((), ())), preferred_element_type=F32)
        cols = slice(g * GROUP_COLS, (g + 1) * GROUP_COLS)
        h_g = h_scr[g]
        y_g = jnp.dot(cg, h_g.astype(BF16), preferred_element_type=F32) * decay_in_e[:, cols]
        pair_out = []
        for pr in range(HEADS_PER_GROUP // 2):
            acc = None
            for par in range(2):
                h = g * HEADS_PER_GROUP + 2 * pr + par
                seg = (jnp.broadcast_to(a_cs[:, h:h + 1], (q, q))
                       - jnp.broadcast_to(a_cs_t[h:h + 1, :], (q, q)))
                m = (cbm * jnp.exp(jnp.where(causal, seg, -jnp.inf))).astype(BF16)
                pcols = slice((h // 2) * LANES, (h // 2 + 1) * LANES)
                src = xdt_lo if par == 0 else xdt_hi
                part = jnp.dot(m, src[:, pcols], preferred_element_type=F32)
                acc = part if acc is None else acc + part
            pair_out.append(acc)
        y_groups.append(y_g + jnp.concatenate(pair_out, axis=1))
        h_scr[g] = (h_g * decay_blk_e[:, cols]
                    + jnp.dot(bg.T.astype(BF16), wx[:, cols], preferred_element_type=F32))

    y = jnp.concatenate(y_groups, axis=1) + dskip_ref[...] * xs
    yz = y * _silu(z_ref[0])
    yn_ref[0] = (yz * _rms_scale(yz) * gn_ref[...]).astype(yn_ref.dtype)

    @pl.when(c == pl.num_programs(1) - 1)
    def _():
        hfin_ref[0] = h_scr[...]


def _state_to_kernel_layout(h):
    b = h.shape[0]
    h = h.reshape(b, SSM_GROUPS, HEADS_PER_GROUP, SSM_HEAD_DIM, D_STATE)
    return h.transpose(0, 1, 4, 2, 3).reshape(b, SSM_GROUPS, D_STATE, GROUP_COLS)


def _state_from_kernel_layout(h):
    b = h.shape[0]
    h = h.reshape(b, SSM_GROUPS, D_STATE, HEADS_PER_GROUP, SSM_HEAD_DIM)
    return h.transpose(0, 1, 3, 4, 2).reshape(b, N_SSM_HEADS, SSM_HEAD_DIM, D_STATE)


def _pad_lanes(v):
    return jnp.pad(v.astype(F32), (0, LANES - v.shape[0])).reshape(1, LANES)


def ssd_mixer(xbc, dt_raw, z, conv_buf, h0, conv_w, conv_b, dt_bias, a_log, d_skip, gnorm):
    bsz, seq, _ = xbc.shape
    q = SSD_BLOCK
    seq_pad = -(-seq // q) * q
    seq_valid = None
    if seq_pad != seq:
        pad = ((0, 0), (0, seq_pad - seq), (0, 0))
        xbc, dt_raw, z = jnp.pad(xbc, pad), jnp.pad(dt_raw, pad), jnp.pad(z, pad)
        seq_valid = seq
    cbuf = jnp.pad(conv_buf, ((0, 0), (SUBLANES - (CONV_WIDTH - 1), 0), (0, 0)))
    expand = (jnp.arange(D_INNER)[None, :] // SSM_HEAD_DIM == jnp.arange(LANES)[:, None]).astype(BF16)
    d_e = jnp.repeat(d_skip.astype(F32), SSM_HEAD_DIM).reshape(1, D_INNER)
    state_spec = pl.BlockSpec((1, SSM_GROUPS, D_STATE, GROUP_COLS), lambda b, c: (b, 0, 0, 0))

    def const(shape):
        return pl.BlockSpec(shape, lambda b, c: (0,) * len(shape))

    yn, h_fin = pl.pallas_call(
        functools.partial(_ssd_kernel, seq_valid=seq_valid),
        grid=(bsz, seq_pad // q),
        in_specs=[pl.BlockSpec((1, q, CONV_DIM), lambda b, c: (b, c, 0)),
                  pl.BlockSpec((1, q, LANES), lambda b, c: (b, c, 0)),
                  pl.BlockSpec((1, q, D_INNER), lambda b, c: (b, c, 0)),
                  pl.BlockSpec((1, SUBLANES, CONV_DIM), lambda b, c: (b, 0, 0)),
                  state_spec,
                  const((CONV_WIDTH, CONV_DIM)), const((1, CONV_DIM)), const((1, LANES)), const((1, LANES)),
                  const((1, D_INNER)), const((1, D_INNER)), const((LANES, D_INNER))],
        out_specs=[pl.BlockSpec((1, q, D_INNER), lambda b, c: (b, c, 0)), state_spec],
        out_shape=[jax.ShapeDtypeStruct((bsz, seq_pad, D_INNER), BF16),
                   jax.ShapeDtypeStruct((bsz, SSM_GROUPS, D_STATE, GROUP_COLS), F32)],
        scratch_shapes=[pltpu.VMEM((SSM_GROUPS, D_STATE, GROUP_COLS), F32),
                        pltpu.VMEM((q + SUBLANES, CONV_DIM), F32)],
        compiler_params=_cparams("parallel", "arbitrary"),
        name="ssd_mixer",
    )(xbc, dt_raw, z, cbuf, _state_to_kernel_layout(h0.astype(F32)), conv_w.astype(F32),
      conv_b.astype(F32).reshape(1, CONV_DIM), _pad_lanes(dt_bias), _pad_lanes(a_log), d_e,
      gnorm.astype(F32).reshape(1, D_INNER), expand)
    return yn[:, :seq], _state_from_kernel_layout(h_fin)


_ALIBI_SLOPES = (2.0 ** (-8.0 * np.arange(1, N_Q_HEADS + 1, dtype=np.float32) / N_Q_HEADS)).astype(np.float32)


def _alibi_slope(h):
    return float(_ALIBI_SLOPES[h])


def _attn_kernel(sink_ref, q_ref, k_ref, v_ref, o_ref, *, lbq, chunk, past_valid):
    n = pl.program_id(1)
    w = WINDOW + lbq
    base = pl.multiple_of(n * lbq, lbq)
    kband = k_ref[0, pl.ds(base, w), :].astype(F32)
    vband = v_ref[0, pl.ds(base, w), :].astype(F32)

    qi = lax.broadcasted_iota(I32, (2 * lbq, w), 0) % lbq
    si = lax.broadcasted_iota(I32, (2 * lbq, w), 1)
    dist = jnp.abs(WINDOW + qi - si).astype(F32)
    lo = (qi // chunk) * chunk
    valid = jnp.logical_and(si >= lo, si < lo + WINDOW + chunk)
    if not past_valid:
        valid = jnp.logical_and(valid, si + base >= WINDOW)
    first_pair = lax.broadcasted_iota(I32, (2 * lbq, 1), 0) < lbq
    lane = lax.broadcasted_iota(I32, (w, LANES), 1)
    low_half = lane < ATTN_HEAD_DIM

    for kv in range(N_KV_HEADS):
        pcols = slice((kv // 2) * LANES, (kv // 2 + 1) * LANES)
        kpair, vpair = kband[:, pcols], vband[:, pcols]
        kswap = pltpu.roll(kpair, ATTN_HEAD_DIM, axis=1)
        vswap = pltpu.roll(vpair, ATTN_HEAD_DIM, axis=1)
        in_low = kv % 2 == 0
        k_lo = jnp.where(low_half, kpair if in_low else kswap, 0.0).astype(BF16)
        k_hi = jnp.where(low_half, 0.0, kswap if in_low else kpair).astype(BF16)
        v_lo = jnp.where(low_half, vpair if in_low else vswap, 0.0).astype(BF16)
        v_hi = jnp.where(low_half, 0.0, vswap if in_low else vpair).astype(BF16)
        qs = q_ref[2 * kv:2 * kv + 2, 0].reshape(2 * lbq, LANES)
        out = None
        for par, (kx, vx) in enumerate(((k_lo, v_lo), (k_hi, v_hi))):
            h0, h1 = 4 * kv + par, 4 * kv + 2 + par
            slope = jnp.where(first_pair, _alibi_slope(h0), _alibi_slope(h1))
            sink = jnp.where(first_pair, sink_ref[h0], sink_ref[h1])
            s = lax.dot_general(qs, kx, (((1,), (1,)), ((), ())), preferred_element_type=F32)
            s = s * ATTN_SCALE - slope * dist
            s = jnp.where(valid, s, NEG_INF)
            m = jnp.maximum(jnp.max(s, axis=-1, keepdims=True), sink)
            p = jnp.exp(s - m)
            denom = jnp.sum(p, axis=-1, keepdims=True) + jnp.exp(sink - m)
            pn = (p / denom).astype(BF16)
            part = jnp.dot(pn, vx, preferred_element_type=F32)
            out = part if out is None else out + part
        o_ref[0, :, (2 * kv) * LANES:(2 * kv + 1) * LANES] = out[:lbq].astype(o_ref.dtype)
        o_ref[0, :, (2 * kv + 1) * LANES:(2 * kv + 2) * LANES] = out[lbq:].astype(o_ref.dtype)


def window_attention(q_pairs, k_all, v_all, sinks, *, lbq, chunk, past_valid):
    n_pairs, bsz, seq, _ = q_pairs.shape
    rows = k_all.shape[1]
    return pl.pallas_call(
        functools.partial(_attn_kernel, lbq=lbq, chunk=chunk, past_valid=past_valid),
        grid_spec=pltpu.PrefetchScalarGridSpec(
            num_scalar_prefetch=1,
            grid=(bsz, seq // lbq),
            in_specs=[pl.BlockSpec((n_pairs, 1, lbq, LANES), lambda b, n, s: (0, b, n, 0)),
                      pl.BlockSpec((1, rows, KV_DIM), lambda b, n, s: (b, 0, 0)),
                      pl.BlockSpec((1, rows, KV_DIM), lambda b, n, s: (b, 0, 0))],
            out_specs=pl.BlockSpec((1, lbq, D_MODEL), lambda b, n, s: (b, n, 0))),
        out_shape=jax.ShapeDtypeStruct((bsz, seq, D_MODEL), BF16),
        compiler_params=_cparams("parallel", "arbitrary"),
        name="window_attention",
    )(sinks.astype(F32), q_pairs, k_all, v_all)


def _trunk(x, conv_buf, ssm_state, k_past, v_past, past_valid, p):
    bsz, seq, d = x.shape
    t = bsz * seq
    x2 = x.reshape(t, d)

    z, xbc, dt_raw = in_proj(x2, p['norm_mix'][0], p['w_z'], p['w_xbc'], p['w_dt'])
    xbc3 = xbc.reshape(bsz, seq, CONV_DIM)
    new_conv = jnp.concatenate([conv_buf.astype(F32), xbc3], axis=1)[:, seq:]
    yn, new_ssm = ssd_mixer(xbc3, dt_raw.reshape(bsz, seq, LANES), z.reshape(bsz, seq, D_INNER),
                            conv_buf.astype(F32), ssm_state, p['ssm_conv_w'], p['ssm_conv_b'],
                            p['ssm_dt_bias'], p['ssm_a_log'], p['ssm_d'], p['ssm_norm'])
    h = matmul_residual(yn.reshape(t, D_INNER), p['ssm_out_w'], x2)

    h = ffn(h, p['norm_ffn'][0], p['ffn_w_gate_up'], p['ffn_w_down'])

    kv, q_pairs = kv_q_proj(h, p['kv_norm'], p['norm_mix'][1], p['w_kv'], p['w_q'])
    kv = kv.reshape(bsz, seq, 2 * KV_DIM)
    k_new, v_new = kv[..., :KV_DIM], kv[..., KV_DIM:]
    k_all = jnp.concatenate([k_past.reshape(bsz, WINDOW, KV_DIM), k_new], axis=1).astype(BF16)
    v_all = jnp.concatenate([v_past.reshape(bsz, WINDOW, KV_DIM), v_new], axis=1).astype(BF16)

    chunk = min(ATTN_CHUNK, seq)
    lbq = 2 * chunk if seq % (2 * chunk) == 0 else chunk
    q_pairs = q_pairs.reshape(D_MODEL // LANES, bsz, seq, LANES)
    o = window_attention(q_pairs, k_all, v_all, p['attn_sinks'], lbq=lbq, chunk=chunk, past_valid=past_valid)
    h = matmul_residual(o.reshape(t, D_MODEL), p['w_o'], h)

    y = moe_and_final_norm(h, p['norm_ffn'][1], p['moe_router'], p['moe_w_gate_up'], p['moe_w_down'],
                           p['final_norm'])

    k4 = k_new.reshape(bsz, seq, N_KV_HEADS, ATTN_HEAD_DIM)
    v4 = v_new.reshape(bsz, seq, N_KV_HEADS, ATTN_HEAD_DIM)
    return y.reshape(bsz, seq, d), new_conv[None], new_ssm[None], k4, v4


def kernel(x_prompt, x_sample, state_conv, state_ssm, cache_k, cache_v, norm_mix, norm_ffn, ssm_in_w, ssm_conv_w,
           ssm_conv_b, ssm_dt_bias, ssm_a_log, ssm_d, ssm_norm, ssm_out_w, kv_norm, w_kv, w_q, attn_sinks, w_o,
           ffn_w_gate_up, ffn_w_down, moe_router, moe_w_gate_up, moe_w_down, final_norm):
    in_w = ssm_in_w[0]
    n_dt = N_SSM_HEADS
    p = {
        'norm_mix': norm_mix, 'norm_ffn': norm_ffn,
        'w_z': in_w[:, :D_INNER].astype(BF16),
        'w_xbc': in_w[:, D_INNER:D_INNER + CONV_DIM].astype(BF16),
        'w_dt': jnp.pad(in_w[:, D_INNER + CONV_DIM:], ((0, 0), (0, LANES - n_dt))).astype(BF16),
        'ssm_conv_w': ssm_conv_w[0], 'ssm_conv_b': ssm_conv_b[0], 'ssm_dt_bias': ssm_dt_bias[0],
        'ssm_a_log': ssm_a_log[0], 'ssm_d': ssm_d[0], 'ssm_norm': ssm_norm[0],
        'ssm_out_w': ssm_out_w[0].astype(BF16),
        'kv_norm': kv_norm, 'w_kv': w_kv.astype(BF16), 'w_q': w_q[0].astype(BF16),
        'attn_sinks': attn_sinks[0], 'w_o': w_o[0].astype(BF16),
        'ffn_w_gate_up': ffn_w_gate_up.astype(BF16), 'ffn_w_down': ffn_w_down.astype(BF16),
        'moe_router': jnp.pad(moe_router[0].astype(F32), ((0, 0), (0, LANES - N_EXPERTS))),
        'moe_w_gate_up': moe_w_gate_up[0].astype(BF16), 'moe_w_down': moe_w_down[0].astype(BF16),
        'final_norm': final_norm,
    }
    bsz, seq_p = x_prompt.shape[:2]
    dt = x_prompt.dtype
    zero_conv = jnp.zeros((bsz, CONV_WIDTH - 1, CONV_DIM), dt)
    zero_ssm = jnp.zeros((bsz, N_SSM_HEADS, SSM_HEAD_DIM, D_STATE), dt)
    zero_kv = jnp.zeros((bsz, WINDOW, N_KV_HEADS, ATTN_HEAD_DIM), dt)
    y_p, conv_p, ssm_p, k_p, v_p = _trunk(x_prompt, zero_conv, zero_ssm, zero_kv, zero_kv, False, p)
    keep = min(WINDOW, seq_p)
    y_s, conv_s, ssm_s, k_s, v_s = _trunk(x_sample, state_conv[0], state_ssm[0], cache_k, cache_v, True, p)
    return (y_p, y_s, conv_p, ssm_p, k_p[:, seq_p - keep:], v_p[:, seq_p - keep:], conv_s, ssm_s, k_s, v_s)
```

```python
import functools
import math

import jax
import jax.numpy as jnp
import numpy as np
from jax import lax
from jax.experimental import pallas as pl
from jax.experimental.pallas import tpu as pltpu

F32 = jnp.float32
BF16 = jnp.bfloat16
I32 = jnp.int32

D_MODEL = 1024
EPS = 1e-6
D_INNER = 2048
SSM_HEAD_DIM = 64
N_SSM_HEADS = 32
SSM_GROUPS = 4
HEADS_PER_GROUP = N_SSM_HEADS // SSM_GROUPS
D_STATE = 128
CONV_WIDTH = 4
CONV_DIM = D_INNER + 2 * SSM_GROUPS * D_STATE
ATTN_HEAD_DIM = 64
N_Q_HEADS = 16
N_KV_HEADS = 4
KV_DIM = N_KV_HEADS * ATTN_HEAD_DIM
WINDOW = 128
ATTN_CHUNK = 64
ATTN_SCALE = 1.0 / math.sqrt(ATTN_HEAD_DIM)
NEG_INF = -1e30
D_FF = 2816
N_EXPERTS = 8
TOP_K = 2

LANES = 128
SUBLANES = 8
VMEM_LIMIT_BYTES = 56 * 1024 * 1024
SMEM_I32_CHUNK = 1024

SSD_BLOCK = 128
GROUP_COLS = HEADS_PER_GROUP * SSM_HEAD_DIM
FF_CHUNK = 512


def _cparams(*sem):
    return pltpu.CompilerParams(dimension_semantics=sem, vmem_limit_bytes=VMEM_LIMIT_BYTES)


def _row_tile(t, pref=512):
    return pref if t % pref == 0 else t


def _resident(shape):
    return pl.BlockSpec(shape, lambda *_: (0,) * len(shape), pipeline_mode=pl.Buffered(1))


def _rms_scale(x):
    return lax.rsqrt(jnp.mean(x * x, axis=-1, keepdims=True) + EPS)


def _silu(x):
    return x / (1.0 + jnp.exp(-x))


def _in_proj_kernel(x_ref, g_ref, wz_ref, wx_ref, wd_ref, z_ref, xbc_ref, dt_ref):
    x = x_ref[...]
    xn = (x * _rms_scale(x) * g_ref[...]).astype(BF16)
    z_ref[...] = jnp.dot(xn, wz_ref[...], preferred_element_type=F32)
    xbc_ref[...] = jnp.dot(xn, wx_ref[...], preferred_element_type=F32)
    dt_ref[...] = jnp.dot(xn, wd_ref[...], preferred_element_type=F32)


def in_proj(x, g, w_z, w_xbc, w_dt):
    t, k = x.shape
    tm = _row_tile(t)
    ws = (w_z, w_xbc, w_dt)
    return pl.pallas_call(
        _in_proj_kernel,
        grid=(t // tm,),
        in_specs=[pl.BlockSpec((tm, k), lambda i: (i, 0)), pl.BlockSpec((1, k), lambda i: (0, 0))]
        + [_resident(w.shape) for w in ws],
        out_specs=[pl.BlockSpec((tm, w.shape[1]), lambda i: (i, 0)) for w in ws],
        out_shape=[jax.ShapeDtypeStruct((t, w.shape[1]), F32) for w in ws],
        compiler_params=_cparams("parallel"),
        name="in_proj",
    )(x, g.reshape(1, k), *ws)


def _kvq_kernel(x_ref, gkv_ref, gq_ref, wkv_ref, wq_ref, kv_ref, q_ref):
    x = x_ref[...]
    xs = x * _rms_scale(x)
    kv_ref[...] = jnp.dot((xs * gkv_ref[...]).astype(BF16), wkv_ref[...], preferred_element_type=F32)
    q = jnp.dot((xs * gq_ref[...]).astype(BF16), wq_ref[...], preferred_element_type=F32)
    q = (q * ATTN_SCALE).astype(q_ref.dtype)
    for p in range(q_ref.shape[0]):
        q_ref[p] = q[:, p * LANES:(p + 1) * LANES]


def kv_q_proj(x, g_kv, g_q, w_kv, w_q):
    t, k = x.shape
    tm = _row_tile(t)
    n_pairs = w_q.shape[1] // LANES
    return pl.pallas_call(
        _kvq_kernel,
        grid=(t // tm,),
        in_specs=[pl.BlockSpec((tm, k), lambda i: (i, 0)),
                  pl.BlockSpec((1, k), lambda i: (0, 0)), pl.BlockSpec((1, k), lambda i: (0, 0)),
                  pl.BlockSpec(w_kv.shape, lambda i: (0, 0)), pl.BlockSpec(w_q.shape, lambda i: (0, 0))],
        out_specs=[pl.BlockSpec((tm, w_kv.shape[1]), lambda i: (i, 0)),
                   pl.BlockSpec((n_pairs, tm, LANES), lambda i: (0, i, 0))],
        out_shape=[jax.ShapeDtypeStruct((t, w_kv.shape[1]), F32),
                   jax.ShapeDtypeStruct((n_pairs, t, LANES), BF16)],
        compiler_params=_cparams("parallel"),
        name="kv_q_proj",
    )(x, g_kv.reshape(1, k), g_q.reshape(1, k), w_kv, w_q)


def _matmul_residual_kernel(a_ref, w_ref, r_ref, o_ref):
    o_ref[...] = r_ref[...] + jnp.dot(a_ref[...], w_ref[...], preferred_element_type=F32)


def matmul_residual(a, w, r):
    t, k = a.shape
    n = w.shape[1]
    tm = _row_tile(t)
    return pl.pallas_call(
        _matmul_residual_kernel,
        grid=(t // tm,),
        in_specs=[pl.BlockSpec((tm, k), lambda i: (i, 0)),
                  pl.BlockSpec((k, n), lambda i: (0, 0)),
                  pl.BlockSpec((tm, n), lambda i: (i, 0))],
        out_specs=pl.BlockSpec((tm, n), lambda i: (i, 0)),
        out_shape=jax.ShapeDtypeStruct((t, n), F32),
        compiler_params=_cparams("parallel"),
        name="matmul_residual",
    )(a, w, r)


def _swiglu(xb, wgu_ref, wd_ref):
    acc = None
    for c0 in range(0, D_FF, FF_CHUNK):
        cw = min(FF_CHUNK, D_FF - c0)
        gg = jnp.dot(xb, wgu_ref[0, :, c0:c0 + cw], preferred_element_type=F32)
        uu = jnp.dot(xb, wgu_ref[0, :, D_FF + c0:D_FF + c0 + cw], preferred_element_type=F32)
        act = (_silu(gg) * uu).astype(BF16)
        part = jnp.dot(act, wd_ref[0, c0:c0 + cw, :], preferred_element_type=F32)
        acc = part if acc is None else acc + part
    return acc


def _ffn_kernel(x_ref, g_ref, wgu_ref, wd_ref, o_ref):
    x = x_ref[...]
    xb = (x * _rms_scale(x) * g_ref[...]).astype(BF16)
    o_ref[...] = x + _swiglu(xb, wgu_ref, wd_ref)


def ffn(x, g, w_gu, w_down):
    t, d = x.shape
    tm = _row_tile(t)
    return pl.pallas_call(
        _ffn_kernel,
        grid=(t // tm,),
        in_specs=[pl.BlockSpec((tm, d), lambda i: (i, 0)),
                  pl.BlockSpec((1, d), lambda i: (0, 0)),
                  _resident(w_gu.shape), _resident(w_down.shape)],
        out_specs=pl.BlockSpec((tm, d), lambda i: (i, 0)),
        out_shape=jax.ShapeDtypeStruct((t, d), F32),
        compiler_params=_cparams("parallel"),
        name="ffn",
    )(x, g.reshape(1, d), w_gu, w_down)


def _expert_ffn_kernel(te_ref, na_ref, x_ref, wgu_ref, wd_ref, o_ref):
    active = pl.program_id(0) < na_ref[0]

    @pl.when(active)
    def _():
        o_ref[...] = _swiglu(x_ref[...].astype(BF16), wgu_ref, wd_ref)

    @pl.when(jnp.logical_not(active))
    def _():
        o_ref[...] = jnp.zeros_like(o_ref)


def expert_ffn(xs, tile_expert, n_active, w_gu, w_down, tm):
    rows, d = xs.shape
    return pl.pallas_call(
        _expert_ffn_kernel,
        grid_spec=pltpu.PrefetchScalarGridSpec(
            num_scalar_prefetch=2,
            grid=(rows // tm,),
            in_specs=[pl.BlockSpec((tm, d), lambda i, te, na: (jnp.minimum(i, na[0] - 1), 0)),
                      pl.BlockSpec((1, d, 2 * D_FF), lambda i, te, na: (te[i], 0, 0)),
                      pl.BlockSpec((1, D_FF, d), lambda i, te, na: (te[i], 0, 0))],
            out_specs=pl.BlockSpec((tm, d), lambda i, te, na: (i, 0))),
        out_shape=jax.ShapeDtypeStruct((rows, d), F32),
        compiler_params=_cparams("arbitrary"),
        name="expert_ffn",
    )(tile_expert, n_active, xs, w_gu, w_down)


def _router_kernel(x_ref, g_ref, whi_ref, wlo_ref, gate_ref, ids_ref, cnt_ref):
    x = x_ref[...]
    xn = x * _rms_scale(x) * g_ref[...]
    x_hi = xn.astype(BF16)
    x_lo = (xn - x_hi.astype(F32)).astype(BF16)
    logits = (jnp.dot(x_hi, whi_ref[...], preferred_element_type=F32)
              + jnp.dot(x_lo, whi_ref[...], preferred_element_type=F32)
              + jnp.dot(x_hi, wlo_ref[...], preferred_element_type=F32))
    lane = lax.broadcasted_iota(I32, logits.shape, 1)
    logits = jnp.where(lane < N_EXPERTS, logits, -jnp.inf)
    m1 = jnp.max(logits, axis=-1, keepdims=True)
    i1 = jnp.min(jnp.where(logits == m1, lane, LANES), axis=-1, keepdims=True)
    rest = jnp.where(lane == i1, -jnp.inf, logits)
    m2 = jnp.max(rest, axis=-1, keepdims=True)
    i2 = jnp.min(jnp.where(rest == m2, lane, LANES), axis=-1, keepdims=True)
    e2 = jnp.exp(m2 - m1)
    w1 = 1.0 / (1.0 + e2)
    w2 = e2 / (1.0 + e2)
    gate_ref[...] = jnp.where(lane == 0, w1, jnp.where(lane == 1, w2, 0.0))
    ids_ref[...] = jnp.where(lane == 0, i1, jnp.where(lane == 1, i2, 0))
    chosen = jnp.where(jnp.logical_or(lane == i1, lane == i2), 1.0, 0.0)

    @pl.when(pl.program_id(0) == 0)
    def _():
        cnt_ref[...] = jnp.zeros_like(cnt_ref)

    cnt_ref[...] += jnp.broadcast_to(jnp.sum(chosen, axis=0, keepdims=True), cnt_ref.shape)


def router(x, g, w_router):
    t, d = x.shape
    tm = _row_tile(t)
    w_hi = w_router.astype(BF16)
    w_lo = (w_router - w_hi.astype(F32)).astype(BF16)
    return pl.pallas_call(
        _router_kernel,
        grid=(t // tm,),
        in_specs=[pl.BlockSpec((tm, d), lambda i: (i, 0)),
                  pl.BlockSpec((1, d), lambda i: (0, 0)),
                  pl.BlockSpec((d, LANES), lambda i: (0, 0)),
                  pl.BlockSpec((d, LANES), lambda i: (0, 0))],
        out_specs=[pl.BlockSpec((tm, LANES), lambda i: (i, 0)),
                   pl.BlockSpec((tm, LANES), lambda i: (i, 0)),
                   pl.BlockSpec((SUBLANES, LANES), lambda i: (0, 0))],
        out_shape=[jax.ShapeDtypeStruct((t, LANES), F32),
                   jax.ShapeDtypeStruct((t, LANES), I32),
                   jax.ShapeDtypeStruct((SUBLANES, LANES), F32)],
        compiler_params=_cparams("arbitrary"),
        name="router",
    )(x, g.reshape(1, d), w_hi, w_lo)


def _plan_kernel(ids_ref, base_ref, pos_ref, carry_ref):
    @pl.when(pl.program_id(0) == 0)
    def _():
        carry_ref[...] = jnp.zeros_like(carry_ref)

    ids = ids_ref[...]
    tt = ids.shape[0]
    lane = lax.broadcasted_iota(I32, ids.shape, 1)
    pick0 = lane == ids[:, 0:1]
    pick1 = lane == ids[:, 1:2]
    chosen = jnp.where(jnp.logical_or(pick0, pick1), 1.0, 0.0)
    ri = lax.broadcasted_iota(I32, (tt, tt), 0)
    ci = lax.broadcasted_iota(I32, (tt, tt), 1)
    before = jnp.where(ri > ci, 1.0, 0.0).astype(BF16)
    rank = jnp.dot(before, chosen.astype(BF16), preferred_element_type=F32) + carry_ref[0:1, :]
    carry_ref[...] += jnp.broadcast_to(jnp.sum(chosen, axis=0, keepdims=True), carry_ref.shape)
    dest = base_ref[0:1, :] + rank
    p0 = jnp.sum(jnp.where(pick0, dest, 0.0), axis=-1, keepdims=True)
    p1 = jnp.sum(jnp.where(pick1, dest, 0.0), axis=-1, keepdims=True)
    pos_ref[...] = jnp.where(lane == 0, p0, jnp.where(lane == 1, p1, 0.0)).astype(I32)


def dispatch_plan(ids, base):
    t = ids.shape[0]
    tt = _row_tile(t)
    return pl.pallas_call(
        _plan_kernel,
        grid=(t // tt,),
        in_specs=[pl.BlockSpec((tt, LANES), lambda i: (i, 0)),
                  pl.BlockSpec((SUBLANES, LANES), lambda i: (0, 0))],
        out_specs=pl.BlockSpec((tt, LANES), lambda i: (i, 0)),
        out_shape=jax.ShapeDtypeStruct((t, LANES), I32),
        scratch_shapes=[pltpu.VMEM((SUBLANES, LANES), F32)],
        compiler_params=_cparams("arbitrary"),
        name="dispatch_plan",
    )(ids, base)


def _load_positions(pos_hbm, pos_smem, sem):
    chunk = pos_smem.shape[0]
    start = pl.multiple_of(pl.program_id(0) * chunk, chunk)
    copy = pltpu.make_async_copy(pos_hbm.at[pl.ds(start, chunk)], pos_smem, sem)
    copy.start()
    return copy


def _scatter_kernel(x_ref, g_ref, pos_hbm, init_hbm, xs_hbm, xn_scr, pos_smem, sems):
    del init_hbm
    tt = x_ref.shape[0]
    pos_copy = _load_positions(pos_hbm, pos_smem, sems.at[0])
    x = x_ref[...]
    xn_scr[...] = x * _rms_scale(x) * g_ref[...]
    pos_copy.wait()

    def issue(t, carry):
        for k in range(TOP_K):
            row = pos_smem[TOP_K * t + k]
            pltpu.make_async_copy(xn_scr.at[pl.ds(t, 1)], xs_hbm.at[pl.ds(row, 1)], sems.at[1]).start(priority=k)
        return carry

    lax.fori_loop(0, tt, issue, 0, unroll=8)
    done = xs_hbm.at[pl.ds(0, TOP_K * tt)]
    pltpu.make_async_copy(done, done, sems.at[1]).wait()


def scatter_rows(x, g, pos_flat, n_rows):
    t, d = x.shape
    tt = _row_tile(t)
    chunk = pos_flat.shape[0] // (t // tt)
    return pl.pallas_call(
        _scatter_kernel,
        grid=(t // tt,),
        in_specs=[pl.BlockSpec((tt, d), lambda i: (i, 0)),
                  pl.BlockSpec((1, d), lambda i: (0, 0)),
                  pl.BlockSpec(memory_space=pl.ANY),
                  pl.BlockSpec(memory_space=pl.ANY)],
        out_specs=pl.BlockSpec(memory_space=pl.ANY),
        out_shape=jax.ShapeDtypeStruct((n_rows, d), F32),
        scratch_shapes=[pltpu.VMEM((tt, d), F32), pltpu.SMEM((chunk,), I32),
                        pltpu.SemaphoreType.DMA((2,))],
        input_output_aliases={3: 0},
        compiler_params=_cparams("arbitrary"),
        name="scatter_rows",
    )(x, g.reshape(1, d), pos_flat, jnp.zeros((n_rows, d), F32))


def _combine_kernel(h_ref, gate_ref, fg_ref, pos_hbm, ys_hbm, o_ref, ybuf, pos_smem, sems):
    tt = h_ref.shape[0]
    _load_positions(pos_hbm, pos_smem, sems.at[0]).wait()

    def issue(t, carry):
        for k in range(TOP_K):
            row = pos_smem[TOP_K * t + k]
            pltpu.make_async_copy(ys_hbm.at[pl.ds(row, 1)], ybuf.at[k, pl.ds(t, 1)], sems.at[1]).start(priority=k)
        return carry

    lax.fori_loop(0, tt, issue, 0, unroll=8)
    pltpu.make_async_copy(ybuf, ybuf, sems.at[1]).wait()
    gate = gate_ref[...]
    y = h_ref[...] + gate[:, 0:1] * ybuf[0] + gate[:, 1:2] * ybuf[1]
    o_ref[...] = y * _rms_scale(y) * fg_ref[...]


def combine_rows(h, gates, final_gain, pos_flat, ys):
    t, d = h.shape
    tt = _row_tile(t)
    chunk = pos_flat.shape[0] // (t // tt)
    return pl.pallas_call(
        _combine_kernel,
        grid=(t // tt,),
        in_specs=[pl.BlockSpec((tt, d), lambda i: (i, 0)),
                  pl.BlockSpec((tt, LANES), lambda i: (i, 0)),
                  pl.BlockSpec((1, d), lambda i: (0, 0)),
                  pl.BlockSpec(memory_space=pl.ANY),
                  pl.BlockSpec(memory_space=pl.ANY)],
        out_specs=pl.BlockSpec((tt, d), lambda i: (i, 0)),
        out_shape=jax.ShapeDtypeStruct((t, d), F32),
        scratch_shapes=[pltpu.VMEM((TOP_K, tt, d), F32), pltpu.SMEM((chunk,), I32),
                        pltpu.SemaphoreType.DMA((2,))],
        compiler_params=_cparams("arbitrary"),
        name="combine_rows",
    )(h, gates, final_gain.reshape(1, d), pos_flat, ys)


def moe_and_final_norm(h, g, w_router, w_gu, w_down, final_gain):
    t, d = h.shape
    tt = _row_tile(t)
    tm = 512 if t >= 4096 else 128
    n_rows = TOP_K * t + N_EXPERTS * tm
    gates, ids, counts = router(h, g, w_router)

    cnt = counts[0, :N_EXPERTS].astype(I32)
    tiles = (cnt + tm - 1) // tm
    ends = jnp.cumsum(tiles)
    base = jnp.zeros((SUBLANES, LANES), F32).at[:, :N_EXPERTS].set(((ends - tiles) * tm).astype(F32))
    n_active = ends[-1:]
    tile_idx = jnp.arange(n_rows // tm, dtype=I32)
    owner = jnp.sum(tile_idx[:, None] >= ends[None, :], axis=1).astype(I32)
    last_owner = jnp.sum(n_active - 1 >= ends).astype(I32)
    tile_expert = jnp.where(tile_idx < n_active, owner, last_owner)

    pos = dispatch_plan(ids, base)
    chunk = max(TOP_K * tt, SMEM_I32_CHUNK)
    pos_flat = pos[:, :TOP_K].reshape(t // tt, TOP_K * tt)
    pos_flat = jnp.pad(pos_flat, ((0, 0), (0, chunk - TOP_K * tt))).reshape(-1)
    xs = scatter_rows(h, g, pos_flat, n_rows)
    ys = expert_ffn(xs, tile_expert, n_active, w_gu, w_down, tm)
    return combine_rows(h, gates, final_gain, pos_flat, ys)


def _split3(v):
    hi = v.astype(BF16)
    r1 = v - hi.astype(F32)
    mid = r1.astype(BF16)
    lo = (r1 - mid.astype(F32)).astype(BF16)
    return hi, mid, lo


def _ssd_kernel(xbc_ref, dt_ref, z_ref, cbuf_ref, h0_ref, cw_ref, cb_ref, dtb_ref, alog_ref,
                dskip_ref, gn_ref, expand_ref, yn_ref, hfin_ref, h_scr, xext_scr, *, seq_valid):
    q = SSD_BLOCK
    c = pl.program_id(1)
    halo = SUBLANES

    @pl.when(c == 0)
    def _():
        h_scr[...] = h0_ref[0]
        xext_scr[0:halo, :] = cbuf_ref[0]

    xext_scr[halo:halo + q, :] = xbc_ref[0]
    xext = xext_scr[...]
    conv = cb_ref[...] + xext[halo:] * cw_ref[CONV_WIDTH - 1:CONV_WIDTH, :]
    for k in range(CONV_WIDTH - 1):
        delayed = pltpu.roll(xext, CONV_WIDTH - 1 - k, axis=0)
        conv = conv + delayed[halo:] * cw_ref[k:k + 1, :]
    xext_scr[0:halo, :] = xext_scr[q:q + halo, :]
    xbc = _silu(conv)
    xs = xbc[:, :D_INNER]
    b_all = xbc[:, D_INNER:D_INNER + SSM_GROUPS * D_STATE]
    c_all = xbc[:, D_INNER + SSM_GROUPS * D_STATE:]

    v = dt_ref[0] + dtb_ref[...]
    dt = jnp.maximum(v, 0.0) + jnp.log1p(jnp.exp(-jnp.abs(v)))
    row = lax.broadcasted_iota(I32, (q, LANES), 0)
    if seq_valid is not None:
        dt = jnp.where(row + c * q < seq_valid, dt, 0.0)
    a_neg = -jnp.exp(alog_ref[...])
    dta = dt * a_neg

    ti = lax.broadcasted_iota(I32, (q, q), 0)
    si = lax.broadcasted_iota(I32, (q, q), 1)
    causal = ti >= si
    tri = jnp.where(causal, 1.0, 0.0).astype(BF16)
    a_cs = sum(jnp.dot(tri, part, preferred_element_type=F32) for part in _split3(dta))
    a_last = a_cs[q - 1:q, :]
    a_cs_t = a_cs.T

    stacked = jnp.concatenate([dt, jnp.exp(a_cs), jnp.exp(a_last - a_cs),
                               jnp.broadcast_to(jnp.exp(a_last), (SUBLANES, LANES))], axis=0)
    s_hi = stacked.astype(BF16)
    s_lo = (stacked - s_hi.astype(F32)).astype(BF16)
    expand = expand_ref[...]
    expanded = (jnp.dot(s_hi, expand, preferred_element_type=F32)
                + jnp.dot(s_lo, expand, preferred_element_type=F32))
    dt_e = expanded[0:q]
    decay_in_e = expanded[q:2 * q]
    decay_out_e = expanded[2 * q:3 * q]
    decay_blk_e = expanded[3 * q:3 * q + 1]

    xdt = xs * dt_e
    xdt_bf = xdt.astype(BF16)
    wx = (xdt * decay_out_e).astype(BF16)
    low_half = lax.broadcasted_iota(I32, (q, LANES), 1) < SSM_HEAD_DIM

    y_groups = []
    for g in range(SSM_GROUPS):
        bg = b_all[:, g * D_STATE:(g + 1) * D_STATE]
        cg = c_all[:, g * D_STATE:(g + 1) * D_STATE].astype(BF16)
        bg_bf = bg.astype(BF16)
        cbm = lax.dot_general(cg, bg_bf, (((1,), (1,)), ((), ())), preferred_element_type=F32)
        cols = slice(g * GROUP_COLS, (g + 1) * GROUP_COLS)
        h_g = h_scr[g]
        y_g = jnp.dot(cg, h_g.astype(BF16), preferred_element_type=F32) * decay_in_e[:, cols]
        pair_out = []
        for pr in range(HEADS_PER_GROUP // 2):
            parts = []
            for par in range(2):
                h = g * HEADS_PER_GROUP + 2 * pr + par
                seg = (jnp.broadcast_to(a_cs[:, h:h + 1], (q, q))
                       - jnp.broadcast_to(a_cs_t[h:h + 1, :], (q, q)))
                m = (cbm * jnp.exp(jnp.where(causal, seg, -jnp.inf))).astype(BF16)
                pcols = slice((h // 2) * LANES, (h // 2 + 1) * LANES)
                parts.append(jnp.dot(m, xdt_bf[:, pcols], preferred_element_type=F32))
            pair_out.append(jnp.where(low_half, parts[0], parts[1]))
        y_groups.append(y_g + jnp.concatenate(pair_out, axis=1))
        h_scr[g] = (h_g * decay_blk_e[:, cols]
                    + jnp.dot(bg.T.astype(BF16), wx[:, cols], preferred_element_type=F32))

    y = jnp.concatenate(y_groups, axis=1) + dskip_ref[...] * xs
    yz = y * _silu(z_ref[0])
    yn_ref[0] = (yz * _rms_scale(yz) * gn_ref[...]).astype(yn_ref.dtype)

    @pl.when(c == pl.num_programs(1) - 1)
    def _():
        hfin_ref[0] = h_scr[...]


def _state_to_kernel_layout(h):
    b = h.shape[0]
    h = h.reshape(b, SSM_GROUPS, HEADS_PER_GROUP, SSM_HEAD_DIM, D_STATE)
    return h.transpose(0, 1, 4, 2, 3).reshape(b, SSM_GROUPS, D_STATE, GROUP_COLS)


def _state_from_kernel_layout(h):
    b = h.shape[0]
    h = h.reshape(b, SSM_GROUPS, D_STATE, HEADS_PER_GROUP, SSM_HEAD_DIM)
    return h.transpose(0, 1, 3, 4, 2).reshape(b, N_SSM_HEADS, SSM_HEAD_DIM, D_STATE)


def _pad_lanes(v):
    return jnp.pad(v.astype(F32), (0, LANES - v.shape[0])).reshape(1, LANES)


def ssd_mixer(xbc, dt_raw, z, conv_buf, h0, conv_w, conv_b, dt_bias, a_log, d_skip, gnorm):
    bsz, seq, _ = xbc.shape
    q = SSD_BLOCK
    seq_pad = -(-seq // q) * q
    seq_valid = None
    if seq_pad != seq:
        pad = ((0, 0), (0, seq_pad - seq), (0, 0))
        xbc, dt_raw, z = jnp.pad(xbc, pad), jnp.pad(dt_raw, pad), jnp.pad(z, pad)
        seq_valid = seq
    cbuf = jnp.pad(conv_buf, ((0, 0), (SUBLANES - (CONV_WIDTH - 1), 0), (0, 0)))
    expand = (jnp.arange(D_INNER)[None, :] // SSM_HEAD_DIM == jnp.arange(LANES)[:, None]).astype(BF16)
    d_e = jnp.repeat(d_skip.astype(F32), SSM_HEAD_DIM).reshape(1, D_INNER)
    state_spec = pl.BlockSpec((1, SSM_GROUPS, D_STATE, GROUP_COLS), lambda b, c: (b, 0, 0, 0))

    def const(shape):
        return pl.BlockSpec(shape, lambda b, c: (0,) * len(shape))

    yn, h_fin = pl.pallas_call(
        functools.partial(_ssd_kernel, seq_valid=seq_valid),
        grid=(bsz, seq_pad // q),
        in_specs=[pl.BlockSpec((1, q, CONV_DIM), lambda b, c: (b, c, 0)),
                  pl.BlockSpec((1, q, LANES), lambda b, c: (b, c, 0)),
                  pl.BlockSpec((1, q, D_INNER), lambda b, c: (b, c, 0)),
                  pl.BlockSpec((1, SUBLANES, CONV_DIM), lambda b, c: (b, 0, 0)),
                  state_spec,
                  const((CONV_WIDTH, CONV_DIM)), const((1, CONV_DIM)), const((1, LANES)), const((1, LANES)),
                  const((1, D_INNER)), const((1, D_INNER)), const((LANES, D_INNER))],
        out_specs=[pl.BlockSpec((1, q, D_INNER), lambda b, c: (b, c, 0)), state_spec],
        out_shape=[jax.ShapeDtypeStruct((bsz, seq_pad, D_INNER), BF16),
                   jax.ShapeDtypeStruct((bsz, SSM_GROUPS, D_STATE, GROUP_COLS), F32)],
        scratch_shapes=[pltpu.VMEM((SSM_GROUPS, D_STATE, GROUP_COLS), F32),
                        pltpu.VMEM((q + SUBLANES, CONV_DIM), F32)],
        compiler_params=_cparams("parallel", "arbitrary"),
        name="ssd_mixer",
    )(xbc, dt_raw, z, cbuf, _state_to_kernel_layout(h0.astype(F32)), conv_w.astype(F32),
      conv_b.astype(F32).reshape(1, CONV_DIM), _pad_lanes(dt_bias), _pad_lanes(a_log), d_e,
      gnorm.astype(F32).reshape(1, D_INNER), expand)
    return yn[:, :seq], _state_from_kernel_layout(h_fin)


_ALIBI_SLOPES = (2.0 ** (-8.0 * np.arange(1, N_Q_HEADS + 1, dtype=np.float32) / N_Q_HEADS)).astype(np.float32)


def _attn_bias(lbq, chunk, past_valid):
    w = WINDOW + lbq
    qi = np.arange(lbq)[:, None]
    si = np.arange(w)[None, :]
    dist = np.abs(WINDOW + qi - si).astype(np.float32)
    lo = (qi // chunk) * chunk
    visible = (si >= lo) & (si < lo + WINDOW + chunk)
    variants = []
    for first_block in (True, False):
        valid = visible & (si >= WINDOW) if (first_block and not past_valid) else visible
        per_head = [np.where(valid, -(_ALIBI_SLOPES[h] * dist), np.float32(NEG_INF)) for h in range(N_Q_HEADS)]
        variants.append(np.stack([np.concatenate([per_head[4 * kv + par], per_head[4 * kv + 2 + par]], axis=0)
                                  for kv in range(N_KV_HEADS) for par in range(2)]))
    return np.stack(variants).astype(np.float32)


def _attn_kernel(sink_ref, q_ref, k_ref, v_ref, bias_ref, o_ref, *, lbq):
    n = pl.program_id(1)
    w = WINDOW + lbq
    base = pl.multiple_of(n * lbq, lbq)
    kband = k_ref[0, pl.ds(base, w), :].astype(F32)
    vband = v_ref[0, pl.ds(base, w), :].astype(F32)
    first_pair = lax.broadcasted_iota(I32, (2 * lbq, 1), 0) < lbq
    lane = lax.broadcasted_iota(I32, (w, LANES), 1)
    low_half = lane < ATTN_HEAD_DIM

    for kv in range(N_KV_HEADS):
        pcols = slice((kv // 2) * LANES, (kv // 2 + 1) * LANES)
        kpair, vpair = kband[:, pcols], vband[:, pcols]
        kswap = pltpu.roll(kpair, ATTN_HEAD_DIM, axis=1)
        vswap = pltpu.roll(vpair, ATTN_HEAD_DIM, axis=1)
        in_low = kv % 2 == 0
        k_lo = jnp.where(low_half, kpair if in_low else kswap, 0.0).astype(BF16)
        k_hi = jnp.where(low_half, 0.0, kswap if in_low else kpair).astype(BF16)
        v_lo = jnp.where(low_half, vpair if in_low else vswap, 0.0).astype(BF16)
        v_hi = jnp.where(low_half, 0.0, vswap if in_low else vpair).astype(BF16)
        qs = q_ref[2 * kv:2 * kv + 2, 0].reshape(2 * lbq, LANES)
        out = None
        for par, (kx, vx) in enumerate(((k_lo, v_lo), (k_hi, v_hi))):
            h0, h1 = 4 * kv + par, 4 * kv + 2 + par
            sink = jnp.where(first_pair, sink_ref[h0], sink_ref[h1])
            s = lax.dot_general(qs, kx, (((1,), (1,)), ((), ())), preferred_element_type=F32)
            s = s + bias_ref[0, 2 * kv + par]
            m = jnp.maximum(jnp.max(s, axis=-1, keepdims=True), sink)
            p = jnp.exp(s - m)
            denom = jnp.sum(p, axis=-1, keepdims=True) + jnp.exp(sink - m)
            pn = (p / denom).astype(BF16)
            part = jnp.dot(pn, vx, preferred_element_type=F32)
            out = part if out is None else out + part
        o_ref[0, :, (2 * kv) * LANES:(2 * kv + 1) * LANES] = out[:lbq].astype(o_ref.dtype)
        o_ref[0, :, (2 * kv + 1) * LANES:(2 * kv + 2) * LANES] = out[lbq:].astype(o_ref.dtype)


def window_attention(q_pairs, k_all, v_all, sinks, *, lbq, chunk, past_valid):
    n_pairs, bsz, seq, _ = q_pairs.shape
    rows = k_all.shape[1]
    bias = jnp.asarray(_attn_bias(lbq, chunk, past_valid))
    return pl.pallas_call(
        functools.partial(_attn_kernel, lbq=lbq),
        grid_spec=pltpu.PrefetchScalarGridSpec(
            num_scalar_prefetch=1,
            grid=(bsz, seq // lbq),
            in_specs=[pl.BlockSpec((n_pairs, 1, lbq, LANES), lambda b, n, s: (0, b, n, 0)),
                      pl.BlockSpec((1, rows, KV_DIM), lambda b, n, s: (b, 0, 0)),
                      pl.BlockSpec((1, rows, KV_DIM), lambda b, n, s: (b, 0, 0)),
                      pl.BlockSpec((1,) + bias.shape[1:], lambda b, n, s: (jnp.minimum(n, 1), 0, 0, 0))],
            out_specs=pl.BlockSpec((1, lbq, D_MODEL), lambda b, n, s: (b, n, 0))),
        out_shape=jax.ShapeDtypeStruct((bsz, seq, D_MODEL), BF16),
        compiler_params=_cparams("parallel", "arbitrary"),
        name="window_attention",
    )(sinks.astype(F32), q_pairs, k_all, v_all, bias)


def _trunk(x, conv_buf, ssm_state, k_past, v_past, past_valid, p):
    bsz, seq, d = x.shape
    t = bsz * seq
    x2 = x.reshape(t, d)

    z, xbc, dt_raw = in_proj(x2, p['norm_mix'][0], p['w_z'], p['w_xbc'], p['w_dt'])
    xbc3 = xbc.reshape(bsz, seq, CONV_DIM)
    new_conv = jnp.concatenate([conv_buf.astype(F32), xbc3], axis=1)[:, seq:]
    yn, new_ssm = ssd_mixer(xbc3, dt_raw.reshape(bsz, seq, LANES), z.reshape(bsz, seq, D_INNER),
                            conv_buf.astype(F32), ssm_state, p['ssm_conv_w'], p['ssm_conv_b'],
                            p['ssm_dt_bias'], p['ssm_a_log'], p['ssm_d'], p['ssm_norm'])
    h = matmul_residual(yn.reshape(t, D_INNER), p['ssm_out_w'], x2)

    h = ffn(h, p['norm_ffn'][0], p['ffn_w_gate_up'], p['ffn_w_down'])

    kv, q_pairs = kv_q_proj(h, p['kv_norm'], p['norm_mix'][1], p['w_kv'], p['w_q'])
    kv = kv.reshape(bsz, seq, 2 * KV_DIM)
    k_new, v_new = kv[..., :KV_DIM], kv[..., KV_DIM:]
    k_all = jnp.concatenate([k_past.reshape(bsz, WINDOW, KV_DIM), k_new], axis=1).astype(BF16)
    v_all = jnp.concatenate([v_past.reshape(bsz, WINDOW, KV_DIM), v_new], axis=1).astype(BF16)

    chunk = min(ATTN_CHUNK, seq)
    lbq = 2 * chunk if seq % (2 * chunk) == 0 else chunk
    q_pairs = q_pairs.reshape(D_MODEL // LANES, bsz, seq, LANES)
    o = window_attention(q_pairs, k_all, v_all, p['attn_sinks'], lbq=lbq, chunk=chunk, past_valid=past_valid)
    h = matmul_residual(o.reshape(t, D_MODEL), p['w_o'], h)

    y = moe_and_final_norm(h, p['norm_ffn'][1], p['moe_router'], p['moe_w_gate_up'], p['moe_w_down'],
                           p['final_norm'])

    k4 = k_new.reshape(bsz, seq, N_KV_HEADS, ATTN_HEAD_DIM)
    v4 = v_new.reshape(bsz, seq, N_KV_HEADS, ATTN_HEAD_DIM)
    return y.reshape(bsz, seq, d), new_conv[None], new_ssm[None], k4, v4


def kernel(x_prompt, x_sample, state_conv, state_ssm, cache_k, cache_v, norm_mix, norm_ffn, ssm_in_w, ssm_conv_w,
           ssm_conv_b, ssm_dt_bias, ssm_a_log, ssm_d, ssm_norm, ssm_out_w, kv_norm, w_kv, w_q, attn_sinks, w_o,
           ffn_w_gate_up, ffn_w_down, moe_router, moe_w_gate_up, moe_w_down, final_norm):
    in_w = ssm_in_w[0]
    n_dt = N_SSM_HEADS
    p = {
        'norm_mix': norm_mix, 'norm_ffn': norm_ffn,
        'w_z': in_w[:, :D_INNER].astype(BF16),
        'w_xbc': in_w[:, D_INNER:D_INNER + CONV_DIM].astype(BF16),
        'w_dt': jnp.pad(in_w[:, D_INNER + CONV_DIM:], ((0, 0), (0, LANES - n_dt))).astype(BF16),
        'ssm_conv_w': ssm_conv_w[0], 'ssm_conv_b': ssm_conv_b[0], 'ssm_dt_bias': ssm_dt_bias[0],
        'ssm_a_log': ssm_a_log[0], 'ssm_d': ssm_d[0], 'ssm_norm': ssm_norm[0],
        'ssm_out_w': ssm_out_w[0].astype(BF16),
        'kv_norm': kv_norm, 'w_kv': w_kv.astype(BF16), 'w_q': w_q[0].astype(BF16),
        'attn_sinks': attn_sinks[0], 'w_o': w_o[0].astype(BF16),
        'ffn_w_gate_up': ffn_w_gate_up.astype(BF16), 'ffn_w_down': ffn_w_down.astype(BF16),
        'moe_router': jnp.pad(moe_router[0].astype(F32), ((0, 0), (0, LANES - N_EXPERTS))),
        'moe_w_gate_up': moe_w_gate_up[0].astype(BF16), 'moe_w_down': moe_w_down[0].astype(BF16),
        'final_norm': final_norm,
    }
    bsz, seq_p = x_prompt.shape[:2]
    dt = x_prompt.dtype
    zero_conv = jnp.zeros((bsz, CONV_WIDTH - 1, CONV_DIM), dt)
    zero_ssm = jnp.zeros((bsz, N_SSM_HEADS, SSM_HEAD_DIM, D_STATE), dt)
    zero_kv = jnp.zeros((bsz, WINDOW, N_KV_HEADS, ATTN_HEAD_DIM), dt)
    y_p, conv_p, ssm_p, k_p, v_p = _trunk(x_prompt, zero_conv, zero_ssm, zero_kv, zero_kv, False, p)
    keep = min(WINDOW, seq_p)
    y_s, conv_s, ssm_s, k_s, v_s = _trunk(x_sample, state_conv[0], state_ssm[0], cache_k, cache_v, True, p)
    return (y_p, y_s, conv_p, ssm_p, k_p[:, seq_p - keep:], v_p[:, seq_p - keep:], conv_s, ssm_s, k_s, v_s)
```

```python
import functools
import math

import jax
import jax.numpy as jnp
import numpy as np
from jax import lax
from jax.experimental import pallas as pl
from jax.experimental.pallas import tpu as pltpu

F32 = jnp.float32
BF16 = jnp.bfloat16
I32 = jnp.int32

D_MODEL = 1024
EPS = 1e-6
D_INNER = 2048
SSM_HEAD_DIM = 64
N_SSM_HEADS = 32
SSM_GROUPS = 4
HEADS_PER_GROUP = N_SSM_HEADS // SSM_GROUPS
D_STATE = 128
CONV_WIDTH = 4
CONV_DIM = D_INNER + 2 * SSM_GROUPS * D_STATE
ATTN_HEAD_DIM = 64
N_Q_HEADS = 16
N_KV_HEADS = 4
KV_DIM = N_KV_HEADS * ATTN_HEAD_DIM
WINDOW = 128
ATTN_CHUNK = 64
ATTN_SCALE = 1.0 / math.sqrt(ATTN_HEAD_DIM)
NEG_INF = -1e30
D_FF = 2816
N_EXPERTS = 8
TOP_K = 2

LANES = 128
SUBLANES = 8
VMEM_LIMIT_BYTES = 56 * 1024 * 1024
RUN_ALIGN = SUBLANES

SSD_BLOCK = 128
GROUP_COLS = HEADS_PER_GROUP * SSM_HEAD_DIM
FF_CHUNK = 512


def _cparams(*sem):
    return pltpu.CompilerParams(dimension_semantics=sem, vmem_limit_bytes=VMEM_LIMIT_BYTES)


def _row_tile(t, pref=512):
    return pref if t % pref == 0 else t


def _resident(shape):
    return pl.BlockSpec(shape, lambda *_: (0,) * len(shape), pipeline_mode=pl.Buffered(1))


def _rms_scale(x):
    return lax.rsqrt(jnp.mean(x * x, axis=-1, keepdims=True) + EPS)


def _silu(x):
    return x / (1.0 + jnp.exp(-x))


def _in_proj_kernel(x_ref, g_ref, wz_ref, wx_ref, wd_ref, z_ref, xbc_ref, dt_ref):
    x = x_ref[...]
    xn = (x * _rms_scale(x) * g_ref[...]).astype(BF16)
    z_ref[...] = jnp.dot(xn, wz_ref[...], preferred_element_type=F32)
    xbc_ref[...] = jnp.dot(xn, wx_ref[...], preferred_element_type=F32)
    dt_ref[...] = jnp.dot(xn, wd_ref[...], preferred_element_type=F32)


def in_proj(x, g, w_z, w_xbc, w_dt):
    t, k = x.shape
    tm = _row_tile(t)
    ws = (w_z, w_xbc, w_dt)
    return pl.pallas_call(
        _in_proj_kernel,
        grid=(t // tm,),
        in_specs=[pl.BlockSpec((tm, k), lambda i: (i, 0)), pl.BlockSpec((1, k), lambda i: (0, 0))]
        + [_resident(w.shape) for w in ws],
        out_specs=[pl.BlockSpec((tm, w.shape[1]), lambda i: (i, 0)) for w in ws],
        out_shape=[jax.ShapeDtypeStruct((t, w.shape[1]), F32) for w in ws],
        compiler_params=_cparams("parallel"),
        name="in_proj",
    )(x, g.reshape(1, k), *ws)


def _kvq_kernel(x_ref, gkv_ref, gq_ref, wkv_ref, wq_ref, kv_ref, q_ref):
    x = x_ref[...]
    xs = x * _rms_scale(x)
    kv_ref[...] = jnp.dot((xs * gkv_ref[...]).astype(BF16), wkv_ref[...], preferred_element_type=F32)
    q = jnp.dot((xs * gq_ref[...]).astype(BF16), wq_ref[...], preferred_element_type=F32)
    q = (q * ATTN_SCALE).astype(q_ref.dtype)
    for p in range(q_ref.shape[0]):
        q_ref[p] = q[:, p * LANES:(p + 1) * LANES]


def kv_q_proj(x, g_kv, g_q, w_kv, w_q):
    t, k = x.shape
    tm = _row_tile(t)
    n_pairs = w_q.shape[1] // LANES
    return pl.pallas_call(
        _kvq_kernel,
        grid=(t // tm,),
        in_specs=[pl.BlockSpec((tm, k), lambda i: (i, 0)),
                  pl.BlockSpec((1, k), lambda i: (0, 0)), pl.BlockSpec((1, k), lambda i: (0, 0)),
                  pl.BlockSpec(w_kv.shape, lambda i: (0, 0)), pl.BlockSpec(w_q.shape, lambda i: (0, 0))],
        out_specs=[pl.BlockSpec((tm, w_kv.shape[1]), lambda i: (i, 0)),
                   pl.BlockSpec((n_pairs, tm, LANES), lambda i: (0, i, 0))],
        out_shape=[jax.ShapeDtypeStruct((t, w_kv.shape[1]), F32),
                   jax.ShapeDtypeStruct((n_pairs, t, LANES), BF16)],
        compiler_params=_cparams("parallel"),
        name="kv_q_proj",
    )(x, g_kv.reshape(1, k), g_q.reshape(1, k), w_kv, w_q)


def _matmul_residual_kernel(a_ref, w_ref, r_ref, o_ref):
    o_ref[...] = r_ref[...] + jnp.dot(a_ref[...], w_ref[...], preferred_element_type=F32)


def matmul_residual(a, w, r):
    t, k = a.shape
    n = w.shape[1]
    tm = _row_tile(t)
    return pl.pallas_call(
        _matmul_residual_kernel,
        grid=(t // tm,),
        in_specs=[pl.BlockSpec((tm, k), lambda i: (i, 0)),
                  pl.BlockSpec((k, n), lambda i: (0, 0)),
                  pl.BlockSpec((tm, n), lambda i: (i, 0))],
        out_specs=pl.BlockSpec((tm, n), lambda i: (i, 0)),
        out_shape=jax.ShapeDtypeStruct((t, n), F32),
        compiler_params=_cparams("parallel"),
        name="matmul_residual",
    )(a, w, r)


def _swiglu(xb, wgu_ref, wd_ref):
    acc = None
    for c0 in range(0, D_FF, FF_CHUNK):
        cw = min(FF_CHUNK, D_FF - c0)
        gg = jnp.dot(xb, wgu_ref[0, :, c0:c0 + cw], preferred_element_type=F32)
        uu = jnp.dot(xb, wgu_ref[0, :, D_FF + c0:D_FF + c0 + cw], preferred_element_type=F32)
        act = (_silu(gg) * uu).astype(BF16)
        part = jnp.dot(act, wd_ref[0, c0:c0 + cw, :], preferred_element_type=F32)
        acc = part if acc is None else acc + part
    return acc


def _ffn_kernel(x_ref, g_ref, wgu_ref, wd_ref, o_ref):
    x = x_ref[...]
    xb = (x * _rms_scale(x) * g_ref[...]).astype(BF16)
    o_ref[...] = x + _swiglu(xb, wgu_ref, wd_ref)


def ffn(x, g, w_gu, w_down):
    t, d = x.shape
    tm = _row_tile(t)
    return pl.pallas_call(
        _ffn_kernel,
        grid=(t // tm,),
        in_specs=[pl.BlockSpec((tm, d), lambda i: (i, 0)),
                  pl.BlockSpec((1, d), lambda i: (0, 0)),
                  _resident(w_gu.shape), _resident(w_down.shape)],
        out_specs=pl.BlockSpec((tm, d), lambda i: (i, 0)),
        out_shape=jax.ShapeDtypeStruct((t, d), F32),
        compiler_params=_cparams("parallel"),
        name="ffn",
    )(x, g.reshape(1, d), w_gu, w_down)


def _expert_ffn_kernel(te_ref, na_ref, x_ref, wgu_ref, wd_ref, o_ref):
    active = pl.program_id(0) < na_ref[0]

    @pl.when(active)
    def _():
        o_ref[...] = _swiglu(x_ref[...].astype(BF16), wgu_ref, wd_ref)

    @pl.when(jnp.logical_not(active))
    def _():
        o_ref[...] = jnp.zeros_like(o_ref)


def expert_ffn(xs, tile_expert, n_active, w_gu, w_down, tm):
    rows, d = xs.shape
    return pl.pallas_call(
        _expert_ffn_kernel,
        grid_spec=pltpu.PrefetchScalarGridSpec(
            num_scalar_prefetch=2,
            grid=(rows // tm,),
            in_specs=[pl.BlockSpec((tm, d), lambda i, te, na: (jnp.minimum(i, na[0] - 1), 0)),
                      pl.BlockSpec((1, d, 2 * D_FF), lambda i, te, na: (te[i], 0, 0)),
                      pl.BlockSpec((1, D_FF, d), lambda i, te, na: (te[i], 0, 0))],
            out_specs=pl.BlockSpec((tm, d), lambda i, te, na: (i, 0))),
        out_shape=jax.ShapeDtypeStruct((rows, d), F32),
        compiler_params=_cparams("arbitrary"),
        name="expert_ffn",
    )(tile_expert, n_active, xs, w_gu, w_down)


def _router_kernel(x_ref, g_ref, whi_ref, wlo_ref, gate_ref, ids_ref, cnt_ref):
    x = x_ref[...]
    xn = x * _rms_scale(x) * g_ref[...]
    x_hi = xn.astype(BF16)
    x_lo = (xn - x_hi.astype(F32)).astype(BF16)
    logits = (jnp.dot(x_hi, whi_ref[...], preferred_element_type=F32)
              + jnp.dot(x_lo, whi_ref[...], preferred_element_type=F32)
              + jnp.dot(x_hi, wlo_ref[...], preferred_element_type=F32))
    lane = lax.broadcasted_iota(I32, logits.shape, 1)
    logits = jnp.where(lane < N_EXPERTS, logits, -jnp.inf)
    m1 = jnp.max(logits, axis=-1, keepdims=True)
    i1 = jnp.min(jnp.where(logits == m1, lane, LANES), axis=-1, keepdims=True)
    rest = jnp.where(lane == i1, -jnp.inf, logits)
    m2 = jnp.max(rest, axis=-1, keepdims=True)
    i2 = jnp.min(jnp.where(rest == m2, lane, LANES), axis=-1, keepdims=True)
    e2 = jnp.exp(m2 - m1)
    w1 = 1.0 / (1.0 + e2)
    w2 = e2 / (1.0 + e2)
    gate_ref[...] = jnp.where(lane == 0, w1, jnp.where(lane == 1, w2, 0.0))
    ids_ref[...] = jnp.where(lane == 0, i1, jnp.where(lane == 1, i2, 0))
    chosen = jnp.where(jnp.logical_or(lane == i1, lane == i2), 1.0, 0.0)

    @pl.when(pl.program_id(0) == 0)
    def _():
        cnt_ref[...] = jnp.zeros_like(cnt_ref)

    n_tile = jnp.sum(chosen, axis=0, keepdims=True)
    n_pad = jnp.ceil(n_tile * (1.0 / RUN_ALIGN)) * RUN_ALIGN
    cnt_ref[...] += jnp.broadcast_to(n_pad, cnt_ref.shape)


def router(x, g, w_router):
    t, d = x.shape
    tm = _row_tile(t)
    w_hi = w_router.astype(BF16)
    w_lo = (w_router - w_hi.astype(F32)).astype(BF16)
    return pl.pallas_call(
        _router_kernel,
        grid=(t // tm,),
        in_specs=[pl.BlockSpec((tm, d), lambda i: (i, 0)),
                  pl.BlockSpec((1, d), lambda i: (0, 0)),
                  pl.BlockSpec((d, LANES), lambda i: (0, 0)),
                  pl.BlockSpec((d, LANES), lambda i: (0, 0))],
        out_specs=[pl.BlockSpec((tm, LANES), lambda i: (i, 0)),
                   pl.BlockSpec((tm, LANES), lambda i: (i, 0)),
                   pl.BlockSpec((SUBLANES, LANES), lambda i: (0, 0))],
        out_shape=[jax.ShapeDtypeStruct((t, LANES), F32),
                   jax.ShapeDtypeStruct((t, LANES), I32),
                   jax.ShapeDtypeStruct((SUBLANES, LANES), F32)],
        compiler_params=_cparams("arbitrary"),
        name="router",
    )(x, g.reshape(1, d), w_hi, w_lo)


META_ROWS = 3


def _staging_rows(tt):
    need = TOP_K * tt + N_EXPERTS * (RUN_ALIGN - 1)
    return -(-need // LANES) * LANES


def _plan_kernel(ids_ref, base_ref, slot_ref, meta_ref, carry_ref):
    @pl.when(pl.program_id(0) == 0)
    def _():
        carry_ref[...] = jnp.zeros_like(carry_ref)

    ids = ids_ref[...]
    tt = ids.shape[0]
    lane = lax.broadcasted_iota(I32, ids.shape, 1)
    pick0 = lane == ids[:, 0:1]
    pick1 = lane == ids[:, 1:2]
    chosen = jnp.where(jnp.logical_or(pick0, pick1), 1.0, 0.0)
    ri = lax.broadcasted_iota(I32, (tt, tt), 0)
    ci = lax.broadcasted_iota(I32, (tt, tt), 1)
    before = jnp.where(ri > ci, 1.0, 0.0).astype(BF16)
    rank = jnp.dot(before, chosen.astype(BF16), preferred_element_type=F32)
    n_pad = jnp.ceil(jnp.sum(chosen, axis=0, keepdims=True) * (1.0 / RUN_ALIGN)) * RUN_ALIGN
    ei = lax.broadcasted_iota(I32, (LANES, LANES), 0)
    ej = lax.broadcasted_iota(I32, (LANES, LANES), 1)
    earlier = jnp.where(ei < ej, 1.0, 0.0).astype(BF16)
    n_pad8 = jnp.broadcast_to(n_pad, (SUBLANES, LANES))
    local = jnp.dot(n_pad8.astype(BF16), earlier, preferred_element_type=F32)[0:1, :]
    slot = local + rank
    s0 = jnp.sum(jnp.where(pick0, slot, 0.0), axis=-1, keepdims=True)
    s1 = jnp.sum(jnp.where(pick1, slot, 0.0), axis=-1, keepdims=True)
    slot_ref[...] = jnp.where(lane == 0, s0, jnp.where(lane == 1, s1, 0.0))
    row = lax.broadcasted_iota(I32, (SUBLANES, LANES), 0)
    sorted_start = base_ref[0:1, :] + carry_ref[0:1, :]
    meta = jnp.where(row == 0, sorted_start, jnp.where(row == 1, local, jnp.where(row == 2, n_pad, 0.0)))
    meta_ref[...] = meta.astype(I32)
    carry_ref[...] += n_pad8


def dispatch_plan(ids, base):
    t = ids.shape[0]
    tt = _row_tile(t)
    return pl.pallas_call(
        _plan_kernel,
        grid=(t // tt,),
        in_specs=[pl.BlockSpec((tt, LANES), lambda i: (i, 0)),
                  pl.BlockSpec((SUBLANES, LANES), lambda i: (0, 0))],
        out_specs=[pl.BlockSpec((tt, LANES), lambda i: (i, 0)),
                   pl.BlockSpec((SUBLANES, LANES), lambda i: (i, 0))],
        out_shape=[jax.ShapeDtypeStruct((t, LANES), F32),
                   jax.ShapeDtypeStruct((SUBLANES * (t // tt), LANES), I32)],
        scratch_shapes=[pltpu.VMEM((SUBLANES, LANES), F32)],
        compiler_params=_cparams("arbitrary"),
        name="dispatch_plan",
    )(ids, base)


def _run_copies(meta_ref, staging, sorted_hbm, sem, to_sorted):
    first = pl.program_id(0) * (META_ROWS * N_EXPERTS)
    out = []
    for e in range(N_EXPERTS):
        sorted_start = pl.multiple_of(meta_ref[first + e], RUN_ALIGN)
        local_start = pl.multiple_of(meta_ref[first + N_EXPERTS + e], RUN_ALIGN)
        n = pl.multiple_of(meta_ref[first + 2 * N_EXPERTS + e], RUN_ALIGN)
        a, b = staging.at[pl.ds(local_start, n)], sorted_hbm.at[pl.ds(sorted_start, n)]
        out.append((n, pltpu.make_async_copy(a, b, sem) if to_sorted else pltpu.make_async_copy(b, a, sem)))
    return out


def _start_then_wait(copies):
    for n, copy in copies:
        pl.when(n > 0)(copy.start)
    for n, copy in copies:
        pl.when(n > 0)(copy.wait)


def _scatter_kernel(meta_ref, tail_ref, x_ref, g_ref, slot_ref, xs_hbm, stage_scr, zero_scr, sem):
    x = x_ref[...]
    xn = (x * _rms_scale(x) * g_ref[...]).astype(BF16)
    tt, rt = x.shape[0], stage_scr.shape[0]
    slot_t = slot_ref[...].T.astype(I32)
    r = lax.broadcasted_iota(I32, (rt, tt), 0)
    place = jnp.where(jnp.logical_or(r == slot_t[0:1, :], r == slot_t[1:2, :]), 1.0, 0.0).astype(BF16)
    stage_scr[...] = jnp.dot(place, xn, preferred_element_type=F32)
    _start_then_wait(_run_copies(meta_ref, stage_scr, xs_hbm, sem.at[0], to_sorted=True))

    @pl.when(pl.program_id(0) == pl.num_programs(0) - 1)
    def _():
        zero_scr[...] = jnp.zeros_like(zero_scr)
        tails = []
        for e in range(N_EXPERTS):
            start = pl.multiple_of(tail_ref[e], RUN_ALIGN)
            n = pl.multiple_of(tail_ref[N_EXPERTS + e], RUN_ALIGN)
            tails.append((n, pltpu.make_async_copy(zero_scr.at[pl.ds(0, n)], xs_hbm.at[pl.ds(start, n)], sem.at[0])))
        _start_then_wait(tails)

        tm = zero_scr.shape[0]

        def clear_tile(j, carry):
            copy = pltpu.make_async_copy(zero_scr, xs_hbm.at[pl.ds(pl.multiple_of(j * tm, tm), tm)], sem.at[0])
            copy.start()
            copy.wait()
            return carry

        lax.fori_loop(tail_ref[2 * N_EXPERTS], xs_hbm.shape[0] // tm, clear_tile, 0)


def scatter_rows(x, g, slot, meta, tails, n_rows, tm):
    t, d = x.shape
    tt = _row_tile(t)
    return pl.pallas_call(
        _scatter_kernel,
        grid_spec=pltpu.PrefetchScalarGridSpec(
            num_scalar_prefetch=2,
            grid=(t // tt,),
            in_specs=[pl.BlockSpec((tt, d), lambda i, m, tl: (i, 0)),
                      pl.BlockSpec((1, d), lambda i, m, tl: (0, 0)),
                      pl.BlockSpec((tt, LANES), lambda i, m, tl: (i, 0))],
            out_specs=pl.BlockSpec(memory_space=pl.ANY),
            scratch_shapes=[pltpu.VMEM((_staging_rows(tt), d), F32), pltpu.VMEM((tm, d), F32),
                            pltpu.SemaphoreType.DMA((1,))]),
        out_shape=jax.ShapeDtypeStruct((n_rows, d), F32),
        compiler_params=_cparams("arbitrary"),
        name="scatter_rows",
    )(meta, tails, x, g.reshape(1, d), slot)


def _combine_kernel(meta_ref, h_ref, gate_ref, slot_ref, fg_ref, ys_hbm, o_ref, stage_scr, sem):
    @pl.when(pl.program_id(0) == 0)
    def _():
        stage_scr[...] = jnp.zeros_like(stage_scr)

    _start_then_wait(_run_copies(meta_ref, stage_scr, ys_hbm, sem.at[0], to_sorted=False))
    tt, rt = h_ref.shape[0], stage_scr.shape[0]
    slot = slot_ref[...].astype(I32)
    gate = gate_ref[...]
    c = lax.broadcasted_iota(I32, (tt, rt), 1)
    weight = (jnp.where(c == slot[:, 0:1], gate[:, 0:1], 0.0)
              + jnp.where(c == slot[:, 1:2], gate[:, 1:2], 0.0))
    w_hi = weight.astype(BF16)
    w_lo = (weight - w_hi.astype(F32)).astype(BF16)
    yb = stage_scr[...].astype(BF16)
    moe = jnp.dot(w_hi, yb, preferred_element_type=F32) + jnp.dot(w_lo, yb, preferred_element_type=F32)
    y = h_ref[...] + moe
    o_ref[...] = y * _rms_scale(y) * fg_ref[...]


def combine_rows(h, gates, slot, meta, final_gain, ys):
    t, d = h.shape
    tt = _row_tile(t)
    return pl.pallas_call(
        _combine_kernel,
        grid_spec=pltpu.PrefetchScalarGridSpec(
            num_scalar_prefetch=1,
            grid=(t // tt,),
            in_specs=[pl.BlockSpec((tt, d), lambda i, m: (i, 0)),
                      pl.BlockSpec((tt, LANES), lambda i, m: (i, 0)),
                      pl.BlockSpec((tt, LANES), lambda i, m: (i, 0)),
                      pl.BlockSpec((1, d), lambda i, m: (0, 0)),
                      pl.BlockSpec(memory_space=pl.ANY)],
            out_specs=pl.BlockSpec((tt, d), lambda i, m: (i, 0)),
            scratch_shapes=[pltpu.VMEM((_staging_rows(tt), d), F32), pltpu.SemaphoreType.DMA((1,))]),
        out_shape=jax.ShapeDtypeStruct((t, d), F32),
        compiler_params=_cparams("arbitrary"),
        name="combine_rows",
    )(meta, h, gates, slot, final_gain.reshape(1, d), ys)


def moe_and_final_norm(h, g, w_router, w_gu, w_down, final_gain):
    t, d = h.shape
    tt = _row_tile(t)
    n_tok_tiles = t // tt
    tm = 512 if t >= 4096 else 128
    worst = TOP_K * t + n_tok_tiles * N_EXPERTS * (RUN_ALIGN - 1) + N_EXPERTS * tm
    n_rows = -(-worst // tm) * tm
    gates, ids, counts = router(h, g, w_router)

    cnt = counts[0, :N_EXPERTS].astype(I32)
    tiles = (cnt + tm - 1) // tm
    ends = jnp.cumsum(tiles)
    starts = (ends - tiles) * tm
    base = jnp.zeros((SUBLANES, LANES), F32).at[:, :N_EXPERTS].set(starts.astype(F32))
    n_active = ends[-1:]
    tile_idx = jnp.arange(n_rows // tm, dtype=I32)
    owner = jnp.sum(tile_idx[:, None] >= ends[None, :], axis=1).astype(I32)
    last_owner = jnp.sum(n_active - 1 >= ends).astype(I32)
    tile_expert = jnp.where(tile_idx < n_active, owner, last_owner)
    tails = jnp.concatenate([starts + cnt, tiles * tm - cnt, n_active]).astype(I32)

    slot, meta = dispatch_plan(ids, base)
    meta_flat = meta.reshape(n_tok_tiles, SUBLANES, LANES)[:, :META_ROWS, :N_EXPERTS].reshape(-1)
    xs = scatter_rows(h, g, slot, meta_flat, tails, n_rows, tm)
    ys = expert_ffn(xs, tile_expert, n_active, w_gu, w_down, tm)
    return combine_rows(h, gates, slot, meta_flat, final_gain, ys)


def _split3(v):
    hi = v.astype(BF16)
    r1 = v - hi.astype(F32)
    mid = r1.astype(BF16)
    lo = (r1 - mid.astype(F32)).astype(BF16)
    return hi, mid, lo


def _ssd_kernel(xbc_ref, dt_ref, z_ref, cbuf_ref, h0_ref, cw_ref, cb_ref, dtb_ref, alog_ref,
                dskip_ref, gn_ref, expand_ref, yn_ref, hfin_ref, h_scr, xext_scr, *, seq_valid):
    q = SSD_BLOCK
    c = pl.program_id(1)
    halo = SUBLANES

    @pl.when(c == 0)
    def _():
        h_scr[...] = h0_ref[0]
        xext_scr[0:halo, :] = cbuf_ref[0]

    xext_scr[halo:halo + q, :] = xbc_ref[0]
    xext = xext_scr[...]
    conv = cb_ref[...] + xext[halo:] * cw_ref[CONV_WIDTH - 1:CONV_WIDTH, :]
    for k in range(CONV_WIDTH - 1):
        delayed = pltpu.roll(xext, CONV_WIDTH - 1 - k, axis=0)
        conv = conv + delayed[halo:] * cw_ref[k:k + 1, :]
    xext_scr[0:halo, :] = xext_scr[q:q + halo, :]
    xbc = _silu(conv)
    xs = xbc[:, :D_INNER]
    b_all = xbc[:, D_INNER:D_INNER + SSM_GROUPS * D_STATE]
    c_all = xbc[:, D_INNER + SSM_GROUPS * D_STATE:]

    v = dt_ref[0] + dtb_ref[...]
    dt = jnp.maximum(v, 0.0) + jnp.log1p(jnp.exp(-jnp.abs(v)))
    row = lax.broadcasted_iota(I32, (q, LANES), 0)
    if seq_valid is not None:
        dt = jnp.where(row + c * q < seq_valid, dt, 0.0)
    a_neg = -jnp.exp(alog_ref[...])
    dta = dt * a_neg

    ti = lax.broadcasted_iota(I32, (q, q), 0)
    si = lax.broadcasted_iota(I32, (q, q), 1)
    causal = ti >= si
    tri = jnp.where(causal, 1.0, 0.0).astype(BF16)
    a_cs = sum(jnp.dot(tri, part, preferred_element_type=F32) for part in _split3(dta))
    a_last = a_cs[q - 1:q, :]
    a_cs_t = a_cs.T

    stacked = jnp.concatenate([dt, jnp.exp(a_cs), jnp.exp(a_last - a_cs),
                               jnp.broadcast_to(jnp.exp(a_last), (SUBLANES, LANES))], axis=0)
    s_hi = stacked.astype(BF16)
    s_lo = (stacked - s_hi.astype(F32)).astype(BF16)
    expand = expand_ref[...]
    expanded = (jnp.dot(s_hi, expand, preferred_element_type=F32)
                + jnp.dot(s_lo, expand, preferred_element_type=F32))
    dt_e = expanded[0:q]
    decay_in_e = expanded[q:2 * q]
    decay_out_e = expanded[2 * q:3 * q]
    decay_blk_e = expanded[3 * q:3 * q + 1]

    xdt = xs * dt_e
    xdt_bf = xdt.astype(BF16)
    wx = (xdt * decay_out_e).astype(BF16)
    low_half = lax.broadcasted_iota(I32, (q, LANES), 1) < SSM_HEAD_DIM

    y_groups = []
    for g in range(SSM_GROUPS):
        bg = b_all[:, g * D_STATE:(g + 1) * D_STATE]
        cg = c_all[:, g * D_STATE:(g + 1) * D_STATE].astype(BF16)
        bg_bf = bg.astype(BF16)
        cbm = lax.dot_general(cg, bg_bf, (((1,), (1,)), ((), ())), preferred_element_type=F32)
        cols = slice(g * GROUP_COLS, (g + 1) * GROUP_COLS)
        h_g = h_scr[g]
        y_g = jnp.dot(cg, h_g.astype(BF16), preferred_element_type=F32) * decay_in_e[:, cols]
        pair_out = []
        for pr in range(HEADS_PER_GROUP // 2):
            parts = []
            for par in range(2):
                h = g * HEADS_PER_GROUP + 2 * pr + par
                seg = (jnp.broadcast_to(a_cs[:, h:h + 1], (q, q))
                       - jnp.broadcast_to(a_cs_t[h:h + 1, :], (q, q)))
                m = (cbm * jnp.exp(jnp.where(causal, seg, -jnp.inf))).astype(BF16)
                pcols = slice((h // 2) * LANES, (h // 2 + 1) * LANES)
                parts.append(jnp.dot(m, xdt_bf[:, pcols], preferred_element_type=F32))
            pair_out.append(jnp.where(low_half, parts[0], parts[1]))
        y_groups.append(y_g + jnp.concatenate(pair_out, axis=1))
        h_scr[g] = (h_g * decay_blk_e[:, cols]
                    + jnp.dot(bg.T.astype(BF16), wx[:, cols], preferred_element_type=F32))

    y = jnp.concatenate(y_groups, axis=1) + dskip_ref[...] * xs
    yz = y * _silu(z_ref[0])
    yn_ref[0] = (yz * _rms_scale(yz) * gn_ref[...]).astype(yn_ref.dtype)

    @pl.when(c == pl.num_programs(1) - 1)
    def _():
        hfin_ref[0] = h_scr[...]


def _state_to_kernel_layout(h):
    b = h.shape[0]
    h = h.reshape(b, SSM_GROUPS, HEADS_PER_GROUP, SSM_HEAD_DIM, D_STATE)
    return h.transpose(0, 1, 4, 2, 3).reshape(b, SSM_GROUPS, D_STATE, GROUP_COLS)


def _state_from_kernel_layout(h):
    b = h.shape[0]
    h = h.reshape(b, SSM_GROUPS, D_STATE, HEADS_PER_GROUP, SSM_HEAD_DIM)
    return h.transpose(0, 1, 3, 4, 2).reshape(b, N_SSM_HEADS, SSM_HEAD_DIM, D_STATE)


def _pad_lanes(v):
    return jnp.pad(v.astype(F32), (0, LANES - v.shape[0])).reshape(1, LANES)


def ssd_mixer(xbc, dt_raw, z, conv_buf, h0, conv_w, conv_b, dt_bias, a_log, d_skip, gnorm):
    bsz, seq, _ = xbc.shape
    q = SSD_BLOCK
    seq_pad = -(-seq // q) * q
    seq_valid = None
    if seq_pad != seq:
        pad = ((0, 0), (0, seq_pad - seq), (0, 0))
        xbc, dt_raw, z = jnp.pad(xbc, pad), jnp.pad(dt_raw, pad), jnp.pad(z, pad)
        seq_valid = seq
    cbuf = jnp.pad(conv_buf, ((0, 0), (SUBLANES - (CONV_WIDTH - 1), 0), (0, 0)))
    expand = (jnp.arange(D_INNER)[None, :] // SSM_HEAD_DIM == jnp.arange(LANES)[:, None]).astype(BF16)
    d_e = jnp.repeat(d_skip.astype(F32), SSM_HEAD_DIM).reshape(1, D_INNER)
    state_spec = pl.BlockSpec((1, SSM_GROUPS, D_STATE, GROUP_COLS), lambda b, c: (b, 0, 0, 0))

    def const(shape):
        return pl.BlockSpec(shape, lambda b, c: (0,) * len(shape))

    yn, h_fin = pl.pallas_call(
        functools.partial(_ssd_kernel, seq_valid=seq_valid),
        grid=(bsz, seq_pad // q),
        in_specs=[pl.BlockSpec((1, q, CONV_DIM), lambda b, c: (b, c, 0)),
                  pl.BlockSpec((1, q, LANES), lambda b, c: (b, c, 0)),
                  pl.BlockSpec((1, q, D_INNER), lambda b, c: (b, c, 0)),
                  pl.BlockSpec((1, SUBLANES, CONV_DIM), lambda b, c: (b, 0, 0)),
                  state_spec,
                  const((CONV_WIDTH, CONV_DIM)), const((1, CONV_DIM)), const((1, LANES)), const((1, LANES)),
                  const((1, D_INNER)), const((1, D_INNER)), const((LANES, D_INNER))],
        out_specs=[pl.BlockSpec((1, q, D_INNER), lambda b, c: (b, c, 0)), state_spec],
        out_shape=[jax.ShapeDtypeStruct((bsz, seq_pad, D_INNER), BF16),
                   jax.ShapeDtypeStruct((bsz, SSM_GROUPS, D_STATE, GROUP_COLS), F32)],
        scratch_shapes=[pltpu.VMEM((SSM_GROUPS, D_STATE, GROUP_COLS), F32),
                        pltpu.VMEM((q + SUBLANES, CONV_DIM), F32)],
        compiler_params=_cparams("parallel", "arbitrary"),
        name="ssd_mixer",
    )(xbc, dt_raw, z, cbuf, _state_to_kernel_layout(h0.astype(F32)), conv_w.astype(F32),
      conv_b.astype(F32).reshape(1, CONV_DIM), _pad_lanes(dt_bias), _pad_lanes(a_log), d_e,
      gnorm.astype(F32).reshape(1, D_INNER), expand)
    return yn[:, :seq], _state_from_kernel_layout(h_fin)


_ALIBI_SLOPES = (2.0 ** (-8.0 * np.arange(1, N_Q_HEADS + 1, dtype=np.float32) / N_Q_HEADS)).astype(np.float32)


def _attn_bias(lbq, chunk, past_valid):
    w = WINDOW + lbq
    qi = np.arange(lbq)[:, None]
    si = np.arange(w)[None, :]
    dist = np.abs(WINDOW + qi - si).astype(np.float32)
    lo = (qi // chunk) * chunk
    visible = (si >= lo) & (si < lo + WINDOW + chunk)
    variants = []
    for first_block in (True, False):
        valid = visible & (si >= WINDOW) if (first_block and not past_valid) else visible
        per_head = [np.where(valid, -(_ALIBI_SLOPES[h] * dist), np.float32(NEG_INF)) for h in range(N_Q_HEADS)]
        variants.append(np.stack([np.concatenate([per_head[4 * kv + par], per_head[4 * kv + 2 + par]], axis=0)
                                  for kv in range(N_KV_HEADS) for par in range(2)]))
    return np.stack(variants).astype(np.float32)


def _attn_kernel(sink_ref, q_ref, k_ref, v_ref, bias_ref, o_ref, *, lbq):
    n = pl.program_id(1)
    w = WINDOW + lbq
    base = pl.multiple_of(n * lbq, lbq)
    kband = k_ref[0, pl.ds(base, w), :].astype(F32)
    vband = v_ref[0, pl.ds(base, w), :].astype(F32)
    first_pair = lax.broadcasted_iota(I32, (2 * lbq, 1), 0) < lbq
    lane = lax.broadcasted_iota(I32, (w, LANES), 1)
    low_half = lane < ATTN_HEAD_DIM

    for kv in range(N_KV_HEADS):
        pcols = slice((kv // 2) * LANES, (kv // 2 + 1) * LANES)
        kpair, vpair = kband[:, pcols], vband[:, pcols]
        kswap = pltpu.roll(kpair, ATTN_HEAD_DIM, axis=1)
        vswap = pltpu.roll(vpair, ATTN_HEAD_DIM, axis=1)
        in_low = kv % 2 == 0
        k_lo = jnp.where(low_half, kpair if in_low else kswap, 0.0).astype(BF16)
        k_hi = jnp.where(low_half, 0.0, kswap if in_low else kpair).astype(BF16)
        v_lo = jnp.where(low_half, vpair if in_low else vswap, 0.0).astype(BF16)
        v_hi = jnp.where(low_half, 0.0, vswap if in_low else vpair).astype(BF16)
        qs = q_ref[2 * kv:2 * kv + 2, 0].reshape(2 * lbq, LANES)
        out = None
        for par, (kx, vx) in enumerate(((k_lo, v_lo), (k_hi, v_hi))):
            h0, h1 = 4 * kv + par, 4 * kv + 2 + par
            sink = jnp.where(first_pair, sink_ref[h0], sink_ref[h1])
            s = lax.dot_general(qs, kx, (((1,), (1,)), ((), ())), preferred_element_type=F32)
            s = s + bias_ref[0, 2 * kv + par]
            m = jnp.maximum(jnp.max(s, axis=-1, keepdims=True), sink)
            p = jnp.exp(s - m)
            denom = jnp.sum(p, axis=-1, keepdims=True) + jnp.exp(sink - m)
            pn = (p / denom).astype(BF16)
            part = jnp.dot(pn, vx, preferred_element_type=F32)
            out = part if out is None else out + part
        o_ref[0, :, (2 * kv) * LANES:(2 * kv + 1) * LANES] = out[:lbq].astype(o_ref.dtype)
        o_ref[0, :, (2 * kv + 1) * LANES:(2 * kv + 2) * LANES] = out[lbq:].astype(o_ref.dtype)


def window_attention(q_pairs, k_all, v_all, sinks, *, lbq, chunk, past_valid):
    n_pairs, bsz, seq, _ = q_pairs.shape
    rows = k_all.shape[1]
    bias = jnp.asarray(_attn_bias(lbq, chunk, past_valid))
    return pl.pallas_call(
        functools.partial(_attn_kernel, lbq=lbq),
        grid_spec=pltpu.PrefetchScalarGridSpec(
            num_scalar_prefetch=1,
            grid=(bsz, seq // lbq),
            in_specs=[pl.BlockSpec((n_pairs, 1, lbq, LANES), lambda b, n, s: (0, b, n, 0)),
                      pl.BlockSpec((1, rows, KV_DIM), lambda b, n, s: (b, 0, 0)),
                      pl.BlockSpec((1, rows, KV_DIM), lambda b, n, s: (b, 0, 0)),
                      pl.BlockSpec((1,) + bias.shape[1:], lambda b, n, s: (jnp.minimum(n, 1), 0, 0, 0))],
            out_specs=pl.BlockSpec((1, lbq, D_MODEL), lambda b, n, s: (b, n, 0))),
        out_shape=jax.ShapeDtypeStruct((bsz, seq, D_MODEL), BF16),
        compiler_params=_cparams("parallel", "arbitrary"),
        name="window_attention",
    )(sinks.astype(F32), q_pairs, k_all, v_all, bias)


def _trunk(x, conv_buf, ssm_state, k_past, v_past, past_valid, p):
    bsz, seq, d = x.shape
    t = bsz * seq
    x2 = x.reshape(t, d)

    z, xbc, dt_raw = in_proj(x2, p['norm_mix'][0], p['w_z'], p['w_xbc'], p['w_dt'])
    xbc3 = xbc.reshape(bsz, seq, CONV_DIM)
    new_conv = jnp.concatenate([conv_buf.astype(F32), xbc3], axis=1)[:, seq:]
    yn, new_ssm = ssd_mixer(xbc3, dt_raw.reshape(bsz, seq, LANES), z.reshape(bsz, seq, D_INNER),
                            conv_buf.astype(F32), ssm_state, p['ssm_conv_w'], p['ssm_conv_b'],
                            p['ssm_dt_bias'], p['ssm_a_log'], p['ssm_d'], p['ssm_norm'])
    h = matmul_residual(yn.reshape(t, D_INNER), p['ssm_out_w'], x2)

    h = ffn(h, p['norm_ffn'][0], p['ffn_w_gate_up'], p['ffn_w_down'])

    kv, q_pairs = kv_q_proj(h, p['kv_norm'], p['norm_mix'][1], p['w_kv'], p['w_q'])
    kv = kv.reshape(bsz, seq, 2 * KV_DIM)
    k_new, v_new = kv[..., :KV_DIM], kv[..., KV_DIM:]
    k_all = jnp.concatenate([k_past.reshape(bsz, WINDOW, KV_DIM), k_new], axis=1).astype(BF16)
    v_all = jnp.concatenate([v_past.reshape(bsz, WINDOW, KV_DIM), v_new], axis=1).astype(BF16)

    chunk = min(ATTN_CHUNK, seq)
    lbq = 2 * chunk if seq % (2 * chunk) == 0 else chunk
    q_pairs = q_pairs.reshape(D_MODEL // LANES, bsz, seq, LANES)
    o = window_attention(q_pairs, k_all, v_all, p['attn_sinks'], lbq=lbq, chunk=chunk, past_valid=past_valid)
    h = matmul_residual(o.reshape(t, D_MODEL), p['w_o'], h)

    y = moe_and_final_norm(h, p['norm_ffn'][1], p['moe_router'], p['moe_w_gate_up'], p['moe_w_down'],
                           p['final_norm'])

    k4 = k_new.reshape(bsz, seq, N_KV_HEADS, ATTN_HEAD_DIM)
    v4 = v_new.reshape(bsz, seq, N_KV_HEADS, ATTN_HEAD_DIM)
    return y.reshape(bsz, seq, d), new_conv[None], new_ssm[None], k4, v4


def kernel(x_prompt, x_sample, state_conv, state_ssm, cache_k, cache_v, norm_mix, norm_ffn, ssm_in_w, ssm_conv_w,
           ssm_conv_b, ssm_dt_bias, ssm_a_log, ssm_d, ssm_norm, ssm_out_w, kv_norm, w_kv, w_q, attn_sinks, w_o,
           ffn_w_gate_up, ffn_w_down, moe_router, moe_w_gate_up, moe_w_down, final_norm):
    in_w = ssm_in_w[0]
    n_dt = N_SSM_HEADS
    p = {
        'norm_mix': norm_mix, 'norm_ffn': norm_ffn,
        'w_z': in_w[:, :D_INNER].astype(BF16),
        'w_xbc': in_w[:, D_INNER:D_INNER + CONV_DIM].astype(BF16),
        'w_dt': jnp.pad(in_w[:, D_INNER + CONV_DIM:], ((0, 0), (0, LANES - n_dt))).astype(BF16),
        'ssm_conv_w': ssm_conv_w[0], 'ssm_conv_b': ssm_conv_b[0], 'ssm_dt_bias': ssm_dt_bias[0],
        'ssm_a_log': ssm_a_log[0], 'ssm_d': ssm_d[0], 'ssm_norm': ssm_norm[0],
        'ssm_out_w': ssm_out_w[0].astype(BF16),
        'kv_norm': kv_norm, 'w_kv': w_kv.astype(BF16), 'w_q': w_q[0].astype(BF16),
        'attn_sinks': attn_sinks[0], 'w_o': w_o[0].astype(BF16),
        'ffn_w_gate_up': ffn_w_gate_up.astype(BF16), 'ffn_w_down': ffn_w_down.astype(BF16),
        'moe_router': jnp.pad(moe_router[0].astype(F32), ((0, 0), (0, LANES - N_EXPERTS))),
        'moe_w_gate_up': moe_w_gate_up[0].astype(BF16), 'moe_w_down': moe_w_down[0].astype(BF16),
        'final_norm': final_norm,
    }
    bsz, seq_p = x_prompt.shape[:2]
    dt = x_prompt.dtype
    zero_conv = jnp.zeros((bsz, CONV_WIDTH - 1, CONV_DIM), dt)
    zero_ssm = jnp.zeros((bsz, N_SSM_HEADS, SSM_HEAD_DIM, D_STATE), dt)
    zero_kv = jnp.zeros((bsz, WINDOW, N_KV_HEADS, ATTN_HEAD_DIM), dt)
    y_p, conv_p, ssm_p, k_p, v_p = _trunk(x_prompt, zero_conv, zero_ssm, zero_kv, zero_kv, False, p)
    keep = min(WINDOW, seq_p)
    y_s, conv_s, ssm_s, k_s, v_s = _trunk(x_sample, state_conv[0], state_ssm[0], cache_k, cache_v, True, p)
    return (y_p, y_s, conv_p, ssm_p, k_p[:, seq_p - keep:], v_p[:, seq_p - keep:], conv_s, ssm_s, k_s, v_s)
```

```python
import functools
import math

import jax
import jax.numpy as jnp
import numpy as np
from jax import lax
from jax.experimental import pallas as pl
from jax.experimental.pallas import tpu as pltpu

F32 = jnp.float32
BF16 = jnp.bfloat16
I32 = jnp.int32

D_MODEL = 1024
EPS = 1e-6
D_INNER = 2048
SSM_HEAD_DIM = 64
N_SSM_HEADS = 32
SSM_GROUPS = 4
HEADS_PER_GROUP = N_SSM_HEADS // SSM_GROUPS
D_STATE = 128
CONV_WIDTH = 4
CONV_DIM = D_INNER + 2 * SSM_GROUPS * D_STATE
ATTN_HEAD_DIM = 64
N_Q_HEADS = 16
N_KV_HEADS = 4
KV_DIM = N_KV_HEADS * ATTN_HEAD_DIM
WINDOW = 128
ATTN_CHUNK = 64
ATTN_SCALE = 1.0 / math.sqrt(ATTN_HEAD_DIM)
NEG_INF = -1e30
D_FF = 2816
N_EXPERTS = 8
TOP_K = 2

LANES = 128
SUBLANES = 8
VMEM_LIMIT_BYTES = 56 * 1024 * 1024
RUN_ALIGN = SUBLANES

SSD_BLOCK = 128
GROUP_COLS = HEADS_PER_GROUP * SSM_HEAD_DIM
FF_CHUNK = 512


def _cparams(*sem):
    return pltpu.CompilerParams(dimension_semantics=sem, vmem_limit_bytes=VMEM_LIMIT_BYTES)


def _row_tile(t, pref=512):
    return pref if t % pref == 0 else t


def _resident(shape):
    return pl.BlockSpec(shape, lambda *_: (0,) * len(shape), pipeline_mode=pl.Buffered(1))


def _rms_scale(x):
    return lax.rsqrt(jnp.mean(x * x, axis=-1, keepdims=True) + EPS)


def _silu(x):
    return x / (1.0 + jnp.exp(-x))


def _in_proj_kernel(x_ref, g_ref, wz_ref, wx_ref, wd_ref, z_ref, xbc_ref, dt_ref):
    x = x_ref[...]
    xn = (x * _rms_scale(x) * g_ref[...]).astype(BF16)
    z_ref[...] = jnp.dot(xn, wz_ref[...], preferred_element_type=F32)
    xbc_ref[...] = jnp.dot(xn, wx_ref[...], preferred_element_type=F32)
    dt_ref[...] = jnp.dot(xn, wd_ref[...], preferred_element_type=F32)


def in_proj(x, g, w_z, w_xbc, w_dt):
    t, k = x.shape
    tm = _row_tile(t)
    ws = (w_z, w_xbc, w_dt)
    return pl.pallas_call(
        _in_proj_kernel,
        grid=(t // tm,),
        in_specs=[pl.BlockSpec((tm, k), lambda i: (i, 0)), pl.BlockSpec((1, k), lambda i: (0, 0))]
        + [_resident(w.shape) for w in ws],
        out_specs=[pl.BlockSpec((tm, w.shape[1]), lambda i: (i, 0)) for w in ws],
        out_shape=[jax.ShapeDtypeStruct((t, w.shape[1]), F32) for w in ws],
        compiler_params=_cparams("parallel"),
        name="in_proj",
    )(x, g.reshape(1, k), *ws)


def _kvq_kernel(x_ref, gkv_ref, gq_ref, wkv_ref, wq_ref, kv_ref, q_ref):
    x = x_ref[...]
    xs = x * _rms_scale(x)
    kv_ref[...] = jnp.dot((xs * gkv_ref[...]).astype(BF16), wkv_ref[...], preferred_element_type=F32)
    q = jnp.dot((xs * gq_ref[...]).astype(BF16), wq_ref[...], preferred_element_type=F32)
    q = (q * ATTN_SCALE).astype(q_ref.dtype)
    for p in range(q_ref.shape[0]):
        q_ref[p] = q[:, p * LANES:(p + 1) * LANES]


def kv_q_proj(x, g_kv, g_q, w_kv, w_q):
    t, k = x.shape
    tm = _row_tile(t)
    n_pairs = w_q.shape[1] // LANES
    return pl.pallas_call(
        _kvq_kernel,
        grid=(t // tm,),
        in_specs=[pl.BlockSpec((tm, k), lambda i: (i, 0)),
                  pl.BlockSpec((1, k), lambda i: (0, 0)), pl.BlockSpec((1, k), lambda i: (0, 0)),
                  pl.BlockSpec(w_kv.shape, lambda i: (0, 0)), pl.BlockSpec(w_q.shape, lambda i: (0, 0))],
        out_specs=[pl.BlockSpec((tm, w_kv.shape[1]), lambda i: (i, 0)),
                   pl.BlockSpec((n_pairs, tm, LANES), lambda i: (0, i, 0))],
        out_shape=[jax.ShapeDtypeStruct((t, w_kv.shape[1]), F32),
                   jax.ShapeDtypeStruct((n_pairs, t, LANES), BF16)],
        compiler_params=_cparams("parallel"),
        name="kv_q_proj",
    )(x, g_kv.reshape(1, k), g_q.reshape(1, k), w_kv, w_q)


def _matmul_residual_kernel(a_ref, w_ref, r_ref, o_ref):
    o_ref[...] = r_ref[...] + jnp.dot(a_ref[...], w_ref[...], preferred_element_type=F32)


def matmul_residual(a, w, r):
    t, k = a.shape
    n = w.shape[1]
    tm = _row_tile(t)
    return pl.pallas_call(
        _matmul_residual_kernel,
        grid=(t // tm,),
        in_specs=[pl.BlockSpec((tm, k), lambda i: (i, 0)),
                  pl.BlockSpec((k, n), lambda i: (0, 0)),
                  pl.BlockSpec((tm, n), lambda i: (i, 0))],
        out_specs=pl.BlockSpec((tm, n), lambda i: (i, 0)),
        out_shape=jax.ShapeDtypeStruct((t, n), F32),
        compiler_params=_cparams("parallel"),
        name="matmul_residual",
    )(a, w, r)


def _swiglu(xb, wgu_ref, wd_ref):
    acc = None
    for c0 in range(0, D_FF, FF_CHUNK):
        cw = min(FF_CHUNK, D_FF - c0)
        gg = jnp.dot(xb, wgu_ref[0, :, c0:c0 + cw], preferred_element_type=F32)
        uu = jnp.dot(xb, wgu_ref[0, :, D_FF + c0:D_FF + c0 + cw], preferred_element_type=F32)
        act = (_silu(gg) * uu).astype(BF16)
        part = jnp.dot(act, wd_ref[0, c0:c0 + cw, :], preferred_element_type=F32)
        acc = part if acc is None else acc + part
    return acc


def _ffn_kernel(x_ref, g_ref, wgu_ref, wd_ref, o_ref):
    x = x_ref[...]
    xb = (x * _rms_scale(x) * g_ref[...]).astype(BF16)
    o_ref[...] = x + _swiglu(xb, wgu_ref, wd_ref)


def ffn(x, g, w_gu, w_down):
    t, d = x.shape
    tm = _row_tile(t)
    return pl.pallas_call(
        _ffn_kernel,
        grid=(t // tm,),
        in_specs=[pl.BlockSpec((tm, d), lambda i: (i, 0)),
                  pl.BlockSpec((1, d), lambda i: (0, 0)),
                  _resident(w_gu.shape), _resident(w_down.shape)],
        out_specs=pl.BlockSpec((tm, d), lambda i: (i, 0)),
        out_shape=jax.ShapeDtypeStruct((t, d), F32),
        compiler_params=_cparams("parallel"),
        name="ffn",
    )(x, g.reshape(1, d), w_gu, w_down)


def _expert_ffn_kernel(te_ref, na_ref, x_ref, wgu_ref, wd_ref, o_ref):
    active = pl.program_id(0) < na_ref[0]

    @pl.when(active)
    def _():
        o_ref[...] = _swiglu(x_ref[...].astype(BF16), wgu_ref, wd_ref)

    @pl.when(jnp.logical_not(active))
    def _():
        o_ref[...] = jnp.zeros_like(o_ref)


def expert_ffn(xs, tile_expert, n_active, w_gu, w_down, tm):
    rows, d = xs.shape
    return pl.pallas_call(
        _expert_ffn_kernel,
        grid_spec=pltpu.PrefetchScalarGridSpec(
            num_scalar_prefetch=2,
            grid=(rows // tm,),
            in_specs=[pl.BlockSpec((tm, d), lambda i, te, na: (jnp.minimum(i, na[0] - 1), 0)),
                      pl.BlockSpec((1, d, 2 * D_FF), lambda i, te, na: (te[i], 0, 0)),
                      pl.BlockSpec((1, D_FF, d), lambda i, te, na: (te[i], 0, 0))],
            out_specs=pl.BlockSpec((tm, d), lambda i, te, na: (i, 0))),
        out_shape=jax.ShapeDtypeStruct((rows, d), F32),
        compiler_params=_cparams("arbitrary"),
        name="expert_ffn",
    )(tile_expert, n_active, xs, w_gu, w_down)


def _router_kernel(a_ref, wo_ref, r_ref, g_ref, whi_ref, wlo_ref, h_ref, gate_ref, ids_ref, cnt_ref):
    x = r_ref[...] + jnp.dot(a_ref[...], wo_ref[...], preferred_element_type=F32)
    h_ref[...] = x
    xn = x * _rms_scale(x) * g_ref[...]
    x_hi = xn.astype(BF16)
    x_lo = (xn - x_hi.astype(F32)).astype(BF16)
    logits = (jnp.dot(x_hi, whi_ref[...], preferred_element_type=F32)
              + jnp.dot(x_lo, whi_ref[...], preferred_element_type=F32)
              + jnp.dot(x_hi, wlo_ref[...], preferred_element_type=F32))
    lane = lax.broadcasted_iota(I32, logits.shape, 1)
    logits = jnp.where(lane < N_EXPERTS, logits, -jnp.inf)
    m1 = jnp.max(logits, axis=-1, keepdims=True)
    i1 = jnp.min(jnp.where(logits == m1, lane, LANES), axis=-1, keepdims=True)
    rest = jnp.where(lane == i1, -jnp.inf, logits)
    m2 = jnp.max(rest, axis=-1, keepdims=True)
    i2 = jnp.min(jnp.where(rest == m2, lane, LANES), axis=-1, keepdims=True)
    e2 = jnp.exp(m2 - m1)
    w1 = 1.0 / (1.0 + e2)
    w2 = e2 / (1.0 + e2)
    gate_ref[...] = jnp.where(lane == 0, w1, jnp.where(lane == 1, w2, 0.0))
    ids_ref[...] = jnp.where(lane == 0, i1, jnp.where(lane == 1, i2, 0))
    chosen = jnp.where(jnp.logical_or(lane == i1, lane == i2), 1.0, 0.0)

    @pl.when(pl.program_id(0) == 0)
    def _():
        cnt_ref[...] = jnp.zeros_like(cnt_ref)

    n_tile = jnp.sum(chosen, axis=0, keepdims=True)
    n_pad = jnp.ceil(n_tile * (1.0 / RUN_ALIGN)) * RUN_ALIGN
    cnt_ref[...] += jnp.broadcast_to(n_pad, cnt_ref.shape)


def out_proj_router(a, w_o, r, g, w_router):
    t, k = a.shape
    d = w_o.shape[1]
    tm = _row_tile(t)
    w_hi = w_router.astype(BF16)
    w_lo = (w_router - w_hi.astype(F32)).astype(BF16)
    return pl.pallas_call(
        _router_kernel,
        grid=(t // tm,),
        in_specs=[pl.BlockSpec((tm, k), lambda i: (i, 0)),
                  _resident(w_o.shape),
                  pl.BlockSpec((tm, d), lambda i: (i, 0)),
                  pl.BlockSpec((1, d), lambda i: (0, 0)),
                  _resident((d, LANES)), _resident((d, LANES))],
        out_specs=[pl.BlockSpec((tm, d), lambda i: (i, 0)),
                   pl.BlockSpec((tm, LANES), lambda i: (i, 0)),
                   pl.BlockSpec((tm, LANES), lambda i: (i, 0)),
                   pl.BlockSpec((SUBLANES, LANES), lambda i: (0, 0))],
        out_shape=[jax.ShapeDtypeStruct((t, d), F32),
                   jax.ShapeDtypeStruct((t, LANES), F32),
                   jax.ShapeDtypeStruct((t, LANES), I32),
                   jax.ShapeDtypeStruct((SUBLANES, LANES), F32)],
        compiler_params=_cparams("arbitrary"),
        name="out_proj_router",
    )(a, w_o, r, g.reshape(1, d), w_hi, w_lo)


META_ROWS = 3


def _staging_rows(tt):
    need = TOP_K * tt + N_EXPERTS * (RUN_ALIGN - 1)
    return -(-need // LANES) * LANES


def _plan_kernel(ids_ref, base_ref, slot_ref, meta_ref, carry_ref):
    @pl.when(pl.program_id(0) == 0)
    def _():
        carry_ref[...] = jnp.zeros_like(carry_ref)

    ids = ids_ref[...]
    tt = ids.shape[0]
    lane = lax.broadcasted_iota(I32, ids.shape, 1)
    pick0 = lane == ids[:, 0:1]
    pick1 = lane == ids[:, 1:2]
    chosen = jnp.where(jnp.logical_or(pick0, pick1), 1.0, 0.0)
    ri = lax.broadcasted_iota(I32, (tt, tt), 0)
    ci = lax.broadcasted_iota(I32, (tt, tt), 1)
    before = jnp.where(ri > ci, 1.0, 0.0).astype(BF16)
    rank = jnp.dot(before, chosen.astype(BF16), preferred_element_type=F32)
    n_pad = jnp.ceil(jnp.sum(chosen, axis=0, keepdims=True) * (1.0 / RUN_ALIGN)) * RUN_ALIGN
    ei = lax.broadcasted_iota(I32, (LANES, LANES), 0)
    ej = lax.broadcasted_iota(I32, (LANES, LANES), 1)
    earlier = jnp.where(ei < ej, 1.0, 0.0).astype(BF16)
    n_pad8 = jnp.broadcast_to(n_pad, (SUBLANES, LANES))
    local = jnp.dot(n_pad8.astype(BF16), earlier, preferred_element_type=F32)[0:1, :]
    slot = local + rank
    s0 = jnp.sum(jnp.where(pick0, slot, 0.0), axis=-1, keepdims=True)
    s1 = jnp.sum(jnp.where(pick1, slot, 0.0), axis=-1, keepdims=True)
    slot_ref[...] = jnp.where(lane == 0, s0, jnp.where(lane == 1, s1, 0.0))
    row = lax.broadcasted_iota(I32, (SUBLANES, LANES), 0)
    sorted_start = base_ref[0:1, :] + carry_ref[0:1, :]
    meta = jnp.where(row == 0, sorted_start, jnp.where(row == 1, local, jnp.where(row == 2, n_pad, 0.0)))
    meta_ref[...] = meta.astype(I32)
    carry_ref[...] += n_pad8


def dispatch_plan(ids, base):
    t = ids.shape[0]
    tt = _row_tile(t)
    return pl.pallas_call(
        _plan_kernel,
        grid=(t // tt,),
        in_specs=[pl.BlockSpec((tt, LANES), lambda i: (i, 0)),
                  pl.BlockSpec((SUBLANES, LANES), lambda i: (0, 0))],
        out_specs=[pl.BlockSpec((tt, LANES), lambda i: (i, 0)),
                   pl.BlockSpec((SUBLANES, LANES), lambda i: (i, 0))],
        out_shape=[jax.ShapeDtypeStruct((t, LANES), F32),
                   jax.ShapeDtypeStruct((SUBLANES * (t // tt), LANES), I32)],
        scratch_shapes=[pltpu.VMEM((SUBLANES, LANES), F32)],
        compiler_params=_cparams("arbitrary"),
        name="dispatch_plan",
    )(ids, base)


def _run_copies(meta_ref, tile, staging, sorted_hbm, sem, to_sorted):
    first = tile * (META_ROWS * N_EXPERTS)
    out = []
    for e in range(N_EXPERTS):
        sorted_start = pl.multiple_of(meta_ref[first + e], RUN_ALIGN)
        local_start = pl.multiple_of(meta_ref[first + N_EXPERTS + e], RUN_ALIGN)
        n = pl.multiple_of(meta_ref[first + 2 * N_EXPERTS + e], RUN_ALIGN)
        a, b = staging.at[pl.ds(local_start, n)], sorted_hbm.at[pl.ds(sorted_start, n)]
        out.append((n, pltpu.make_async_copy(a, b, sem) if to_sorted else pltpu.make_async_copy(b, a, sem)))
    return out


def _start(copies):
    for n, copy in copies:
        pl.when(n > 0)(copy.start)


def _wait(copies):
    for n, copy in copies:
        pl.when(n > 0)(copy.wait)


def _start_then_wait(copies):
    _start(copies)
    _wait(copies)


def _scatter_kernel(meta_ref, tail_ref, x_ref, g_ref, slot_ref, xs_hbm, stage_scr, zero_scr, sem):
    i = pl.program_id(0)
    last = pl.num_programs(0) - 1
    buf = i % 2
    x = x_ref[...]
    xn = (x * _rms_scale(x) * g_ref[...]).astype(BF16)
    tt, rt = x.shape[0], stage_scr.shape[1]
    slot_t = slot_ref[...].T.astype(I32)
    r = lax.broadcasted_iota(I32, (rt, tt), 0)
    place = jnp.where(jnp.logical_or(r == slot_t[0:1, :], r == slot_t[1:2, :]), 1.0, 0.0).astype(BF16)
    stage_scr[buf] = jnp.dot(place, xn, preferred_element_type=F32)
    mine = _run_copies(meta_ref, i, stage_scr.at[buf], xs_hbm, sem.at[buf], to_sorted=True)
    _start(mine)

    @pl.when(i > 0)
    def _():
        _wait(_run_copies(meta_ref, i - 1, stage_scr.at[1 - buf], xs_hbm, sem.at[1 - buf], to_sorted=True))

    @pl.when(i == last)
    def _():
        _wait(mine)
        zero_scr[...] = jnp.zeros_like(zero_scr)
        tails = []
        for e in range(N_EXPERTS):
            start = pl.multiple_of(tail_ref[e], RUN_ALIGN)
            n = pl.multiple_of(tail_ref[N_EXPERTS + e], RUN_ALIGN)
            tails.append((n, pltpu.make_async_copy(zero_scr.at[pl.ds(0, n)], xs_hbm.at[pl.ds(start, n)], sem.at[0])))
        _start_then_wait(tails)

        tm = zero_scr.shape[0]

        def clear_tile(j, carry):
            copy = pltpu.make_async_copy(zero_scr, xs_hbm.at[pl.ds(pl.multiple_of(j * tm, tm), tm)], sem.at[0])
            copy.start()
            copy.wait()
            return carry

        lax.fori_loop(tail_ref[2 * N_EXPERTS], xs_hbm.shape[0] // tm, clear_tile, 0)


def scatter_rows(x, g, slot, meta, tails, n_rows, tm):
    t, d = x.shape
    tt = _row_tile(t)
    return pl.pallas_call(
        _scatter_kernel,
        grid_spec=pltpu.PrefetchScalarGridSpec(
            num_scalar_prefetch=2,
            grid=(t // tt,),
            in_specs=[pl.BlockSpec((tt, d), lambda i, m, tl: (i, 0)),
                      pl.BlockSpec((1, d), lambda i, m, tl: (0, 0)),
                      pl.BlockSpec((tt, LANES), lambda i, m, tl: (i, 0))],
            out_specs=pl.BlockSpec(memory_space=pl.ANY),
            scratch_shapes=[pltpu.VMEM((2, _staging_rows(tt), d), F32), pltpu.VMEM((tm, d), F32),
                            pltpu.SemaphoreType.DMA((2,))]),
        out_shape=jax.ShapeDtypeStruct((n_rows, d), F32),
        compiler_params=_cparams("arbitrary"),
        name="scatter_rows",
    )(meta, tails, x, g.reshape(1, d), slot)


def _combine_kernel(meta_ref, h_ref, gate_ref, slot_ref, fg_ref, ys_hbm, o_ref, stage_scr, sem):
    i = pl.program_id(0)
    buf = i % 2

    def fetch(tile, b):
        return _run_copies(meta_ref, tile, stage_scr.at[b], ys_hbm, sem.at[b], to_sorted=False)

    @pl.when(i == 0)
    def _():
        stage_scr[...] = jnp.zeros_like(stage_scr)
        _start(fetch(0, 0))

    @pl.when(i + 1 < pl.num_programs(0))
    def _():
        _start(fetch(i + 1, 1 - buf))

    tt, rt = h_ref.shape[0], stage_scr.shape[1]
    slot = slot_ref[...].astype(I32)
    gate = gate_ref[...]
    c = lax.broadcasted_iota(I32, (tt, rt), 1)
    weight = (jnp.where(c == slot[:, 0:1], gate[:, 0:1], 0.0)
              + jnp.where(c == slot[:, 1:2], gate[:, 1:2], 0.0))
    w_hi = weight.astype(BF16)
    w_lo = (weight - w_hi.astype(F32)).astype(BF16)
    _wait(fetch(i, buf))
    yb = stage_scr[buf].astype(BF16)
    moe = jnp.dot(w_hi, yb, preferred_element_type=F32) + jnp.dot(w_lo, yb, preferred_element_type=F32)
    y = h_ref[...] + moe
    o_ref[...] = y * _rms_scale(y) * fg_ref[...]


def combine_rows(h, gates, slot, meta, final_gain, ys):
    t, d = h.shape
    tt = _row_tile(t)
    return pl.pallas_call(
        _combine_kernel,
        grid_spec=pltpu.PrefetchScalarGridSpec(
            num_scalar_prefetch=1,
            grid=(t // tt,),
            in_specs=[pl.BlockSpec((tt, d), lambda i, m: (i, 0)),
                      pl.BlockSpec((tt, LANES), lambda i, m: (i, 0)),
                      pl.BlockSpec((tt, LANES), lambda i, m: (i, 0)),
                      pl.BlockSpec((1, d), lambda i, m: (0, 0)),
                      pl.BlockSpec(memory_space=pl.ANY)],
            out_specs=pl.BlockSpec((tt, d), lambda i, m: (i, 0)),
            scratch_shapes=[pltpu.VMEM((2, _staging_rows(tt), d), F32), pltpu.SemaphoreType.DMA((2,))]),
        out_shape=jax.ShapeDtypeStruct((t, d), F32),
        compiler_params=_cparams("arbitrary"),
        name="combine_rows",
    )(meta, h, gates, slot, final_gain.reshape(1, d), ys)


def attn_out_moe_final_norm(o, w_o, r, g, w_router, w_gu, w_down, final_gain):
    t, d = r.shape
    tt = _row_tile(t)
    n_tok_tiles = t // tt
    tm = 512 if t >= 4096 else 128
    worst = TOP_K * t + n_tok_tiles * N_EXPERTS * (RUN_ALIGN - 1) + N_EXPERTS * tm
    n_rows = -(-worst // tm) * tm
    h, gates, ids, counts = out_proj_router(o, w_o, r, g, w_router)

    cnt = counts[0, :N_EXPERTS].astype(I32)
    tiles = (cnt + tm - 1) // tm
    ends = jnp.cumsum(tiles)
    starts = (ends - tiles) * tm
    base = jnp.zeros((SUBLANES, LANES), F32).at[:, :N_EXPERTS].set(starts.astype(F32))
    n_active = ends[-1:]
    tile_idx = jnp.arange(n_rows // tm, dtype=I32)
    owner = jnp.sum(tile_idx[:, None] >= ends[None, :], axis=1).astype(I32)
    last_owner = jnp.sum(n_active - 1 >= ends).astype(I32)
    tile_expert = jnp.where(tile_idx < n_active, owner, last_owner)
    tails = jnp.concatenate([starts + cnt, tiles * tm - cnt, n_active]).astype(I32)

    slot, meta = dispatch_plan(ids, base)
    meta_flat = meta.reshape(n_tok_tiles, SUBLANES, LANES)[:, :META_ROWS, :N_EXPERTS].reshape(-1)
    xs = scatter_rows(h, g, slot, meta_flat, tails, n_rows, tm)
    ys = expert_ffn(xs, tile_expert, n_active, w_gu, w_down, tm)
    return combine_rows(h, gates, slot, meta_flat, final_gain, ys)


def _split3(v):
    hi = v.astype(BF16)
    r1 = v - hi.astype(F32)
    mid = r1.astype(BF16)
    lo = (r1 - mid.astype(F32)).astype(BF16)
    return hi, mid, lo


def _ssd_kernel(xbc_ref, dt_ref, z_ref, cbuf_ref, h0_ref, cw_ref, cb_ref, dtb_ref, alog_ref,
                dskip_ref, gn_ref, expand_ref, yn_ref, hfin_ref, h_scr, xext_scr, *, seq_valid):
    q = SSD_BLOCK
    c = pl.program_id(1)
    halo = SUBLANES

    @pl.when(c == 0)
    def _():
        h_scr[...] = h0_ref[0]
        xext_scr[0:halo, :] = cbuf_ref[0]

    xext_scr[halo:halo + q, :] = xbc_ref[0]
    xext = xext_scr[...]
    conv = cb_ref[...] + xext[halo:] * cw_ref[CONV_WIDTH - 1:CONV_WIDTH, :]
    for k in range(CONV_WIDTH - 1):
        delayed = pltpu.roll(xext, CONV_WIDTH - 1 - k, axis=0)
        conv = conv + delayed[halo:] * cw_ref[k:k + 1, :]
    xext_scr[0:halo, :] = xext_scr[q:q + halo, :]
    xbc = _silu(conv)
    xs = xbc[:, :D_INNER]
    b_all = xbc[:, D_INNER:D_INNER + SSM_GROUPS * D_STATE]
    c_all = xbc[:, D_INNER + SSM_GROUPS * D_STATE:]

    v = dt_ref[0] + dtb_ref[...]
    dt = jnp.maximum(v, 0.0) + jnp.log1p(jnp.exp(-jnp.abs(v)))
    row = lax.broadcasted_iota(I32, (q, LANES), 0)
    if seq_valid is not None:
        dt = jnp.where(row + c * q < seq_valid, dt, 0.0)
    a_neg = -jnp.exp(alog_ref[...])
    dta = dt * a_neg

    ti = lax.broadcasted_iota(I32, (q, q), 0)
    si = lax.broadcasted_iota(I32, (q, q), 1)
    causal = ti >= si
    tri = jnp.where(causal, 1.0, 0.0).astype(BF16)
    a_cs = sum(jnp.dot(tri, part, preferred_element_type=F32) for part in _split3(dta))
    a_last = a_cs[q - 1:q, :]
    a_cs_t = a_cs.T

    stacked = jnp.concatenate([dt, jnp.exp(a_cs), jnp.exp(a_last - a_cs),
                               jnp.broadcast_to(jnp.exp(a_last), (SUBLANES, LANES))], axis=0)
    s_hi = stacked.astype(BF16)
    s_lo = (stacked - s_hi.astype(F32)).astype(BF16)
    expand = expand_ref[...]
    expanded = (jnp.dot(s_hi, expand, preferred_element_type=F32)
                + jnp.dot(s_lo, expand, preferred_element_type=F32))
    dt_e = expanded[0:q]
    decay_in_e = expanded[q:2 * q]
    decay_out_e = expanded[2 * q:3 * q]
    decay_blk_e = expanded[3 * q:3 * q + 1]

    xdt = xs * dt_e
    xdt_bf = xdt.astype(BF16)
    wx = (xdt * decay_out_e).astype(BF16)
    low_half = lax.broadcasted_iota(I32, (q, LANES), 1) < SSM_HEAD_DIM

    y_groups = []
    for g in range(SSM_GROUPS):
        bg = b_all[:, g * D_STATE:(g + 1) * D_STATE]
        cg = c_all[:, g * D_STATE:(g + 1) * D_STATE].astype(BF16)
        bg_bf = bg.astype(BF16)
        cbm = lax.dot_general(cg, bg_bf, (((1,), (1,)), ((), ())), preferred_element_type=F32)
        cols = slice(g * GROUP_COLS, (g + 1) * GROUP_COLS)
        h_g = h_scr[g]
        y_g = jnp.dot(cg, h_g.astype(BF16), preferred_element_type=F32) * decay_in_e[:, cols]
        pair_out = []
        for pr in range(HEADS_PER_GROUP // 2):
            parts = []
            for par in range(2):
                h = g * HEADS_PER_GROUP + 2 * pr + par
                seg = (jnp.broadcast_to(a_cs[:, h:h + 1], (q, q))
                       - jnp.broadcast_to(a_cs_t[h:h + 1, :], (q, q)))
                m = (cbm * jnp.exp(jnp.where(causal, seg, -jnp.inf))).astype(BF16)
                pcols = slice((h // 2) * LANES, (h // 2 + 1) * LANES)
                parts.append(jnp.dot(m, xdt_bf[:, pcols], preferred_element_type=F32))
            pair_out.append(jnp.where(low_half, parts[0], parts[1]))
        y_groups.append(y_g + jnp.concatenate(pair_out, axis=1))
        h_scr[g] = (h_g * decay_blk_e[:, cols]
                    + jnp.dot(bg.T.astype(BF16), wx[:, cols], preferred_element_type=F32))

    y = jnp.concatenate(y_groups, axis=1) + dskip_ref[...] * xs
    yz = y * _silu(z_ref[0])
    yn_ref[0] = (yz * _rms_scale(yz) * gn_ref[...]).astype(yn_ref.dtype)

    @pl.when(c == pl.num_programs(1) - 1)
    def _():
        hfin_ref[0] = h_scr[...]


def _state_to_kernel_layout(h):
    b = h.shape[0]
    h = h.reshape(b, SSM_GROUPS, HEADS_PER_GROUP, SSM_HEAD_DIM, D_STATE)
    return h.transpose(0, 1, 4, 2, 3).reshape(b, SSM_GROUPS, D_STATE, GROUP_COLS)


def _state_from_kernel_layout(h):
    b = h.shape[0]
    h = h.reshape(b, SSM_GROUPS, D_STATE, HEADS_PER_GROUP, SSM_HEAD_DIM)
    return h.transpose(0, 1, 3, 4, 2).reshape(b, N_SSM_HEADS, SSM_HEAD_DIM, D_STATE)


def _pad_lanes(v):
    return jnp.pad(v.astype(F32), (0, LANES - v.shape[0])).reshape(1, LANES)


def ssd_mixer(xbc, dt_raw, z, conv_buf, h0, conv_w, conv_b, dt_bias, a_log, d_skip, gnorm):
    bsz, seq, _ = xbc.shape
    q = SSD_BLOCK
    seq_pad = -(-seq // q) * q
    seq_valid = None
    if seq_pad != seq:
        pad = ((0, 0), (0, seq_pad - seq), (0, 0))
        xbc, dt_raw, z = jnp.pad(xbc, pad), jnp.pad(dt_raw, pad), jnp.pad(z, pad)
        seq_valid = seq
    cbuf = jnp.pad(conv_buf, ((0, 0), (SUBLANES - (CONV_WIDTH - 1), 0), (0, 0)))
    expand = (jnp.arange(D_INNER)[None, :] // SSM_HEAD_DIM == jnp.arange(LANES)[:, None]).astype(BF16)
    d_e = jnp.repeat(d_skip.astype(F32), SSM_HEAD_DIM).reshape(1, D_INNER)
    state_spec = pl.BlockSpec((1, SSM_GROUPS, D_STATE, GROUP_COLS), lambda b, c: (b, 0, 0, 0))

    def const(shape):
        return pl.BlockSpec(shape, lambda b, c: (0,) * len(shape))

    yn, h_fin = pl.pallas_call(
        functools.partial(_ssd_kernel, seq_valid=seq_valid),
        grid=(bsz, seq_pad // q),
        in_specs=[pl.BlockSpec((1, q, CONV_DIM), lambda b, c: (b, c, 0)),
                  pl.BlockSpec((1, q, LANES), lambda b, c: (b, c, 0)),
                  pl.BlockSpec((1, q, D_INNER), lambda b, c: (b, c, 0)),
                  pl.BlockSpec((1, SUBLANES, CONV_DIM), lambda b, c: (b, 0, 0)),
                  state_spec,
                  const((CONV_WIDTH, CONV_DIM)), const((1, CONV_DIM)), const((1, LANES)), const((1, LANES)),
                  const((1, D_INNER)), const((1, D_INNER)), const((LANES, D_INNER))],
        out_specs=[pl.BlockSpec((1, q, D_INNER), lambda b, c: (b, c, 0)), state_spec],
        out_shape=[jax.ShapeDtypeStruct((bsz, seq_pad, D_INNER), BF16),
                   jax.ShapeDtypeStruct((bsz, SSM_GROUPS, D_STATE, GROUP_COLS), F32)],
        scratch_shapes=[pltpu.VMEM((SSM_GROUPS, D_STATE, GROUP_COLS), F32),
                        pltpu.VMEM((q + SUBLANES, CONV_DIM), F32)],
        compiler_params=_cparams("parallel", "arbitrary"),
        name="ssd_mixer",
    )(xbc, dt_raw, z, cbuf, _state_to_kernel_layout(h0.astype(F32)), conv_w.astype(F32),
      conv_b.astype(F32).reshape(1, CONV_DIM), _pad_lanes(dt_bias), _pad_lanes(a_log), d_e,
      gnorm.astype(F32).reshape(1, D_INNER), expand)
    return yn[:, :seq], _state_from_kernel_layout(h_fin)


_ALIBI_SLOPES = (2.0 ** (-8.0 * np.arange(1, N_Q_HEADS + 1, dtype=np.float32) / N_Q_HEADS)).astype(np.float32)


def _attn_bias(lbq, chunk, past_valid):
    w = WINDOW + lbq
    qi = np.arange(lbq)[:, None]
    si = np.arange(w)[None, :]
    dist = np.abs(WINDOW + qi - si).astype(np.float32)
    lo = (qi // chunk) * chunk
    visible = (si >= lo) & (si < lo + WINDOW + chunk)
    variants = []
    for first_block in (True, False):
        valid = visible & (si >= WINDOW) if (first_block and not past_valid) else visible
        per_head = [np.where(valid, -(_ALIBI_SLOPES[h] * dist), np.float32(NEG_INF)) for h in range(N_Q_HEADS)]
        variants.append(np.stack([np.concatenate([per_head[4 * kv + par], per_head[4 * kv + 2 + par]], axis=0)
                                  for kv in range(N_KV_HEADS) for par in range(2)]))
    return np.stack(variants).astype(np.float32)


def _attn_kernel(sink_ref, q_ref, k_ref, v_ref, bias_ref, o_ref, *, lbq):
    n = pl.program_id(1)
    w = WINDOW + lbq
    base = pl.multiple_of(n * lbq, lbq)
    kband = k_ref[0, pl.ds(base, w), :].astype(F32)
    vband = v_ref[0, pl.ds(base, w), :].astype(F32)
    first_pair = lax.broadcasted_iota(I32, (2 * lbq, 1), 0) < lbq
    lane = lax.broadcasted_iota(I32, (w, LANES), 1)
    low_half = lane < ATTN_HEAD_DIM

    for kv in range(N_KV_HEADS):
        pcols = slice((kv // 2) * LANES, (kv // 2 + 1) * LANES)
        kpair, vpair = kband[:, pcols], vband[:, pcols]
        kswap = pltpu.roll(kpair, ATTN_HEAD_DIM, axis=1)
        vswap = pltpu.roll(vpair, ATTN_HEAD_DIM, axis=1)
        in_low = kv % 2 == 0
        k_lo = jnp.where(low_half, kpair if in_low else kswap, 0.0).astype(BF16)
        k_hi = jnp.where(low_half, 0.0, kswap if in_low else kpair).astype(BF16)
        v_lo = jnp.where(low_half, vpair if in_low else vswap, 0.0).astype(BF16)
        v_hi = jnp.where(low_half, 0.0, vswap if in_low else vpair).astype(BF16)
        qs = q_ref[2 * kv:2 * kv + 2, 0].reshape(2 * lbq, LANES)
        out = None
        for par, (kx, vx) in enumerate(((k_lo, v_lo), (k_hi, v_hi))):
            h0, h1 = 4 * kv + par, 4 * kv + 2 + par
            sink = jnp.where(first_pair, sink_ref[h0], sink_ref[h1])
            s = lax.dot_general(qs, kx, (((1,), (1,)), ((), ())), preferred_element_type=F32)
            s = s + bias_ref[0, 2 * kv + par]
            m = jnp.maximum(jnp.max(s, axis=-1, keepdims=True), sink)
            p = jnp.exp(s - m)
            denom = jnp.sum(p, axis=-1, keepdims=True) + jnp.exp(sink - m)
            pn = (p / denom).astype(BF16)
            part = jnp.dot(pn, vx, preferred_element_type=F32)
            out = part if out is None else out + part
        o_ref[0, :, (2 * kv) * LANES:(2 * kv + 1) * LANES] = out[:lbq].astype(o_ref.dtype)
        o_ref[0, :, (2 * kv + 1) * LANES:(2 * kv + 2) * LANES] = out[lbq:].astype(o_ref.dtype)


def window_attention(q_pairs, k_all, v_all, sinks, *, lbq, chunk, past_valid):
    n_pairs, bsz, seq, _ = q_pairs.shape
    rows = k_all.shape[1]
    bias = jnp.asarray(_attn_bias(lbq, chunk, past_valid))
    return pl.pallas_call(
        functools.partial(_attn_kernel, lbq=lbq),
        grid_spec=pltpu.PrefetchScalarGridSpec(
            num_scalar_prefetch=1,
            grid=(bsz, seq // lbq),
            in_specs=[pl.BlockSpec((n_pairs, 1, lbq, LANES), lambda b, n, s: (0, b, n, 0)),
                      pl.BlockSpec((1, rows, KV_DIM), lambda b, n, s: (b, 0, 0)),
                      pl.BlockSpec((1, rows, KV_DIM), lambda b, n, s: (b, 0, 0)),
                      pl.BlockSpec((1,) + bias.shape[1:], lambda b, n, s: (jnp.minimum(n, 1), 0, 0, 0))],
            out_specs=pl.BlockSpec((1, lbq, D_MODEL), lambda b, n, s: (b, n, 0))),
        out_shape=jax.ShapeDtypeStruct((bsz, seq, D_MODEL), BF16),
        compiler_params=_cparams("parallel", "arbitrary"),
        name="window_attention",
    )(sinks.astype(F32), q_pairs, k_all, v_all, bias)


def _trunk(x, conv_buf, ssm_state, k_past, v_past, past_valid, p):
    bsz, seq, d = x.shape
    t = bsz * seq
    x2 = x.reshape(t, d)

    z, xbc, dt_raw = in_proj(x2, p['norm_mix'][0], p['w_z'], p['w_xbc'], p['w_dt'])
    xbc3 = xbc.reshape(bsz, seq, CONV_DIM)
    new_conv = jnp.concatenate([conv_buf.astype(F32), xbc3], axis=1)[:, seq:]
    yn, new_ssm = ssd_mixer(xbc3, dt_raw.reshape(bsz, seq, LANES), z.reshape(bsz, seq, D_INNER),
                            conv_buf.astype(F32), ssm_state, p['ssm_conv_w'], p['ssm_conv_b'],
                            p['ssm_dt_bias'], p['ssm_a_log'], p['ssm_d'], p['ssm_norm'])
    h = matmul_residual(yn.reshape(t, D_INNER), p['ssm_out_w'], x2)

    h = ffn(h, p['norm_ffn'][0], p['ffn_w_gate_up'], p['ffn_w_down'])

    kv, q_pairs = kv_q_proj(h, p['kv_norm'], p['norm_mix'][1], p['w_kv'], p['w_q'])
    kv = kv.reshape(bsz, seq, 2 * KV_DIM)
    k_new, v_new = kv[..., :KV_DIM], kv[..., KV_DIM:]
    k_all = jnp.concatenate([k_past.reshape(bsz, WINDOW, KV_DIM), k_new], axis=1).astype(BF16)
    v_all = jnp.concatenate([v_past.reshape(bsz, WINDOW, KV_DIM), v_new], axis=1).astype(BF16)

    chunk = min(ATTN_CHUNK, seq)
    lbq = 2 * chunk if seq % (2 * chunk) == 0 else chunk
    q_pairs = q_pairs.reshape(D_MODEL // LANES, bsz, seq, LANES)
    o = window_attention(q_pairs, k_all, v_all, p['attn_sinks'], lbq=lbq, chunk=chunk, past_valid=past_valid)

    y = attn_out_moe_final_norm(o.reshape(t, D_MODEL), p['w_o'], h, p['norm_ffn'][1], p['moe_router'],
                                p['moe_w_gate_up'], p['moe_w_down'], p['final_norm'])

    k4 = k_new.reshape(bsz, seq, N_KV_HEADS, ATTN_HEAD_DIM)
    v4 = v_new.reshape(bsz, seq, N_KV_HEADS, ATTN_HEAD_DIM)
    return y.reshape(bsz, seq, d), new_conv[None], new_ssm[None], k4, v4


def kernel(x_prompt, x_sample, state_conv, state_ssm, cache_k, cache_v, norm_mix, norm_ffn, ssm_in_w, ssm_conv_w,
           ssm_conv_b, ssm_dt_bias, ssm_a_log, ssm_d, ssm_norm, ssm_out_w, kv_norm, w_kv, w_q, attn_sinks, w_o,
           ffn_w_gate_up, ffn_w_down, moe_router, moe_w_gate_up, moe_w_down, final_norm):
    in_w = ssm_in_w[0]
    n_dt = N_SSM_HEADS
    p = {
        'norm_mix': norm_mix, 'norm_ffn': norm_ffn,
        'w_z': in_w[:, :D_INNER].astype(BF16),
        'w_xbc': in_w[:, D_INNER:D_INNER + CONV_DIM].astype(BF16),
        'w_dt': jnp.pad(in_w[:, D_INNER + CONV_DIM:], ((0, 0), (0, LANES - n_dt))).astype(BF16),
        'ssm_conv_w': ssm_conv_w[0], 'ssm_conv_b': ssm_conv_b[0], 'ssm_dt_bias': ssm_dt_bias[0],
        'ssm_a_log': ssm_a_log[0], 'ssm_d': ssm_d[0], 'ssm_norm': ssm_norm[0],
        'ssm_out_w': ssm_out_w[0].astype(BF16),
        'kv_norm': kv_norm, 'w_kv': w_kv.astype(BF16), 'w_q': w_q[0].astype(BF16),
        'attn_sinks': attn_sinks[0], 'w_o': w_o[0].astype(BF16),
        'ffn_w_gate_up': ffn_w_gate_up.astype(BF16), 'ffn_w_down': ffn_w_down.astype(BF16),
        'moe_router': jnp.pad(moe_router[0].astype(F32), ((0, 0), (0, LANES - N_EXPERTS))),
        'moe_w_gate_up': moe_w_gate_up[0].astype(BF16), 'moe_w_down': moe_w_down[0].astype(BF16),
        'final_norm': final_norm,
    }
    bsz, seq_p = x_prompt.shape[:2]
    dt = x_prompt.dtype
    zero_conv = jnp.zeros((bsz, CONV_WIDTH - 1, CONV_DIM), dt)
    zero_ssm = jnp.zeros((bsz, N_SSM_HEADS, SSM_HEAD_DIM, D_STATE), dt)
    zero_kv = jnp.zeros((bsz, WINDOW, N_KV_HEADS, ATTN_HEAD_DIM), dt)
    y_p, conv_p, ssm_p, k_p, v_p = _trunk(x_prompt, zero_conv, zero_ssm, zero_kv, zero_kv, False, p)
    keep = min(WINDOW, seq_p)
    y_s, conv_s, ssm_s, k_s, v_s = _trunk(x_sample, state_conv[0], state_ssm[0], cache_k, cache_v, True, p)
    return (y_p, y_s, conv_p, ssm_p, k_p[:, seq_p - keep:], v_p[:, seq_p - keep:], conv_s, ssm_s, k_s, v_s)
```

```python
import functools
import math

import jax
import jax.numpy as jnp
import numpy as np
from jax import lax
from jax.experimental import pallas as pl
from jax.experimental.pallas import tpu as pltpu

F32 = jnp.float32
BF16 = jnp.bfloat16
I32 = jnp.int32

D_MODEL = 1024
EPS = 1e-6
D_INNER = 2048
SSM_HEAD_DIM = 64
N_SSM_HEADS = 32
SSM_GROUPS = 4
HEADS_PER_GROUP = N_SSM_HEADS // SSM_GROUPS
D_STATE = 128
CONV_WIDTH = 4
CONV_DIM = D_INNER + 2 * SSM_GROUPS * D_STATE
ATTN_HEAD_DIM = 64
N_Q_HEADS = 16
N_KV_HEADS = 4
KV_DIM = N_KV_HEADS * ATTN_HEAD_DIM
WINDOW = 128
ATTN_CHUNK = 64
ATTN_SCALE = 1.0 / math.sqrt(ATTN_HEAD_DIM)
NEG_INF = -1e30
D_FF = 2816
N_EXPERTS = 8
TOP_K = 2

LANES = 128
SUBLANES = 8
VMEM_LIMIT_BYTES = 56 * 1024 * 1024
RUN_ALIGN = SUBLANES

SSD_BLOCK = 128
GROUP_COLS = HEADS_PER_GROUP * SSM_HEAD_DIM
FF_CHUNK = 512


def _cparams(*sem):
    return pltpu.CompilerParams(dimension_semantics=sem, vmem_limit_bytes=VMEM_LIMIT_BYTES)


def _row_tile(t, pref=512):
    return pref if t % pref == 0 else t


def _resident(shape):
    return pl.BlockSpec(shape, lambda *_: (0,) * len(shape), pipeline_mode=pl.Buffered(1))


def _rms_scale(x):
    return lax.rsqrt(jnp.mean(x * x, axis=-1, keepdims=True) + EPS)


def _silu(x):
    return x / (1.0 + jnp.exp(-x))


def _in_proj_kernel(x_ref, g_ref, wz_ref, wx_ref, wd_ref, z_ref, xbc_ref, dt_ref):
    x = x_ref[...]
    xn = (x * _rms_scale(x) * g_ref[...]).astype(BF16)
    z_ref[...] = jnp.dot(xn, wz_ref[...], preferred_element_type=F32)
    xbc_ref[...] = jnp.dot(xn, wx_ref[...], preferred_element_type=F32)
    dt_ref[...] = jnp.dot(xn, wd_ref[...], preferred_element_type=F32)


def in_proj(x, g, w_z, w_xbc, w_dt):
    t, k = x.shape
    tm = _row_tile(t)
    ws = (w_z, w_xbc, w_dt)
    return pl.pallas_call(
        _in_proj_kernel,
        grid=(t // tm,),
        in_specs=[pl.BlockSpec((tm, k), lambda i: (i, 0)), pl.BlockSpec((1, k), lambda i: (0, 0))]
        + [_resident(w.shape) for w in ws],
        out_specs=[pl.BlockSpec((tm, w.shape[1]), lambda i: (i, 0)) for w in ws],
        out_shape=[jax.ShapeDtypeStruct((t, w.shape[1]), F32) for w in ws],
        compiler_params=_cparams("parallel"),
        name="in_proj",
    )(x, g.reshape(1, k), *ws)


def _swiglu(xb, wgu_ref, wd_ref):
    acc = None
    for c0 in range(0, D_FF, FF_CHUNK):
        cw = min(FF_CHUNK, D_FF - c0)
        gg = jnp.dot(xb, wgu_ref[0, :, c0:c0 + cw], preferred_element_type=F32)
        uu = jnp.dot(xb, wgu_ref[0, :, D_FF + c0:D_FF + c0 + cw], preferred_element_type=F32)
        act = (_silu(gg) * uu).astype(BF16)
        part = jnp.dot(act, wd_ref[0, c0:c0 + cw, :], preferred_element_type=F32)
        acc = part if acc is None else acc + part
    return acc


def _layer0_tail_kernel(a_ref, wo_ref, r_ref, gf_ref, wgu_ref, wd_ref, gkv_ref, gq_ref, wkv_ref, wq_ref,
                        h_ref, kv_ref, q_ref):
    h1 = r_ref[...] + jnp.dot(a_ref[...], wo_ref[...], preferred_element_type=F32)
    xb = (h1 * _rms_scale(h1) * gf_ref[...]).astype(BF16)
    h2 = h1 + _swiglu(xb, wgu_ref, wd_ref)
    h_ref[...] = h2
    xs = h2 * _rms_scale(h2)
    kv_ref[...] = jnp.dot((xs * gkv_ref[...]).astype(BF16), wkv_ref[...], preferred_element_type=F32)
    q = jnp.dot((xs * gq_ref[...]).astype(BF16), wq_ref[...], preferred_element_type=F32)
    q = (q * ATTN_SCALE).astype(q_ref.dtype)
    for p in range(q_ref.shape[0]):
        q_ref[p] = q[:, p * LANES:(p + 1) * LANES]


def layer0_tail(a, w_out, r, g_ffn, w_gu, w_down, g_kv, g_q, w_kv, w_q):
    t, k = a.shape
    d = w_out.shape[1]
    tm = _row_tile(t)
    n_pairs = w_q.shape[1] // LANES

    def gain():
        return pl.BlockSpec((1, d), lambda i: (0, 0))

    return pl.pallas_call(
        _layer0_tail_kernel,
        grid=(t // tm,),
        in_specs=[pl.BlockSpec((tm, k), lambda i: (i, 0)), _resident(w_out.shape),
                  pl.BlockSpec((tm, d), lambda i: (i, 0)), gain(),
                  _resident(w_gu.shape), _resident(w_down.shape), gain(), gain(),
                  _resident(w_kv.shape), _resident(w_q.shape)],
        out_specs=[pl.BlockSpec((tm, d), lambda i: (i, 0)),
                   pl.BlockSpec((tm, w_kv.shape[1]), lambda i: (i, 0)),
                   pl.BlockSpec((n_pairs, tm, LANES), lambda i: (0, i, 0))],
        out_shape=[jax.ShapeDtypeStruct((t, d), F32),
                   jax.ShapeDtypeStruct((t, w_kv.shape[1]), F32),
                   jax.ShapeDtypeStruct((n_pairs, t, LANES), BF16)],
        compiler_params=_cparams("parallel"),
        name="layer0_tail",
    )(a, w_out, r, g_ffn.reshape(1, d), w_gu, w_down, g_kv.reshape(1, d), g_q.reshape(1, d), w_kv, w_q)


def _expert_ffn_kernel(te_ref, na_ref, x_ref, wgu_ref, wd_ref, o_ref):
    active = pl.program_id(0) < na_ref[0]

    @pl.when(active)
    def _():
        o_ref[...] = _swiglu(x_ref[...].astype(BF16), wgu_ref, wd_ref)

    @pl.when(jnp.logical_not(active))
    def _():
        o_ref[...] = jnp.zeros_like(o_ref)


def expert_ffn(xs, tile_expert, n_active, w_gu, w_down, tm):
    rows, d = xs.shape
    return pl.pallas_call(
        _expert_ffn_kernel,
        grid_spec=pltpu.PrefetchScalarGridSpec(
            num_scalar_prefetch=2,
            grid=(rows // tm,),
            in_specs=[pl.BlockSpec((tm, d), lambda i, te, na: (jnp.minimum(i, na[0] - 1), 0)),
                      pl.BlockSpec((1, d, 2 * D_FF), lambda i, te, na: (te[i], 0, 0)),
                      pl.BlockSpec((1, D_FF, d), lambda i, te, na: (te[i], 0, 0))],
            out_specs=pl.BlockSpec((tm, d), lambda i, te, na: (i, 0))),
        out_shape=jax.ShapeDtypeStruct((rows, d), F32),
        compiler_params=_cparams("arbitrary"),
        name="expert_ffn",
    )(tile_expert, n_active, xs, w_gu, w_down)


def _router_kernel(a_ref, wo_ref, r_ref, g_ref, whi_ref, wlo_ref, h_ref, gate_ref, ids_ref, cnt_ref):
    x = r_ref[...] + jnp.dot(a_ref[...], wo_ref[...], preferred_element_type=F32)
    h_ref[...] = x
    xn = x * _rms_scale(x) * g_ref[...]
    x_hi = xn.astype(BF16)
    x_lo = (xn - x_hi.astype(F32)).astype(BF16)
    logits = (jnp.dot(x_hi, whi_ref[...], preferred_element_type=F32)
              + jnp.dot(x_lo, whi_ref[...], preferred_element_type=F32)
              + jnp.dot(x_hi, wlo_ref[...], preferred_element_type=F32))
    lane = lax.broadcasted_iota(I32, logits.shape, 1)
    logits = jnp.where(lane < N_EXPERTS, logits, -jnp.inf)
    m1 = jnp.max(logits, axis=-1, keepdims=True)
    i1 = jnp.min(jnp.where(logits == m1, lane, LANES), axis=-1, keepdims=True)
    rest = jnp.where(lane == i1, -jnp.inf, logits)
    m2 = jnp.max(rest, axis=-1, keepdims=True)
    i2 = jnp.min(jnp.where(rest == m2, lane, LANES), axis=-1, keepdims=True)
    e2 = jnp.exp(m2 - m1)
    w1 = 1.0 / (1.0 + e2)
    w2 = e2 / (1.0 + e2)
    gate_ref[...] = jnp.where(lane == 0, w1, jnp.where(lane == 1, w2, 0.0))
    ids_ref[...] = jnp.where(lane == 0, i1, jnp.where(lane == 1, i2, 0))
    chosen = jnp.where(jnp.logical_or(lane == i1, lane == i2), 1.0, 0.0)

    @pl.when(pl.program_id(0) == 0)
    def _():
        cnt_ref[...] = jnp.zeros_like(cnt_ref)

    n_tile = jnp.sum(chosen, axis=0, keepdims=True)
    n_pad = jnp.ceil(n_tile * (1.0 / RUN_ALIGN)) * RUN_ALIGN
    cnt_ref[...] += jnp.broadcast_to(n_pad, cnt_ref.shape)


def out_proj_router(a, w_o, r, g, w_router):
    t, k = a.shape
    d = w_o.shape[1]
    tm = _row_tile(t)
    w_hi = w_router.astype(BF16)
    w_lo = (w_router - w_hi.astype(F32)).astype(BF16)
    return pl.pallas_call(
        _router_kernel,
        grid=(t // tm,),
        in_specs=[pl.BlockSpec((tm, k), lambda i: (i, 0)),
                  _resident(w_o.shape),
                  pl.BlockSpec((tm, d), lambda i: (i, 0)),
                  pl.BlockSpec((1, d), lambda i: (0, 0)),
                  _resident((d, LANES)), _resident((d, LANES))],
        out_specs=[pl.BlockSpec((tm, d), lambda i: (i, 0)),
                   pl.BlockSpec((tm, LANES), lambda i: (i, 0)),
                   pl.BlockSpec((tm, LANES), lambda i: (i, 0)),
                   pl.BlockSpec((SUBLANES, LANES), lambda i: (0, 0))],
        out_shape=[jax.ShapeDtypeStruct((t, d), F32),
                   jax.ShapeDtypeStruct((t, LANES), F32),
                   jax.ShapeDtypeStruct((t, LANES), I32),
                   jax.ShapeDtypeStruct((SUBLANES, LANES), F32)],
        compiler_params=_cparams("arbitrary"),
        name="out_proj_router",
    )(a, w_o, r, g.reshape(1, d), w_hi, w_lo)


META_ROWS = 3


def _staging_rows(tt):
    need = TOP_K * tt + N_EXPERTS * (RUN_ALIGN - 1)
    return -(-need // LANES) * LANES


def _plan_kernel(ids_ref, base_ref, slot_ref, meta_ref, carry_ref):
    @pl.when(pl.program_id(0) == 0)
    def _():
        carry_ref[...] = jnp.zeros_like(carry_ref)

    ids = ids_ref[...]
    tt = ids.shape[0]
    lane = lax.broadcasted_iota(I32, ids.shape, 1)
    pick0 = lane == ids[:, 0:1]
    pick1 = lane == ids[:, 1:2]
    chosen = jnp.where(jnp.logical_or(pick0, pick1), 1.0, 0.0)
    ri = lax.broadcasted_iota(I32, (tt, tt), 0)
    ci = lax.broadcasted_iota(I32, (tt, tt), 1)
    before = jnp.where(ri > ci, 1.0, 0.0).astype(BF16)
    rank = jnp.dot(before, chosen.astype(BF16), preferred_element_type=F32)
    n_pad = jnp.ceil(jnp.sum(chosen, axis=0, keepdims=True) * (1.0 / RUN_ALIGN)) * RUN_ALIGN
    ei = lax.broadcasted_iota(I32, (LANES, LANES), 0)
    ej = lax.broadcasted_iota(I32, (LANES, LANES), 1)
    earlier = jnp.where(ei < ej, 1.0, 0.0).astype(BF16)
    n_pad8 = jnp.broadcast_to(n_pad, (SUBLANES, LANES))
    local = jnp.dot(n_pad8.astype(BF16), earlier, preferred_element_type=F32)[0:1, :]
    slot = local + rank
    s0 = jnp.sum(jnp.where(pick0, slot, 0.0), axis=-1, keepdims=True)
    s1 = jnp.sum(jnp.where(pick1, slot, 0.0), axis=-1, keepdims=True)
    slot_ref[...] = jnp.where(lane == 0, s0, jnp.where(lane == 1, s1, 0.0))
    row = lax.broadcasted_iota(I32, (SUBLANES, LANES), 0)
    sorted_start = base_ref[0:1, :] + carry_ref[0:1, :]
    meta = jnp.where(row == 0, sorted_start, jnp.where(row == 1, local, jnp.where(row == 2, n_pad, 0.0)))
    meta_ref[...] = meta.astype(I32)
    carry_ref[...] += n_pad8


def dispatch_plan(ids, base):
    t = ids.shape[0]
    tt = _row_tile(t)
    return pl.pallas_call(
        _plan_kernel,
        grid=(t // tt,),
        in_specs=[pl.BlockSpec((tt, LANES), lambda i: (i, 0)),
                  pl.BlockSpec((SUBLANES, LANES), lambda i: (0, 0))],
        out_specs=[pl.BlockSpec((tt, LANES), lambda i: (i, 0)),
                   pl.BlockSpec((SUBLANES, LANES), lambda i: (i, 0))],
        out_shape=[jax.ShapeDtypeStruct((t, LANES), F32),
                   jax.ShapeDtypeStruct((SUBLANES * (t // tt), LANES), I32)],
        scratch_shapes=[pltpu.VMEM((SUBLANES, LANES), F32)],
        compiler_params=_cparams("arbitrary"),
        name="dispatch_plan",
    )(ids, base)


def _run_copies(meta_ref, tile, staging, sorted_hbm, sem, to_sorted):
    first = tile * (META_ROWS * N_EXPERTS)
    out = []
    for e in range(N_EXPERTS):
        sorted_start = pl.multiple_of(meta_ref[first + e], RUN_ALIGN)
        local_start = pl.multiple_of(meta_ref[first + N_EXPERTS + e], RUN_ALIGN)
        n = pl.multiple_of(meta_ref[first + 2 * N_EXPERTS + e], RUN_ALIGN)
        a, b = staging.at[pl.ds(local_start, n)], sorted_hbm.at[pl.ds(sorted_start, n)]
        out.append((n, pltpu.make_async_copy(a, b, sem) if to_sorted else pltpu.make_async_copy(b, a, sem)))
    return out


def _start(copies):
    for n, copy in copies:
        pl.when(n > 0)(copy.start)


def _wait(copies):
    for n, copy in copies:
        pl.when(n > 0)(copy.wait)


def _start_then_wait(copies):
    _start(copies)
    _wait(copies)


def _scatter_kernel(meta_ref, tail_ref, x_ref, g_ref, slot_ref, xs_hbm, stage_scr, zero_scr, sem):
    i = pl.program_id(0)
    last = pl.num_programs(0) - 1
    buf = i % 2
    x = x_ref[...]
    xn = (x * _rms_scale(x) * g_ref[...]).astype(BF16)
    tt, rt = x.shape[0], stage_scr.shape[1]
    slot_t = slot_ref[...].T.astype(I32)
    r = lax.broadcasted_iota(I32, (rt, tt), 0)
    place = jnp.where(jnp.logical_or(r == slot_t[0:1, :], r == slot_t[1:2, :]), 1.0, 0.0).astype(BF16)
    stage_scr[buf] = jnp.dot(place, xn, preferred_element_type=F32)
    mine = _run_copies(meta_ref, i, stage_scr.at[buf], xs_hbm, sem.at[buf], to_sorted=True)
    _start(mine)

    @pl.when(i > 0)
    def _():
        _wait(_run_copies(meta_ref, i - 1, stage_scr.at[1 - buf], xs_hbm, sem.at[1 - buf], to_sorted=True))

    @pl.when(i == last)
    def _():
        _wait(mine)
        zero_scr[...] = jnp.zeros_like(zero_scr)
        tails = []
        for e in range(N_EXPERTS):
            start = pl.multiple_of(tail_ref[e], RUN_ALIGN)
            n = pl.multiple_of(tail_ref[N_EXPERTS + e], RUN_ALIGN)
            tails.append((n, pltpu.make_async_copy(zero_scr.at[pl.ds(0, n)], xs_hbm.at[pl.ds(start, n)], sem.at[0])))
        _start_then_wait(tails)

        tm = zero_scr.shape[0]

        def clear_tile(j, carry):
            copy = pltpu.make_async_copy(zero_scr, xs_hbm.at[pl.ds(pl.multiple_of(j * tm, tm), tm)], sem.at[0])
            copy.start()
            copy.wait()
            return carry

        lax.fori_loop(tail_ref[2 * N_EXPERTS], xs_hbm.shape[0] // tm, clear_tile, 0)


def scatter_rows(x, g, slot, meta, tails, n_rows, tm):
    t, d = x.shape
    tt = _row_tile(t)
    return pl.pallas_call(
        _scatter_kernel,
        grid_spec=pltpu.PrefetchScalarGridSpec(
            num_scalar_prefetch=2,
            grid=(t // tt,),
            in_specs=[pl.BlockSpec((tt, d), lambda i, m, tl: (i, 0)),
                      pl.BlockSpec((1, d), lambda i, m, tl: (0, 0)),
                      pl.BlockSpec((tt, LANES), lambda i, m, tl: (i, 0))],
            out_specs=pl.BlockSpec(memory_space=pl.ANY),
            scratch_shapes=[pltpu.VMEM((2, _staging_rows(tt), d), F32), pltpu.VMEM((tm, d), F32),
                            pltpu.SemaphoreType.DMA((2,))]),
        out_shape=jax.ShapeDtypeStruct((n_rows, d), F32),
        compiler_params=_cparams("arbitrary"),
        name="scatter_rows",
    )(meta, tails, x, g.reshape(1, d), slot)


def _combine_kernel(meta_ref, h_ref, gate_ref, slot_ref, fg_ref, ys_hbm, o_ref, stage_scr, sem):
    i = pl.program_id(0)
    buf = i % 2

    def fetch(tile, b):
        return _run_copies(meta_ref, tile, stage_scr.at[b], ys_hbm, sem.at[b], to_sorted=False)

    @pl.when(i == 0)
    def _():
        stage_scr[...] = jnp.zeros_like(stage_scr)
        _start(fetch(0, 0))

    @pl.when(i + 1 < pl.num_programs(0))
    def _():
        _start(fetch(i + 1, 1 - buf))

    tt, rt = h_ref.shape[0], stage_scr.shape[1]
    slot = slot_ref[...].astype(I32)
    gate = gate_ref[...]
    c = lax.broadcasted_iota(I32, (tt, rt), 1)
    weight = (jnp.where(c == slot[:, 0:1], gate[:, 0:1], 0.0)
              + jnp.where(c == slot[:, 1:2], gate[:, 1:2], 0.0))
    w_hi = weight.astype(BF16)
    w_lo = (weight - w_hi.astype(F32)).astype(BF16)
    _wait(fetch(i, buf))
    yb = stage_scr[buf].astype(BF16)
    moe = jnp.dot(w_hi, yb, preferred_element_type=F32) + jnp.dot(w_lo, yb, preferred_element_type=F32)
    y = h_ref[...] + moe
    o_ref[...] = y * _rms_scale(y) * fg_ref[...]


def combine_rows(h, gates, slot, meta, final_gain, ys):
    t, d = h.shape
    tt = _row_tile(t)
    return pl.pallas_call(
        _combine_kernel,
        grid_spec=pltpu.PrefetchScalarGridSpec(
            num_scalar_prefetch=1,
            grid=(t // tt,),
            in_specs=[pl.BlockSpec((tt, d), lambda i, m: (i, 0)),
                      pl.BlockSpec((tt, LANES), lambda i, m: (i, 0)),
                      pl.BlockSpec((tt, LANES), lambda i, m: (i, 0)),
                      pl.BlockSpec((1, d), lambda i, m: (0, 0)),
                      pl.BlockSpec(memory_space=pl.ANY)],
            out_specs=pl.BlockSpec((tt, d), lambda i, m: (i, 0)),
            scratch_shapes=[pltpu.VMEM((2, _staging_rows(tt), d), F32), pltpu.SemaphoreType.DMA((2,))]),
        out_shape=jax.ShapeDtypeStruct((t, d), F32),
        compiler_params=_cparams("arbitrary"),
        name="combine_rows",
    )(meta, h, gates, slot, final_gain.reshape(1, d), ys)


def attn_out_moe_final_norm(o, w_o, r, g, w_router, w_gu, w_down, final_gain):
    t, d = r.shape
    tt = _row_tile(t)
    n_tok_tiles = t // tt
    tm = 512 if t >= 4096 else 128
    worst = TOP_K * t + n_tok_tiles * N_EXPERTS * (RUN_ALIGN - 1) + N_EXPERTS * tm
    n_rows = -(-worst // tm) * tm
    h, gates, ids, counts = out_proj_router(o, w_o, r, g, w_router)

    cnt = counts[0, :N_EXPERTS].astype(I32)
    tiles = (cnt + tm - 1) // tm
    ends = jnp.cumsum(tiles)
    starts = (ends - tiles) * tm
    base = jnp.zeros((SUBLANES, LANES), F32).at[:, :N_EXPERTS].set(starts.astype(F32))
    n_active = ends[-1:]
    tile_idx = jnp.arange(n_rows // tm, dtype=I32)
    owner = jnp.sum(tile_idx[:, None] >= ends[None, :], axis=1).astype(I32)
    last_owner = jnp.sum(n_active - 1 >= ends).astype(I32)
    tile_expert = jnp.where(tile_idx < n_active, owner, last_owner)
    tails = jnp.concatenate([starts + cnt, tiles * tm - cnt, n_active]).astype(I32)

    slot, meta = dispatch_plan(ids, base)
    meta_flat = meta.reshape(n_tok_tiles, SUBLANES, LANES)[:, :META_ROWS, :N_EXPERTS].reshape(-1)
    xs = scatter_rows(h, g, slot, meta_flat, tails, n_rows, tm)
    ys = expert_ffn(xs, tile_expert, n_active, w_gu, w_down, tm)
    return combine_rows(h, gates, slot, meta_flat, final_gain, ys)


def _split3(v):
    hi = v.astype(BF16)
    r1 = v - hi.astype(F32)
    mid = r1.astype(BF16)
    lo = (r1 - mid.astype(F32)).astype(BF16)
    return hi, mid, lo


def _ssd_block(rows, blk, xbc_ref, dt_ref, z_ref, cw_ref, cb_ref, dtb_ref, alog_ref, dskip_ref, gn_ref,
               expand_ref, yn_ref, h_scr, xext_scr, seq_valid):
    q = SSD_BLOCK
    halo = SUBLANES

    xext_scr[halo:halo + q, :] = xbc_ref[0, rows, :]
    xext = xext_scr[...]
    conv = cb_ref[...] + xext[halo:] * cw_ref[CONV_WIDTH - 1:CONV_WIDTH, :]
    for k in range(CONV_WIDTH - 1):
        delayed = pltpu.roll(xext, CONV_WIDTH - 1 - k, axis=0)
        conv = conv + delayed[halo:] * cw_ref[k:k + 1, :]
    xext_scr[0:halo, :] = xext_scr[q:q + halo, :]
    xbc = _silu(conv)
    xs = xbc[:, :D_INNER]
    b_all = xbc[:, D_INNER:D_INNER + SSM_GROUPS * D_STATE]
    c_all = xbc[:, D_INNER + SSM_GROUPS * D_STATE:]

    v = dt_ref[0, rows, :] + dtb_ref[...]
    dt = jnp.maximum(v, 0.0) + jnp.log1p(jnp.exp(-jnp.abs(v)))
    row = lax.broadcasted_iota(I32, (q, LANES), 0)
    if seq_valid is not None:
        dt = jnp.where(row + blk * q < seq_valid, dt, 0.0)
    a_neg = -jnp.exp(alog_ref[...])
    dta = dt * a_neg

    ti = lax.broadcasted_iota(I32, (q, q), 0)
    si = lax.broadcasted_iota(I32, (q, q), 1)
    causal = ti >= si
    tri = jnp.where(causal, 1.0, 0.0).astype(BF16)
    a_cs = sum(jnp.dot(tri, part, preferred_element_type=F32) for part in _split3(dta))
    a_last = a_cs[q - 1:q, :]
    a_cs_t = a_cs.T

    stacked = jnp.concatenate([dt, jnp.exp(a_cs), jnp.exp(a_last - a_cs),
                               jnp.broadcast_to(jnp.exp(a_last), (SUBLANES, LANES))], axis=0)
    s_hi = stacked.astype(BF16)
    s_lo = (stacked - s_hi.astype(F32)).astype(BF16)
    expand = expand_ref[...]
    expanded = (jnp.dot(s_hi, expand, preferred_element_type=F32)
                + jnp.dot(s_lo, expand, preferred_element_type=F32))
    dt_e = expanded[0:q]
    decay_in_e = expanded[q:2 * q]
    decay_out_e = expanded[2 * q:3 * q]
    decay_blk_e = expanded[3 * q:3 * q + 1]

    xdt = xs * dt_e
    xdt_bf = xdt.astype(BF16)
    wx = (xdt * decay_out_e).astype(BF16)
    low_half = lax.broadcasted_iota(I32, (q, LANES), 1) < SSM_HEAD_DIM

    y_groups = []
    for g in range(SSM_GROUPS):
        bg = b_all[:, g * D_STATE:(g + 1) * D_STATE]
        cg = c_all[:, g * D_STATE:(g + 1) * D_STATE].astype(BF16)
        bg_bf = bg.astype(BF16)
        cbm = lax.dot_general(cg, bg_bf, (((1,), (1,)), ((), ())), preferred_element_type=F32)
        cols = slice(g * GROUP_COLS, (g + 1) * GROUP_COLS)
        h_g = h_scr[g]
        y_g = jnp.dot(cg, h_g.astype(BF16), preferred_element_type=F32) * decay_in_e[:, cols]
        pair_out = []
        for pr in range(HEADS_PER_GROUP // 2):
            parts = []
            for par in range(2):
                h = g * HEADS_PER_GROUP + 2 * pr + par
                seg = (jnp.broadcast_to(a_cs[:, h:h + 1], (q, q))
                       - jnp.broadcast_to(a_cs_t[h:h + 1, :], (q, q)))
                m = (cbm * jnp.exp(jnp.where(causal, seg, -jnp.inf))).astype(BF16)
                pcols = slice((h // 2) * LANES, (h // 2 + 1) * LANES)
                parts.append(jnp.dot(m, xdt_bf[:, pcols], preferred_element_type=F32))
            pair_out.append(jnp.where(low_half, parts[0], parts[1]))
        y_groups.append(y_g + jnp.concatenate(pair_out, axis=1))
        h_scr[g] = (h_g * decay_blk_e[:, cols]
                    + jnp.dot(bg.T.astype(BF16), wx[:, cols], preferred_element_type=F32))

    y = jnp.concatenate(y_groups, axis=1) + dskip_ref[...] * xs
    yz = y * _silu(z_ref[0, rows, :])
    yn_ref[0, rows, :] = (yz * _rms_scale(yz) * gn_ref[...]).astype(yn_ref.dtype)


def _ssd_kernel(xbc_ref, dt_ref, z_ref, cbuf_ref, h0_ref, cw_ref, cb_ref, dtb_ref, alog_ref,
                dskip_ref, gn_ref, expand_ref, yn_ref, hfin_ref, h_scr, xext_scr, *, seq_valid, n_sub):
    c = pl.program_id(1)

    @pl.when(c == 0)
    def _():
        h_scr[...] = h0_ref[0]
        xext_scr[0:SUBLANES, :] = cbuf_ref[0]

    for sub in range(n_sub):
        _ssd_block(slice(sub * SSD_BLOCK, (sub + 1) * SSD_BLOCK), c * n_sub + sub, xbc_ref, dt_ref, z_ref,
                   cw_ref, cb_ref, dtb_ref, alog_ref, dskip_ref, gn_ref, expand_ref, yn_ref, h_scr, xext_scr,
                   seq_valid)

    @pl.when(c == pl.num_programs(1) - 1)
    def _():
        hfin_ref[0] = h_scr[...]


def _state_to_kernel_layout(h):
    b = h.shape[0]
    h = h.reshape(b, SSM_GROUPS, HEADS_PER_GROUP, SSM_HEAD_DIM, D_STATE)
    return h.transpose(0, 1, 4, 2, 3).reshape(b, SSM_GROUPS, D_STATE, GROUP_COLS)


def _state_from_kernel_layout(h):
    b = h.shape[0]
    h = h.reshape(b, SSM_GROUPS, D_STATE, HEADS_PER_GROUP, SSM_HEAD_DIM)
    return h.transpose(0, 1, 3, 4, 2).reshape(b, N_SSM_HEADS, SSM_HEAD_DIM, D_STATE)


def _pad_lanes(v):
    return jnp.pad(v.astype(F32), (0, LANES - v.shape[0])).reshape(1, LANES)


def ssd_mixer(xbc, dt_raw, z, conv_buf, h0, conv_w, conv_b, dt_bias, a_log, d_skip, gnorm):
    bsz, seq, _ = xbc.shape
    q = SSD_BLOCK
    seq_pad = -(-seq // q) * q
    seq_valid = None
    if seq_pad != seq:
        pad = ((0, 0), (0, seq_pad - seq), (0, 0))
        xbc, dt_raw, z = jnp.pad(xbc, pad), jnp.pad(dt_raw, pad), jnp.pad(z, pad)
        seq_valid = seq
    n_sub = 2 if (seq_pad // q) % 2 == 0 else 1
    rows_per_step = n_sub * q
    cbuf = jnp.pad(conv_buf, ((0, 0), (SUBLANES - (CONV_WIDTH - 1), 0), (0, 0)))
    expand = (jnp.arange(D_INNER)[None, :] // SSM_HEAD_DIM == jnp.arange(LANES)[:, None]).astype(BF16)
    d_e = jnp.repeat(d_skip.astype(F32), SSM_HEAD_DIM).reshape(1, D_INNER)
    state_spec = pl.BlockSpec((1, SSM_GROUPS, D_STATE, GROUP_COLS), lambda b, c: (b, 0, 0, 0))

    def const(shape):
        return pl.BlockSpec(shape, lambda b, c: (0,) * len(shape))

    yn, h_fin = pl.pallas_call(
        functools.partial(_ssd_kernel, seq_valid=seq_valid, n_sub=n_sub),
        grid=(bsz, seq_pad // rows_per_step),
        in_specs=[pl.BlockSpec((1, rows_per_step, CONV_DIM), lambda b, c: (b, c, 0)),
                  pl.BlockSpec((1, rows_per_step, LANES), lambda b, c: (b, c, 0)),
                  pl.BlockSpec((1, rows_per_step, D_INNER), lambda b, c: (b, c, 0)),
                  pl.BlockSpec((1, SUBLANES, CONV_DIM), lambda b, c: (b, 0, 0)),
                  state_spec,
                  const((CONV_WIDTH, CONV_DIM)), const((1, CONV_DIM)), const((1, LANES)), const((1, LANES)),
                  const((1, D_INNER)), const((1, D_INNER)), const((LANES, D_INNER))],
        out_specs=[pl.BlockSpec((1, rows_per_step, D_INNER), lambda b, c: (b, c, 0)), state_spec],
        out_shape=[jax.ShapeDtypeStruct((bsz, seq_pad, D_INNER), BF16),
                   jax.ShapeDtypeStruct((bsz, SSM_GROUPS, D_STATE, GROUP_COLS), F32)],
        scratch_shapes=[pltpu.VMEM((SSM_GROUPS, D_STATE, GROUP_COLS), F32),
                        pltpu.VMEM((q + SUBLANES, CONV_DIM), F32)],
        compiler_params=_cparams("parallel", "arbitrary"),
        name="ssd_mixer",
    )(xbc, dt_raw, z, cbuf, _state_to_kernel_layout(h0.astype(F32)), conv_w.astype(F32),
      conv_b.astype(F32).reshape(1, CONV_DIM), _pad_lanes(dt_bias), _pad_lanes(a_log), d_e,
      gnorm.astype(F32).reshape(1, D_INNER), expand)
    return yn[:, :seq], _state_from_kernel_layout(h_fin)


_ALIBI_SLOPES = (2.0 ** (-8.0 * np.arange(1, N_Q_HEADS + 1, dtype=np.float32) / N_Q_HEADS)).astype(np.float32)


def _attn_bias(lbq, chunk, past_valid):
    w = WINDOW + lbq
    qi = np.arange(lbq)[:, None]
    si = np.arange(w)[None, :]
    dist = np.abs(WINDOW + qi - si).astype(np.float32)
    lo = (qi // chunk) * chunk
    visible = (si >= lo) & (si < lo + WINDOW + chunk)
    variants = []
    for first_block in (True, False):
        valid = visible & (si >= WINDOW) if (first_block and not past_valid) else visible
        per_head = [np.where(valid, -(_ALIBI_SLOPES[h] * dist), np.float32(NEG_INF)) for h in range(N_Q_HEADS)]
        variants.append(np.stack([np.concatenate([per_head[4 * kv + par], per_head[4 * kv + 2 + par]], axis=0)
                                  for kv in range(N_KV_HEADS) for par in range(2)]))
    return np.stack(variants).astype(np.float32)


def _attn_kernel(sink_ref, q_ref, k_ref, v_ref, bias_ref, o_ref, *, lbq, n_sub):
    for sub in range(n_sub):
        _attn_block(slice(sub * lbq, (sub + 1) * lbq), pl.program_id(1) * n_sub + sub,
                    sink_ref, q_ref, k_ref, v_ref, bias_ref, o_ref, lbq)


def _attn_block(rows, n, sink_ref, q_ref, k_ref, v_ref, bias_ref, o_ref, lbq):
    w = WINDOW + lbq
    base = pl.multiple_of(n * lbq, lbq)
    variant = jnp.minimum(n, 1)
    kband = k_ref[0, pl.ds(base, w), :].astype(F32)
    vband = v_ref[0, pl.ds(base, w), :].astype(F32)
    first_pair = lax.broadcasted_iota(I32, (2 * lbq, 1), 0) < lbq
    lane = lax.broadcasted_iota(I32, (w, LANES), 1)
    low_half = lane < ATTN_HEAD_DIM

    for kv in range(N_KV_HEADS):
        pcols = slice((kv // 2) * LANES, (kv // 2 + 1) * LANES)
        kpair, vpair = kband[:, pcols], vband[:, pcols]
        kswap = pltpu.roll(kpair, ATTN_HEAD_DIM, axis=1)
        vswap = pltpu.roll(vpair, ATTN_HEAD_DIM, axis=1)
        in_low = kv % 2 == 0
        k_lo = jnp.where(low_half, kpair if in_low else kswap, 0.0).astype(BF16)
        k_hi = jnp.where(low_half, 0.0, kswap if in_low else kpair).astype(BF16)
        v_lo = jnp.where(low_half, vpair if in_low else vswap, 0.0).astype(BF16)
        v_hi = jnp.where(low_half, 0.0, vswap if in_low else vpair).astype(BF16)
        qs = q_ref[2 * kv:2 * kv + 2, 0, rows, :].reshape(2 * lbq, LANES)
        out = None
        for par, (kx, vx) in enumerate(((k_lo, v_lo), (k_hi, v_hi))):
            h0, h1 = 4 * kv + par, 4 * kv + 2 + par
            sink = jnp.where(first_pair, sink_ref[h0], sink_ref[h1])
            s = lax.dot_general(qs, kx, (((1,), (1,)), ((), ())), preferred_element_type=F32)
            s = s + bias_ref[variant, 2 * kv + par]
            m = jnp.maximum(jnp.max(s, axis=-1, keepdims=True), sink)
            p = jnp.exp(s - m)
            denom = jnp.sum(p, axis=-1, keepdims=True) + jnp.exp(sink - m)
            pn = (p / denom).astype(BF16)
            part = jnp.dot(pn, vx, preferred_element_type=F32)
            out = part if out is None else out + part
        o_ref[0, rows, (2 * kv) * LANES:(2 * kv + 1) * LANES] = out[:lbq].astype(o_ref.dtype)
        o_ref[0, rows, (2 * kv + 1) * LANES:(2 * kv + 2) * LANES] = out[lbq:].astype(o_ref.dtype)


def window_attention(q_pairs, k_all, v_all, sinks, *, lbq, chunk, past_valid):
    n_pairs, bsz, seq, _ = q_pairs.shape
    rows = k_all.shape[1]
    bias = jnp.asarray(_attn_bias(lbq, chunk, past_valid))
    n_sub = 2 if (seq // lbq) % 2 == 0 else 1
    step = n_sub * lbq
    return pl.pallas_call(
        functools.partial(_attn_kernel, lbq=lbq, n_sub=n_sub),
        grid_spec=pltpu.PrefetchScalarGridSpec(
            num_scalar_prefetch=1,
            grid=(bsz, seq // step),
            in_specs=[pl.BlockSpec((n_pairs, 1, step, LANES), lambda b, n, s: (0, b, n, 0)),
                      pl.BlockSpec((1, rows, KV_DIM), lambda b, n, s: (b, 0, 0)),
                      pl.BlockSpec((1, rows, KV_DIM), lambda b, n, s: (b, 0, 0)),
                      _resident(bias.shape)],
            out_specs=pl.BlockSpec((1, step, D_MODEL), lambda b, n, s: (b, n, 0))),
        out_shape=jax.ShapeDtypeStruct((bsz, seq, D_MODEL), BF16),
        compiler_params=_cparams("parallel", "arbitrary"),
        name="window_attention",
    )(sinks.astype(F32), q_pairs, k_all, v_all, bias)


def _trunk(x, conv_buf, ssm_state, k_past, v_past, past_valid, p):
    bsz, seq, d = x.shape
    t = bsz * seq
    x2 = x.reshape(t, d)

    z, xbc, dt_raw = in_proj(x2, p['norm_mix'][0], p['w_z'], p['w_xbc'], p['w_dt'])
    xbc3 = xbc.reshape(bsz, seq, CONV_DIM)
    new_conv = jnp.concatenate([conv_buf.astype(F32), xbc3], axis=1)[:, seq:]
    yn, new_ssm = ssd_mixer(xbc3, dt_raw.reshape(bsz, seq, LANES), z.reshape(bsz, seq, D_INNER),
                            conv_buf.astype(F32), ssm_state, p['ssm_conv_w'], p['ssm_conv_b'],
                            p['ssm_dt_bias'], p['ssm_a_log'], p['ssm_d'], p['ssm_norm'])

    h, kv, q_pairs = layer0_tail(yn.reshape(t, D_INNER), p['ssm_out_w'], x2, p['norm_ffn'][0],
                                 p['ffn_w_gate_up'], p['ffn_w_down'], p['kv_norm'], p['norm_mix'][1],
                                 p['w_kv'], p['w_q'])
    kv = kv.reshape(bsz, seq, 2 * KV_DIM)
    k_new, v_new = kv[..., :KV_DIM], kv[..., KV_DIM:]
    k_all = jnp.concatenate([k_past.reshape(bsz, WINDOW, KV_DIM), k_new], axis=1).astype(BF16)
    v_all = jnp.concatenate([v_past.reshape(bsz, WINDOW, KV_DIM), v_new], axis=1).astype(BF16)

    chunk = min(ATTN_CHUNK, seq)
    lbq = 2 * chunk if seq % (2 * chunk) == 0 else chunk
    q_pairs = q_pairs.reshape(D_MODEL // LANES, bsz, seq, LANES)
    o = window_attention(q_pairs, k_all, v_all, p['attn_sinks'], lbq=lbq, chunk=chunk, past_valid=past_valid)

    y = attn_out_moe_final_norm(o.reshape(t, D_MODEL), p['w_o'], h, p['norm_ffn'][1], p['moe_router'],
                                p['moe_w_gate_up'], p['moe_w_down'], p['final_norm'])

    k4 = k_new.reshape(bsz, seq, N_KV_HEADS, ATTN_HEAD_DIM)
    v4 = v_new.reshape(bsz, seq, N_KV_HEADS, ATTN_HEAD_DIM)
    return y.reshape(bsz, seq, d), new_conv[None], new_ssm[None], k4, v4


def kernel(x_prompt, x_sample, state_conv, state_ssm, cache_k, cache_v, norm_mix, norm_ffn, ssm_in_w, ssm_conv_w,
           ssm_conv_b, ssm_dt_bias, ssm_a_log, ssm_d, ssm_norm, ssm_out_w, kv_norm, w_kv, w_q, attn_sinks, w_o,
           ffn_w_gate_up, ffn_w_down, moe_router, moe_w_gate_up, moe_w_down, final_norm):
    in_w = ssm_in_w[0]
    n_dt = N_SSM_HEADS
    p = {
        'norm_mix': norm_mix, 'norm_ffn': norm_ffn,
        'w_z': in_w[:, :D_INNER].astype(BF16),
        'w_xbc': in_w[:, D_INNER:D_INNER + CONV_DIM].astype(BF16),
        'w_dt': jnp.pad(in_w[:, D_INNER + CONV_DIM:], ((0, 0), (0, LANES - n_dt))).astype(BF16),
        'ssm_conv_w': ssm_conv_w[0], 'ssm_conv_b': ssm_conv_b[0], 'ssm_dt_bias': ssm_dt_bias[0],
        'ssm_a_log': ssm_a_log[0], 'ssm_d': ssm_d[0], 'ssm_norm': ssm_norm[0],
        'ssm_out_w': ssm_out_w[0].astype(BF16),
        'kv_norm': kv_norm, 'w_kv': w_kv.astype(BF16), 'w_q': w_q[0].astype(BF16),
        'attn_sinks': attn_sinks[0], 'w_o': w_o[0].astype(BF16),
        'ffn_w_gate_up': ffn_w_gate_up.astype(BF16), 'ffn_w_down': ffn_w_down.astype(BF16),
        'moe_router': jnp.pad(moe_router[0].astype(F32), ((0, 0), (0, LANES - N_EXPERTS))),
        'moe_w_gate_up': moe_w_gate_up[0].astype(BF16), 'moe_w_down': moe_w_down[0].astype(BF16),
        'final_norm': final_norm,
    }
    bsz, seq_p = x_prompt.shape[:2]
    dt = x_prompt.dtype
    zero_conv = jnp.zeros((bsz, CONV_WIDTH - 1, CONV_DIM), dt)
    zero_ssm = jnp.zeros((bsz, N_SSM_HEADS, SSM_HEAD_DIM, D_STATE), dt)
    zero_kv = jnp.zeros((bsz, WINDOW, N_KV_HEADS, ATTN_HEAD_DIM), dt)
    y_p, conv_p, ssm_p, k_p, v_p = _trunk(x_prompt, zero_conv, zero_ssm, zero_kv, zero_kv, False, p)
    keep = min(WINDOW, seq_p)
    y_s, conv_s, ssm_s, k_s, v_s = _trunk(x_sample, state_conv[0], state_ssm[0], cache_k, cache_v, True, p)
    return (y_p, y_s, conv_p, ssm_p, k_p[:, seq_p - keep:], v_p[:, seq_p - keep:], conv_s, ssm_s, k_s, v_s)
```

```python
import functools
import math

import jax
import jax.numpy as jnp
import numpy as np
from jax import lax
from jax.experimental import pallas as pl
from jax.experimental.pallas import tpu as pltpu

F32 = jnp.float32
BF16 = jnp.bfloat16
I32 = jnp.int32

D_MODEL = 1024
EPS = 1e-6
D_INNER = 2048
SSM_HEAD_DIM = 64
N_SSM_HEADS = 32
SSM_GROUPS = 4
HEADS_PER_GROUP = N_SSM_HEADS // SSM_GROUPS
D_STATE = 128
CONV_WIDTH = 4
CONV_DIM = D_INNER + 2 * SSM_GROUPS * D_STATE
ATTN_HEAD_DIM = 64
N_Q_HEADS = 16
N_KV_HEADS = 4
KV_DIM = N_KV_HEADS * ATTN_HEAD_DIM
WINDOW = 128
ATTN_CHUNK = 64
ATTN_SCALE = 1.0 / math.sqrt(ATTN_HEAD_DIM)
NEG_INF = -1e30
D_FF = 2816
N_EXPERTS = 8
TOP_K = 2

LANES = 128
SUBLANES = 8
VMEM_LIMIT_BYTES = 56 * 1024 * 1024
RUN_ALIGN = SUBLANES

SSD_BLOCK = 128
GROUP_COLS = HEADS_PER_GROUP * SSM_HEAD_DIM
FF_CHUNK = 512


def _cparams(*sem):
    return pltpu.CompilerParams(dimension_semantics=sem, vmem_limit_bytes=VMEM_LIMIT_BYTES)


def _row_tile(t, pref=512):
    return pref if t % pref == 0 else t


def _resident(shape):
    return pl.BlockSpec(shape, lambda *_: (0,) * len(shape), pipeline_mode=pl.Buffered(1))


def _rms_scale(x):
    return lax.rsqrt(jnp.mean(x * x, axis=-1, keepdims=True) + EPS)


def _silu(x):
    return x / (1.0 + jnp.exp(-x))


def _in_proj_kernel(x_ref, g_ref, wz_ref, wx_ref, wd_ref, z_ref, xbc_ref, dt_ref):
    x = x_ref[...]
    xn = (x * _rms_scale(x) * g_ref[...]).astype(BF16)
    z_ref[...] = jnp.dot(xn, wz_ref[...], preferred_element_type=F32)
    xbc_ref[...] = jnp.dot(xn, wx_ref[...], preferred_element_type=F32)
    dt_ref[...] = jnp.dot(xn, wd_ref[...], preferred_element_type=F32)


def in_proj(x, g, w_z, w_xbc, w_dt):
    t, k = x.shape
    tm = _row_tile(t)
    ws = (w_z, w_xbc, w_dt)
    return pl.pallas_call(
        _in_proj_kernel,
        grid=(t // tm,),
        in_specs=[pl.BlockSpec((tm, k), lambda i: (i, 0)), pl.BlockSpec((1, k), lambda i: (0, 0))]
        + [_resident(w.shape) for w in ws],
        out_specs=[pl.BlockSpec((tm, w.shape[1]), lambda i: (i, 0)) for w in ws],
        out_shape=[jax.ShapeDtypeStruct((t, w.shape[1]), F32) for w in ws],
        compiler_params=_cparams("parallel"),
        name="in_proj",
    )(x, g.reshape(1, k), *ws)


def _swiglu(xb, wgu_ref, wd_ref):
    acc = None
    for c0 in range(0, D_FF, FF_CHUNK):
        cw = min(FF_CHUNK, D_FF - c0)
        gg = jnp.dot(xb, wgu_ref[0, :, c0:c0 + cw], preferred_element_type=F32)
        uu = jnp.dot(xb, wgu_ref[0, :, D_FF + c0:D_FF + c0 + cw], preferred_element_type=F32)
        act = (_silu(gg) * uu).astype(BF16)
        part = jnp.dot(act, wd_ref[0, c0:c0 + cw, :], preferred_element_type=F32)
        acc = part if acc is None else acc + part
    return acc


def _layer0_tail_kernel(a_ref, wo_ref, r_ref, gf_ref, wgu_ref, wd_ref, gkv_ref, gq_ref, wkv_ref, wq_ref,
                        h_ref, kv_ref, q_ref):
    h1 = r_ref[...] + jnp.dot(a_ref[...], wo_ref[...], preferred_element_type=F32)
    xb = (h1 * _rms_scale(h1) * gf_ref[...]).astype(BF16)
    h2 = h1 + _swiglu(xb, wgu_ref, wd_ref)
    h_ref[...] = h2
    xs = h2 * _rms_scale(h2)
    kv_ref[...] = jnp.dot((xs * gkv_ref[...]).astype(BF16), wkv_ref[...], preferred_element_type=F32)
    q = jnp.dot((xs * gq_ref[...]).astype(BF16), wq_ref[...], preferred_element_type=F32)
    q = (q * ATTN_SCALE).astype(q_ref.dtype)
    for p in range(q_ref.shape[0]):
        q_ref[p] = q[:, p * LANES:(p + 1) * LANES]


def layer0_tail(a, w_out, r, g_ffn, w_gu, w_down, g_kv, g_q, w_kv, w_q):
    t, k = a.shape
    d = w_out.shape[1]
    tm = _row_tile(t)
    n_pairs = w_q.shape[1] // LANES

    def gain():
        return pl.BlockSpec((1, d), lambda i: (0, 0))

    return pl.pallas_call(
        _layer0_tail_kernel,
        grid=(t // tm,),
        in_specs=[pl.BlockSpec((tm, k), lambda i: (i, 0)), _resident(w_out.shape),
                  pl.BlockSpec((tm, d), lambda i: (i, 0)), gain(),
                  _resident(w_gu.shape), _resident(w_down.shape), gain(), gain(),
                  _resident(w_kv.shape), _resident(w_q.shape)],
        out_specs=[pl.BlockSpec((tm, d), lambda i: (i, 0)),
                   pl.BlockSpec((tm, w_kv.shape[1]), lambda i: (i, 0)),
                   pl.BlockSpec((n_pairs, tm, LANES), lambda i: (0, i, 0))],
        out_shape=[jax.ShapeDtypeStruct((t, d), F32),
                   jax.ShapeDtypeStruct((t, w_kv.shape[1]), F32),
                   jax.ShapeDtypeStruct((n_pairs, t, LANES), BF16)],
        compiler_params=_cparams("parallel"),
        name="layer0_tail",
    )(a, w_out, r, g_ffn.reshape(1, d), w_gu, w_down, g_kv.reshape(1, d), g_q.reshape(1, d), w_kv, w_q)


def _expert_ffn_kernel(te_ref, na_ref, x_ref, wgu_ref, wd_ref, o_ref):
    active = pl.program_id(0) < na_ref[0]

    @pl.when(active)
    def _():
        o_ref[...] = _swiglu(x_ref[...].astype(BF16), wgu_ref, wd_ref)

    @pl.when(jnp.logical_not(active))
    def _():
        o_ref[...] = jnp.zeros_like(o_ref)


def expert_ffn(xs, tile_expert, n_active, w_gu, w_down, tm):
    rows, d = xs.shape
    return pl.pallas_call(
        _expert_ffn_kernel,
        grid_spec=pltpu.PrefetchScalarGridSpec(
            num_scalar_prefetch=2,
            grid=(rows // tm,),
            in_specs=[pl.BlockSpec((tm, d), lambda i, te, na: (jnp.minimum(i, na[0] - 1), 0)),
                      pl.BlockSpec((1, d, 2 * D_FF), lambda i, te, na: (te[i], 0, 0)),
                      pl.BlockSpec((1, D_FF, d), lambda i, te, na: (te[i], 0, 0))],
            out_specs=pl.BlockSpec((tm, d), lambda i, te, na: (i, 0))),
        out_shape=jax.ShapeDtypeStruct((rows, d), F32),
        compiler_params=_cparams("arbitrary"),
        name="expert_ffn",
    )(tile_expert, n_active, xs, w_gu, w_down)


def _router_kernel(a_ref, wo_ref, r_ref, g_ref, wr_ref, h_ref, gate_ref, ids_ref, cnt_ref):
    x = r_ref[...] + jnp.dot(a_ref[...], wo_ref[...], preferred_element_type=F32)
    h_ref[...] = x
    tm = x.shape[0]
    xn = x * _rms_scale(x) * g_ref[...]
    x_hi = xn.astype(BF16)
    x_lo = (xn - x_hi.astype(F32)).astype(BF16)
    prod = jnp.dot(jnp.concatenate([x_hi, x_lo], axis=0), wr_ref[...], preferred_element_type=F32)
    hi_rows, lo_rows = prod[:tm], prod[tm:]
    logits = hi_rows + pltpu.roll(hi_rows, LANES - N_EXPERTS, axis=1) + lo_rows
    lane = lax.broadcasted_iota(I32, logits.shape, 1)
    logits = jnp.where(lane < N_EXPERTS, logits, -jnp.inf)
    m1 = jnp.max(logits, axis=-1, keepdims=True)
    i1 = jnp.min(jnp.where(logits == m1, lane, LANES), axis=-1, keepdims=True)
    rest = jnp.where(lane == i1, -jnp.inf, logits)
    m2 = jnp.max(rest, axis=-1, keepdims=True)
    i2 = jnp.min(jnp.where(rest == m2, lane, LANES), axis=-1, keepdims=True)
    e2 = jnp.exp(m2 - m1)
    w1 = 1.0 / (1.0 + e2)
    w2 = e2 / (1.0 + e2)
    gate_ref[...] = jnp.where(lane == 0, w1, jnp.where(lane == 1, w2, 0.0))
    ids_ref[...] = jnp.where(lane == 0, i1, jnp.where(lane == 1, i2, 0))
    chosen = jnp.where(jnp.logical_or(lane == i1, lane == i2), 1.0, 0.0)

    @pl.when(pl.program_id(0) == 0)
    def _():
        cnt_ref[...] = jnp.zeros_like(cnt_ref)

    n_tile = jnp.sum(chosen, axis=0, keepdims=True)
    n_pad = jnp.ceil(n_tile * (1.0 / RUN_ALIGN)) * RUN_ALIGN
    cnt_ref[...] += jnp.broadcast_to(n_pad, cnt_ref.shape)


def out_proj_router(a, w_o, r, g, w_router):
    t, k = a.shape
    d = w_o.shape[1]
    tm = _row_tile(t)
    w_hi = w_router.astype(BF16)
    w_lo = (w_router - w_hi.astype(F32)).astype(BF16)
    w_packed = jnp.concatenate([w_hi[:, :N_EXPERTS], w_lo[:, :N_EXPERTS],
                                jnp.zeros((d, LANES - 2 * N_EXPERTS), BF16)], axis=1)
    return pl.pallas_call(
        _router_kernel,
        grid=(t // tm,),
        in_specs=[pl.BlockSpec((tm, k), lambda i: (i, 0)),
                  _resident(w_o.shape),
                  pl.BlockSpec((tm, d), lambda i: (i, 0)),
                  pl.BlockSpec((1, d), lambda i: (0, 0)),
                  _resident((d, LANES))],
        out_specs=[pl.BlockSpec((tm, d), lambda i: (i, 0)),
                   pl.BlockSpec((tm, LANES), lambda i: (i, 0)),
                   pl.BlockSpec((tm, LANES), lambda i: (i, 0)),
                   pl.BlockSpec((SUBLANES, LANES), lambda i: (0, 0))],
        out_shape=[jax.ShapeDtypeStruct((t, d), F32),
                   jax.ShapeDtypeStruct((t, LANES), F32),
                   jax.ShapeDtypeStruct((t, LANES), I32),
                   jax.ShapeDtypeStruct((SUBLANES, LANES), F32)],
        compiler_params=_cparams("arbitrary"),
        name="out_proj_router",
    )(a, w_o, r, g.reshape(1, d), w_packed)


META_ROWS = 3


def _staging_rows(tt):
    need = TOP_K * tt + N_EXPERTS * (RUN_ALIGN - 1)
    return -(-need // LANES) * LANES


def _plan_kernel(ids_ref, base_ref, slot_ref, meta_ref, carry_ref):
    @pl.when(pl.program_id(0) == 0)
    def _():
        carry_ref[...] = jnp.zeros_like(carry_ref)

    ids = ids_ref[...]
    tt = ids.shape[0]
    lane = lax.broadcasted_iota(I32, ids.shape, 1)
    pick0 = lane == ids[:, 0:1]
    pick1 = lane == ids[:, 1:2]
    chosen = jnp.where(jnp.logical_or(pick0, pick1), 1.0, 0.0)
    ri = lax.broadcasted_iota(I32, (tt, tt), 0)
    ci = lax.broadcasted_iota(I32, (tt, tt), 1)
    before = jnp.where(ri > ci, 1.0, 0.0).astype(BF16)
    rank = jnp.dot(before, chosen.astype(BF16), preferred_element_type=F32)
    n_pad = jnp.ceil(jnp.sum(chosen, axis=0, keepdims=True) * (1.0 / RUN_ALIGN)) * RUN_ALIGN
    ei = lax.broadcasted_iota(I32, (LANES, LANES), 0)
    ej = lax.broadcasted_iota(I32, (LANES, LANES), 1)
    earlier = jnp.where(ei < ej, 1.0, 0.0).astype(BF16)
    n_pad8 = jnp.broadcast_to(n_pad, (SUBLANES, LANES))
    local = jnp.dot(n_pad8.astype(BF16), earlier, preferred_element_type=F32)[0:1, :]
    slot = local + rank
    s0 = jnp.sum(jnp.where(pick0, slot, 0.0), axis=-1, keepdims=True)
    s1 = jnp.sum(jnp.where(pick1, slot, 0.0), axis=-1, keepdims=True)
    slot_ref[...] = jnp.where(lane == 0, s0, jnp.where(lane == 1, s1, 0.0))
    row = lax.broadcasted_iota(I32, (SUBLANES, LANES), 0)
    sorted_start = base_ref[0:1, :] + carry_ref[0:1, :]
    meta = jnp.where(row == 0, sorted_start, jnp.where(row == 1, local, jnp.where(row == 2, n_pad, 0.0)))
    meta_ref[...] = meta.astype(I32)
    carry_ref[...] += n_pad8


def dispatch_plan(ids, base):
    t = ids.shape[0]
    tt = _row_tile(t)
    return pl.pallas_call(
        _plan_kernel,
        grid=(t // tt,),
        in_specs=[pl.BlockSpec((tt, LANES), lambda i: (i, 0)),
                  pl.BlockSpec((SUBLANES, LANES), lambda i: (0, 0))],
        out_specs=[pl.BlockSpec((tt, LANES), lambda i: (i, 0)),
                   pl.BlockSpec((SUBLANES, LANES), lambda i: (i, 0))],
        out_shape=[jax.ShapeDtypeStruct((t, LANES), F32),
                   jax.ShapeDtypeStruct((SUBLANES * (t // tt), LANES), I32)],
        scratch_shapes=[pltpu.VMEM((SUBLANES, LANES), F32)],
        compiler_params=_cparams("arbitrary"),
        name="dispatch_plan",
    )(ids, base)


def _run_copies(meta_ref, tile, staging, sorted_hbm, sem, to_sorted):
    first = tile * (META_ROWS * N_EXPERTS)
    out = []
    for e in range(N_EXPERTS):
        sorted_start = pl.multiple_of(meta_ref[first + e], RUN_ALIGN)
        local_start = pl.multiple_of(meta_ref[first + N_EXPERTS + e], RUN_ALIGN)
        n = pl.multiple_of(meta_ref[first + 2 * N_EXPERTS + e], RUN_ALIGN)
        a, b = staging.at[pl.ds(local_start, n)], sorted_hbm.at[pl.ds(sorted_start, n)]
        out.append((n, pltpu.make_async_copy(a, b, sem) if to_sorted else pltpu.make_async_copy(b, a, sem)))
    return out


def _start(copies):
    for n, copy in copies:
        pl.when(n > 0)(copy.start)


def _wait(copies):
    for n, copy in copies:
        pl.when(n > 0)(copy.wait)


def _start_then_wait(copies):
    _start(copies)
    _wait(copies)


def _scatter_kernel(meta_ref, tail_ref, x_ref, g_ref, slot_ref, xs_hbm, stage_scr, zero_scr, sem):
    i = pl.program_id(0)
    last = pl.num_programs(0) - 1
    buf = i % 2
    x = x_ref[...]
    xn = (x * _rms_scale(x) * g_ref[...]).astype(BF16)
    tt, rt = x.shape[0], stage_scr.shape[1]
    slot_t = slot_ref[...].T.astype(I32)
    r = lax.broadcasted_iota(I32, (rt, tt), 0)
    place = jnp.where(jnp.logical_or(r == slot_t[0:1, :], r == slot_t[1:2, :]), 1.0, 0.0).astype(BF16)
    stage_scr[buf] = jnp.dot(place, xn, preferred_element_type=F32)
    mine = _run_copies(meta_ref, i, stage_scr.at[buf], xs_hbm, sem.at[buf], to_sorted=True)
    _start(mine)

    @pl.when(i > 0)
    def _():
        _wait(_run_copies(meta_ref, i - 1, stage_scr.at[1 - buf], xs_hbm, sem.at[1 - buf], to_sorted=True))

    @pl.when(i == last)
    def _():
        _wait(mine)
        zero_scr[...] = jnp.zeros_like(zero_scr)
        tails = []
        for e in range(N_EXPERTS):
            start = pl.multiple_of(tail_ref[e], RUN_ALIGN)
            n = pl.multiple_of(tail_ref[N_EXPERTS + e], RUN_ALIGN)
            tails.append((n, pltpu.make_async_copy(zero_scr.at[pl.ds(0, n)], xs_hbm.at[pl.ds(start, n)], sem.at[0])))
        _start_then_wait(tails)

        tm = zero_scr.shape[0]

        def clear_tile(j, carry):
            copy = pltpu.make_async_copy(zero_scr, xs_hbm.at[pl.ds(pl.multiple_of(j * tm, tm), tm)], sem.at[0])
            copy.start()
            copy.wait()
            return carry

        lax.fori_loop(tail_ref[2 * N_EXPERTS], xs_hbm.shape[0] // tm, clear_tile, 0)


def scatter_rows(x, g, slot, meta, tails, n_rows, tm):
    t, d = x.shape
    tt = _row_tile(t)
    return pl.pallas_call(
        _scatter_kernel,
        grid_spec=pltpu.PrefetchScalarGridSpec(
            num_scalar_prefetch=2,
            grid=(t // tt,),
            in_specs=[pl.BlockSpec((tt, d), lambda i, m, tl: (i, 0)),
                      pl.BlockSpec((1, d), lambda i, m, tl: (0, 0)),
                      pl.BlockSpec((tt, LANES), lambda i, m, tl: (i, 0))],
            out_specs=pl.BlockSpec(memory_space=pl.ANY),
            scratch_shapes=[pltpu.VMEM((2, _staging_rows(tt), d), F32), pltpu.VMEM((tm, d), F32),
                            pltpu.SemaphoreType.DMA((2,))]),
        out_shape=jax.ShapeDtypeStruct((n_rows, d), F32),
        compiler_params=_cparams("arbitrary"),
        name="scatter_rows",
    )(meta, tails, x, g.reshape(1, d), slot)


def _combine_kernel(meta_ref, h_ref, gate_ref, slot_ref, fg_ref, ys_hbm, o_ref, stage_scr, sem):
    i = pl.program_id(0)
    buf = i % 2

    def fetch(tile, b):
        return _run_copies(meta_ref, tile, stage_scr.at[b], ys_hbm, sem.at[b], to_sorted=False)

    @pl.when(i == 0)
    def _():
        stage_scr[...] = jnp.zeros_like(stage_scr)
        _start(fetch(0, 0))

    @pl.when(i + 1 < pl.num_programs(0))
    def _():
        _start(fetch(i + 1, 1 - buf))

    tt, rt = h_ref.shape[0], stage_scr.shape[1]
    slot = slot_ref[...].astype(I32)
    gate = gate_ref[...]
    c = lax.broadcasted_iota(I32, (tt, rt), 1)
    weight = (jnp.where(c == slot[:, 0:1], gate[:, 0:1], 0.0)
              + jnp.where(c == slot[:, 1:2], gate[:, 1:2], 0.0))
    _wait(fetch(i, buf))
    moe = jnp.dot(weight.astype(BF16), stage_scr[buf].astype(BF16), preferred_element_type=F32)
    y = h_ref[...] + moe
    o_ref[...] = y * _rms_scale(y) * fg_ref[...]


def combine_rows(h, gates, slot, meta, final_gain, ys):
    t, d = h.shape
    tt = _row_tile(t)
    return pl.pallas_call(
        _combine_kernel,
        grid_spec=pltpu.PrefetchScalarGridSpec(
            num_scalar_prefetch=1,
            grid=(t // tt,),
            in_specs=[pl.BlockSpec((tt, d), lambda i, m: (i, 0)),
                      pl.BlockSpec((tt, LANES), lambda i, m: (i, 0)),
                      pl.BlockSpec((tt, LANES), lambda i, m: (i, 0)),
                      pl.BlockSpec((1, d), lambda i, m: (0, 0)),
                      pl.BlockSpec(memory_space=pl.ANY)],
            out_specs=pl.BlockSpec((tt, d), lambda i, m: (i, 0)),
            scratch_shapes=[pltpu.VMEM((2, _staging_rows(tt), d), F32), pltpu.SemaphoreType.DMA((2,))]),
        out_shape=jax.ShapeDtypeStruct((t, d), F32),
        compiler_params=_cparams("arbitrary"),
        name="combine_rows",
    )(meta, h, gates, slot, final_gain.reshape(1, d), ys)


def attn_out_moe_final_norm(o, w_o, r, g, w_router, w_gu, w_down, final_gain):
    t, d = r.shape
    tt = _row_tile(t)
    n_tok_tiles = t // tt
    tm = 512 if t >= 4096 else 128
    worst = TOP_K * t + n_tok_tiles * N_EXPERTS * (RUN_ALIGN - 1) + N_EXPERTS * tm
    n_rows = -(-worst // tm) * tm
    h, gates, ids, counts = out_proj_router(o, w_o, r, g, w_router)

    cnt = counts[0, :N_EXPERTS].astype(I32)
    tiles = (cnt + tm - 1) // tm
    ends = jnp.cumsum(tiles)
    starts = (ends - tiles) * tm
    base = jnp.zeros((SUBLANES, LANES), F32).at[:, :N_EXPERTS].set(starts.astype(F32))
    n_active = ends[-1:]
    tile_idx = jnp.arange(n_rows // tm, dtype=I32)
    owner = jnp.sum(tile_idx[:, None] >= ends[None, :], axis=1).astype(I32)
    last_owner = jnp.sum(n_active - 1 >= ends).astype(I32)
    tile_expert = jnp.where(tile_idx < n_active, owner, last_owner)
    tails = jnp.concatenate([starts + cnt, tiles * tm - cnt, n_active]).astype(I32)

    slot, meta = dispatch_plan(ids, base)
    meta_flat = meta.reshape(n_tok_tiles, SUBLANES, LANES)[:, :META_ROWS, :N_EXPERTS].reshape(-1)
    xs = scatter_rows(h, g, slot, meta_flat, tails, n_rows, tm)
    ys = expert_ffn(xs, tile_expert, n_active, w_gu, w_down, tm)
    return combine_rows(h, gates, slot, meta_flat, final_gain, ys)


def _split3(v):
    hi = v.astype(BF16)
    r1 = v - hi.astype(F32)
    mid = r1.astype(BF16)
    lo = (r1 - mid.astype(F32)).astype(BF16)
    return hi, mid, lo


def _ssd_block(rows, blk, xbc_ref, dt_ref, z_ref, cw_ref, cb_ref, dtb_ref, alog_ref, dskip_ref, gn_ref,
               expand_ref, yn_ref, h_scr, xext_scr, seq_valid):
    q = SSD_BLOCK
    halo = SUBLANES

    xext_scr[halo:halo + q, :] = xbc_ref[0, rows, :]
    xext = xext_scr[...]
    conv = cb_ref[...] + xext[halo:] * cw_ref[CONV_WIDTH - 1:CONV_WIDTH, :]
    for k in range(CONV_WIDTH - 1):
        delayed = pltpu.roll(xext, CONV_WIDTH - 1 - k, axis=0)
        conv = conv + delayed[halo:] * cw_ref[k:k + 1, :]
    xext_scr[0:halo, :] = xext_scr[q:q + halo, :]
    xbc = _silu(conv)
    xs = xbc[:, :D_INNER]
    b_all = xbc[:, D_INNER:D_INNER + SSM_GROUPS * D_STATE]
    c_all = xbc[:, D_INNER + SSM_GROUPS * D_STATE:]

    v = dt_ref[0, rows, :] + dtb_ref[...]
    dt = jnp.maximum(v, 0.0) + jnp.log1p(jnp.exp(-jnp.abs(v)))
    row = lax.broadcasted_iota(I32, (q, LANES), 0)
    if seq_valid is not None:
        dt = jnp.where(row + blk * q < seq_valid, dt, 0.0)
    a_neg = -jnp.exp(alog_ref[...])
    dta = dt * a_neg

    ti = lax.broadcasted_iota(I32, (q, q), 0)
    si = lax.broadcasted_iota(I32, (q, q), 1)
    causal = ti >= si
    tri = jnp.where(causal, 1.0, 0.0).astype(BF16)
    a_cs = sum(jnp.dot(tri, part, preferred_element_type=F32) for part in _split3(dta))
    a_last = a_cs[q - 1:q, :]
    a_cs_t = a_cs.T

    stacked = jnp.concatenate([dt, jnp.exp(a_cs), jnp.exp(a_last - a_cs),
                               jnp.broadcast_to(jnp.exp(a_last), (SUBLANES, LANES))], axis=0)
    s_hi = stacked.astype(BF16)
    s_lo = (stacked - s_hi.astype(F32)).astype(BF16)
    expand = expand_ref[...]
    expanded = (jnp.dot(s_hi, expand, preferred_element_type=F32)
                + jnp.dot(s_lo, expand, preferred_element_type=F32))
    dt_e = expanded[0:q]
    decay_in_e = expanded[q:2 * q]
    decay_out_e = expanded[2 * q:3 * q]
    decay_blk_e = expanded[3 * q:3 * q + 1]

    xdt = xs * dt_e
    xdt_bf = xdt.astype(BF16)
    wx = (xdt * decay_out_e).astype(BF16)
    low_half = lax.broadcasted_iota(I32, (q, LANES), 1) < SSM_HEAD_DIM

    y_groups = []
    for g in range(SSM_GROUPS):
        bg = b_all[:, g * D_STATE:(g + 1) * D_STATE]
        cg = c_all[:, g * D_STATE:(g + 1) * D_STATE].astype(BF16)
        bg_bf = bg.astype(BF16)
        cbm = lax.dot_general(cg, bg_bf, (((1,), (1,)), ((), ())), preferred_element_type=F32)
        cols = slice(g * GROUP_COLS, (g + 1) * GROUP_COLS)
        h_g = h_scr[g]
        y_g = jnp.dot(cg, h_g.astype(BF16), preferred_element_type=F32) * decay_in_e[:, cols]
        pair_out = []
        for pr in range(HEADS_PER_GROUP // 2):
            parts = []
            for par in range(2):
                h = g * HEADS_PER_GROUP + 2 * pr + par
                seg = (jnp.broadcast_to(a_cs[:, h:h + 1], (q, q))
                       - jnp.broadcast_to(a_cs_t[h:h + 1, :], (q, q)))
                m = (cbm * jnp.exp(jnp.where(causal, seg, -jnp.inf))).astype(BF16)
                pcols = slice((h // 2) * LANES, (h // 2 + 1) * LANES)
                parts.append(jnp.dot(m, xdt_bf[:, pcols], preferred_element_type=F32))
            pair_out.append(jnp.where(low_half, parts[0], parts[1]))
        y_groups.append(y_g + jnp.concatenate(pair_out, axis=1))
        h_scr[g] = (h_g * decay_blk_e[:, cols]
                    + jnp.dot(bg.T.astype(BF16), wx[:, cols], preferred_element_type=F32))

    y = jnp.concatenate(y_groups, axis=1) + dskip_ref[...] * xs
    yz = y * _silu(z_ref[0, rows, :])
    yn_ref[0, rows, :] = (yz * _rms_scale(yz) * gn_ref[...]).astype(yn_ref.dtype)


def _ssd_kernel(xbc_ref, dt_ref, z_ref, cbuf_ref, h0_ref, cw_ref, cb_ref, dtb_ref, alog_ref,
                dskip_ref, gn_ref, expand_ref, yn_ref, hfin_ref, h_scr, xext_scr, *, seq_valid, n_sub):
    c = pl.program_id(1)

    @pl.when(c == 0)
    def _():
        h_scr[...] = h0_ref[0]
        xext_scr[0:SUBLANES, :] = cbuf_ref[0]

    for sub in range(n_sub):
        _ssd_block(slice(sub * SSD_BLOCK, (sub + 1) * SSD_BLOCK), c * n_sub + sub, xbc_ref, dt_ref, z_ref,
                   cw_ref, cb_ref, dtb_ref, alog_ref, dskip_ref, gn_ref, expand_ref, yn_ref, h_scr, xext_scr,
                   seq_valid)

    @pl.when(c == pl.num_programs(1) - 1)
    def _():
        hfin_ref[0] = h_scr[...]


def _state_to_kernel_layout(h):
    b = h.shape[0]
    h = h.reshape(b, SSM_GROUPS, HEADS_PER_GROUP, SSM_HEAD_DIM, D_STATE)
    return h.transpose(0, 1, 4, 2, 3).reshape(b, SSM_GROUPS, D_STATE, GROUP_COLS)


def _state_from_kernel_layout(h):
    b = h.shape[0]
    h = h.reshape(b, SSM_GROUPS, D_STATE, HEADS_PER_GROUP, SSM_HEAD_DIM)
    return h.transpose(0, 1, 3, 4, 2).reshape(b, N_SSM_HEADS, SSM_HEAD_DIM, D_STATE)


def _pad_lanes(v):
    return jnp.pad(v.astype(F32), (0, LANES - v.shape[0])).reshape(1, LANES)


def ssd_mixer(xbc, dt_raw, z, conv_buf, h0, conv_w, conv_b, dt_bias, a_log, d_skip, gnorm, seq_valid):
    bsz, seq_pad, _ = xbc.shape
    q = SSD_BLOCK
    assert seq_pad % q == 0
    n_sub = 2 if (seq_pad // q) % 2 == 0 else 1
    rows_per_step = n_sub * q
    cbuf = jnp.pad(conv_buf, ((0, 0), (SUBLANES - (CONV_WIDTH - 1), 0), (0, 0)))
    expand = (jnp.arange(D_INNER)[None, :] // SSM_HEAD_DIM == jnp.arange(LANES)[:, None]).astype(BF16)
    d_e = jnp.repeat(d_skip.astype(F32), SSM_HEAD_DIM).reshape(1, D_INNER)
    state_spec = pl.BlockSpec((1, SSM_GROUPS, D_STATE, GROUP_COLS), lambda b, c: (b, 0, 0, 0))

    def const(shape):
        return pl.BlockSpec(shape, lambda b, c: (0,) * len(shape))

    yn, h_fin = pl.pallas_call(
        functools.partial(_ssd_kernel, seq_valid=seq_valid, n_sub=n_sub),
        grid=(bsz, seq_pad // rows_per_step),
        in_specs=[pl.BlockSpec((1, rows_per_step, CONV_DIM), lambda b, c: (b, c, 0)),
                  pl.BlockSpec((1, rows_per_step, LANES), lambda b, c: (b, c, 0)),
                  pl.BlockSpec((1, rows_per_step, D_INNER), lambda b, c: (b, c, 0)),
                  pl.BlockSpec((1, SUBLANES, CONV_DIM), lambda b, c: (b, 0, 0)),
                  state_spec,
                  const((CONV_WIDTH, CONV_DIM)), const((1, CONV_DIM)), const((1, LANES)), const((1, LANES)),
                  const((1, D_INNER)), const((1, D_INNER)), const((LANES, D_INNER))],
        out_specs=[pl.BlockSpec((1, rows_per_step, D_INNER), lambda b, c: (b, c, 0)), state_spec],
        out_shape=[jax.ShapeDtypeStruct((bsz, seq_pad, D_INNER), BF16),
                   jax.ShapeDtypeStruct((bsz, SSM_GROUPS, D_STATE, GROUP_COLS), F32)],
        scratch_shapes=[pltpu.VMEM((SSM_GROUPS, D_STATE, GROUP_COLS), F32),
                        pltpu.VMEM((q + SUBLANES, CONV_DIM), F32)],
        compiler_params=_cparams("parallel", "arbitrary"),
        name="ssd_mixer",
    )(xbc, dt_raw, z, cbuf, _state_to_kernel_layout(h0.astype(F32)), conv_w.astype(F32),
      conv_b.astype(F32).reshape(1, CONV_DIM), _pad_lanes(dt_bias), _pad_lanes(a_log), d_e,
      gnorm.astype(F32).reshape(1, D_INNER), expand)
    return yn, _state_from_kernel_layout(h_fin)


_ALIBI_SLOPES = (2.0 ** (-8.0 * np.arange(1, N_Q_HEADS + 1, dtype=np.float32) / N_Q_HEADS)).astype(np.float32)


def _attn_bias(lbq, chunk, past_rows):
    w = WINDOW + lbq
    qi = np.arange(lbq)[:, None]
    lo = (qi // chunk) * chunk
    variants = []
    for lead in (past_rows, WINDOW):
        si = np.arange(w)[None, :] + (WINDOW - lead)
        dist = np.abs(WINDOW + qi - si).astype(np.float32)
        valid = (si >= lo) & (si < lo + WINDOW + chunk)
        per_head = [np.where(valid, -(_ALIBI_SLOPES[h] * dist), np.float32(NEG_INF)) for h in range(N_Q_HEADS)]
        variants.append(np.stack([np.concatenate([per_head[4 * kv + par], per_head[4 * kv + 2 + par]], axis=0)
                                  for kv in range(N_KV_HEADS) for par in range(2)]))
    return np.stack(variants).astype(np.float32)


def _attn_kernel(sink_ref, q_ref, kv_ref, bias_ref, o_ref, *, lbq, n_sub, past_rows):
    for sub in range(n_sub):
        _attn_block(slice(sub * lbq, (sub + 1) * lbq), pl.program_id(1) * n_sub + sub,
                    sink_ref, q_ref, kv_ref, bias_ref, o_ref, lbq, past_rows)


def _attn_block(rows, n, sink_ref, q_ref, kv_ref, bias_ref, o_ref, lbq, past_rows):
    w = WINDOW + lbq
    base = pl.multiple_of(jnp.maximum(n * lbq + (past_rows - WINDOW), 0), SUBLANES)
    variant = jnp.minimum(n, 1)
    kband = kv_ref[0, pl.ds(base, w), :KV_DIM]
    vband = kv_ref[0, pl.ds(base, w), KV_DIM:]
    first_pair = lax.broadcasted_iota(I32, (2 * lbq, 1), 0) < lbq
    lane = lax.broadcasted_iota(I32, (w, LANES), 1)
    low_half = lane < ATTN_HEAD_DIM

    for kv in range(N_KV_HEADS):
        pcols = slice((kv // 2) * LANES, (kv // 2 + 1) * LANES)
        kpair, vpair = kband[:, pcols], vband[:, pcols]
        kswap = pltpu.roll(kpair, ATTN_HEAD_DIM, axis=1)
        vswap = pltpu.roll(vpair, ATTN_HEAD_DIM, axis=1)
        in_low = kv % 2 == 0
        k_lo = jnp.where(low_half, kpair if in_low else kswap, 0.0).astype(BF16)
        k_hi = jnp.where(low_half, 0.0, kswap if in_low else kpair).astype(BF16)
        v_lo = jnp.where(low_half, vpair if in_low else vswap, 0.0).astype(BF16)
        v_hi = jnp.where(low_half, 0.0, vswap if in_low else vpair).astype(BF16)
        qs = q_ref[2 * kv:2 * kv + 2, 0, rows, :].reshape(2 * lbq, LANES)
        out = None
        for par, (kx, vx) in enumerate(((k_lo, v_lo), (k_hi, v_hi))):
            h0, h1 = 4 * kv + par, 4 * kv + 2 + par
            sink = jnp.where(first_pair, sink_ref[h0], sink_ref[h1])
            s = lax.dot_general(qs, kx, (((1,), (1,)), ((), ())), preferred_element_type=F32)
            s = s + bias_ref[variant, 2 * kv + par]
            m = jnp.maximum(jnp.max(s, axis=-1, keepdims=True), sink)
            p = jnp.exp(s - m)
            denom = jnp.sum(p, axis=-1, keepdims=True) + jnp.exp(sink - m)
            pn = (p / denom).astype(BF16)
            part = jnp.dot(pn, vx, preferred_element_type=F32)
            out = part if out is None else out + part
        o_ref[0, rows, (2 * kv) * LANES:(2 * kv + 1) * LANES] = out[:lbq].astype(o_ref.dtype)
        o_ref[0, rows, (2 * kv + 1) * LANES:(2 * kv + 2) * LANES] = out[lbq:].astype(o_ref.dtype)


def window_attention(q_pairs, kv_all, sinks, *, lbq, chunk):
    n_pairs, bsz, seq, _ = q_pairs.shape
    rows = kv_all.shape[1]
    past_rows = rows - seq
    assert past_rows in (0, WINDOW) and rows >= WINDOW + lbq
    bias = jnp.asarray(_attn_bias(lbq, chunk, past_rows))
    n_sub = 2 if (seq // lbq) % 2 == 0 else 1
    step = n_sub * lbq
    return pl.pallas_call(
        functools.partial(_attn_kernel, lbq=lbq, n_sub=n_sub, past_rows=past_rows),
        grid_spec=pltpu.PrefetchScalarGridSpec(
            num_scalar_prefetch=1,
            grid=(bsz, seq // step),
            in_specs=[pl.BlockSpec((n_pairs, 1, step, LANES), lambda b, n, s: (0, b, n, 0)),
                      pl.BlockSpec((1, rows, 2 * KV_DIM), lambda b, n, s: (b, 0, 0)),
                      _resident(bias.shape)],
            out_specs=pl.BlockSpec((1, step, D_MODEL), lambda b, n, s: (b, n, 0))),
        out_shape=jax.ShapeDtypeStruct((bsz, seq, D_MODEL), BF16),
        compiler_params=_cparams("parallel", "arbitrary"),
        name="window_attention",
    )(sinks.astype(F32), q_pairs, kv_all, bias)


def _trunk(x, conv_buf, ssm_state, k_past, v_past, past_valid, p):
    bsz, seq, d = x.shape
    t = bsz * seq
    x2 = x.reshape(t, d)

    seq_pad = -(-seq // SSD_BLOCK) * SSD_BLOCK
    x_in = x2 if seq_pad == seq else jnp.pad(x, ((0, 0), (0, seq_pad - seq), (0, 0))).reshape(bsz * seq_pad, d)
    z, xbc, dt_raw = in_proj(x_in, p['norm_mix'][0], p['w_z'], p['w_xbc'], p['w_dt'])
    xbc3 = xbc.reshape(bsz, seq_pad, CONV_DIM)
    new_conv = jnp.concatenate([conv_buf.astype(F32), xbc3[:, :seq]], axis=1)[:, seq:]
    yn, new_ssm = ssd_mixer(xbc3, dt_raw.reshape(bsz, seq_pad, LANES), z.reshape(bsz, seq_pad, D_INNER),
                            conv_buf.astype(F32), ssm_state, p['ssm_conv_w'], p['ssm_conv_b'],
                            p['ssm_dt_bias'], p['ssm_a_log'], p['ssm_d'], p['ssm_norm'],
                            None if seq_pad == seq else seq)
    yn = yn[:, :seq]

    h, kv, q_pairs = layer0_tail(yn.reshape(t, D_INNER), p['ssm_out_w'], x2, p['norm_ffn'][0],
                                 p['ffn_w_gate_up'], p['ffn_w_down'], p['kv_norm'], p['norm_mix'][1],
                                 p['w_kv'], p['w_q'])
    kv = kv.reshape(bsz, seq, 2 * KV_DIM)
    k_new, v_new = kv[..., :KV_DIM], kv[..., KV_DIM:]
    if past_valid:
        past = jnp.concatenate([k_past.reshape(bsz, WINDOW, KV_DIM), v_past.reshape(bsz, WINDOW, KV_DIM)], axis=2)
        kv_all = jnp.concatenate([past.astype(F32), kv], axis=1)
    else:
        kv_all = kv

    chunk = min(ATTN_CHUNK, seq)
    lbq = 2 * chunk if seq % (2 * chunk) == 0 else chunk
    q_pairs = q_pairs.reshape(D_MODEL // LANES, bsz, seq, LANES)
    o = window_attention(q_pairs, kv_all, p['attn_sinks'], lbq=lbq, chunk=chunk)

    y = attn_out_moe_final_norm(o.reshape(t, D_MODEL), p['w_o'], h, p['norm_ffn'][1], p['moe_router'],
                                p['moe_w_gate_up'], p['moe_w_down'], p['final_norm'])

    k4 = k_new.reshape(bsz, seq, N_KV_HEADS, ATTN_HEAD_DIM)
    v4 = v_new.reshape(bsz, seq, N_KV_HEADS, ATTN_HEAD_DIM)
    return y.reshape(bsz, seq, d), new_conv[None], new_ssm[None], k4, v4


def kernel(x_prompt, x_sample, state_conv, state_ssm, cache_k, cache_v, norm_mix, norm_ffn, ssm_in_w, ssm_conv_w,
           ssm_conv_b, ssm_dt_bias, ssm_a_log, ssm_d, ssm_norm, ssm_out_w, kv_norm, w_kv, w_q, attn_sinks, w_o,
           ffn_w_gate_up, ffn_w_down, moe_router, moe_w_gate_up, moe_w_down, final_norm):
    in_w = ssm_in_w[0]
    n_dt = N_SSM_HEADS
    p = {
        'norm_mix': norm_mix, 'norm_ffn': norm_ffn,
        'w_z': in_w[:, :D_INNER].astype(BF16),
        'w_xbc': in_w[:, D_INNER:D_INNER + CONV_DIM].astype(BF16),
        'w_dt': jnp.pad(in_w[:, D_INNER + CONV_DIM:], ((0, 0), (0, LANES - n_dt))).astype(BF16),
        'ssm_conv_w': ssm_conv_w[0], 'ssm_conv_b': ssm_conv_b[0], 'ssm_dt_bias': ssm_dt_bias[0],
        'ssm_a_log': ssm_a_log[0], 'ssm_d': ssm_d[0], 'ssm_norm': ssm_norm[0],
        'ssm_out_w': ssm_out_w[0].astype(BF16),
        'kv_norm': kv_norm, 'w_kv': w_kv.astype(BF16), 'w_q': w_q[0].astype(BF16),
        'attn_sinks': attn_sinks[0], 'w_o': w_o[0].astype(BF16),
        'ffn_w_gate_up': ffn_w_gate_up.astype(BF16), 'ffn_w_down': ffn_w_down.astype(BF16),
        'moe_router': jnp.pad(moe_router[0].astype(F32), ((0, 0), (0, LANES - N_EXPERTS))),
        'moe_w_gate_up': moe_w_gate_up[0].astype(BF16), 'moe_w_down': moe_w_down[0].astype(BF16),
        'final_norm': final_norm,
    }
    bsz, seq_p = x_prompt.shape[:2]
    dt = x_prompt.dtype
    zero_conv = jnp.zeros((bsz, CONV_WIDTH - 1, CONV_DIM), dt)
    zero_ssm = jnp.zeros((bsz, N_SSM_HEADS, SSM_HEAD_DIM, D_STATE), dt)
    zero_kv = jnp.zeros((bsz, WINDOW, N_KV_HEADS, ATTN_HEAD_DIM), dt)
    y_p, conv_p, ssm_p, k_p, v_p = _trunk(x_prompt, zero_conv, zero_ssm, zero_kv, zero_kv, False, p)
    keep = min(WINDOW, seq_p)
    y_s, conv_s, ssm_s, k_s, v_s = _trunk(x_sample, state_conv[0], state_ssm[0], cache_k, cache_v, True, p)
    return (y_p, y_s, conv_p, ssm_p, k_p[:, seq_p - keep:], v_p[:, seq_p - keep:], conv_s, ssm_s, k_s, v_s)
```

```python
import functools
import math

import jax
import jax.numpy as jnp
import numpy as np
from jax import lax
from jax.experimental import pallas as pl
from jax.experimental.pallas import tpu as pltpu

F32 = jnp.float32
BF16 = jnp.bfloat16
I32 = jnp.int32

D_MODEL = 1024
EPS = 1e-6
D_INNER = 2048
SSM_HEAD_DIM = 64
N_SSM_HEADS = 32
SSM_GROUPS = 4
HEADS_PER_GROUP = N_SSM_HEADS // SSM_GROUPS
D_STATE = 128
CONV_WIDTH = 4
CONV_DIM = D_INNER + 2 * SSM_GROUPS * D_STATE
ATTN_HEAD_DIM = 64
N_Q_HEADS = 16
N_KV_HEADS = 4
KV_DIM = N_KV_HEADS * ATTN_HEAD_DIM
WINDOW = 128
ATTN_CHUNK = 64
ATTN_SCALE = 1.0 / math.sqrt(ATTN_HEAD_DIM)
NEG_INF = -1e30
D_FF = 2816
N_EXPERTS = 8
TOP_K = 2

LANES = 128
SUBLANES = 8
VMEM_LIMIT_BYTES = 56 * 1024 * 1024
RUN_ALIGN = SUBLANES

SSD_BLOCK = 128
GROUP_COLS = HEADS_PER_GROUP * SSM_HEAD_DIM
FF_CHUNK = 512


def _cparams(*sem):
    return pltpu.CompilerParams(dimension_semantics=sem, vmem_limit_bytes=VMEM_LIMIT_BYTES)


def _row_tile(t, pref=512):
    return pref if t % pref == 0 else t


def _resident(shape):
    return pl.BlockSpec(shape, lambda *_: (0,) * len(shape), pipeline_mode=pl.Buffered(1))


def _rms_scale(x):
    return lax.rsqrt(jnp.mean(x * x, axis=-1, keepdims=True) + EPS)


def _silu(x):
    half = 0.5 * x
    return half + half * jnp.tanh(half)


def _in_proj_kernel(x_ref, g_ref, wz_ref, wx_ref, wd_ref, z_ref, xbc_ref, dt_ref):
    x = x_ref[...]
    xn = (x * _rms_scale(x) * g_ref[...]).astype(BF16)
    z_ref[...] = jnp.dot(xn, wz_ref[...], preferred_element_type=F32)
    xbc_ref[...] = jnp.dot(xn, wx_ref[...], preferred_element_type=F32)
    dt_ref[...] = jnp.dot(xn, wd_ref[...], preferred_element_type=F32)


def in_proj(x, g, w_z, w_xbc, w_dt):
    t, k = x.shape
    tm = _row_tile(t)
    ws = (w_z, w_xbc, w_dt)
    return pl.pallas_call(
        _in_proj_kernel,
        grid=(t // tm,),
        in_specs=[pl.BlockSpec((tm, k), lambda i: (i, 0)), pl.BlockSpec((1, k), lambda i: (0, 0))]
        + [_resident(w.shape) for w in ws],
        out_specs=[pl.BlockSpec((tm, w.shape[1]), lambda i: (i, 0)) for w in ws],
        out_shape=[jax.ShapeDtypeStruct((t, w.shape[1]), F32) for w in ws],
        compiler_params=_cparams("parallel"),
        name="in_proj",
    )(x, g.reshape(1, k), *ws)


def _swiglu(xb, wgu_ref, wd_ref):
    acc = None
    for c0 in range(0, D_FF, FF_CHUNK):
        cw = min(FF_CHUNK, D_FF - c0)
        gg = jnp.dot(xb, wgu_ref[0, :, c0:c0 + cw], preferred_element_type=F32)
        uu = jnp.dot(xb, wgu_ref[0, :, D_FF + c0:D_FF + c0 + cw], preferred_element_type=F32)
        act = (_silu(gg) * uu).astype(BF16)
        part = jnp.dot(act, wd_ref[0, c0:c0 + cw, :], preferred_element_type=F32)
        acc = part if acc is None else acc + part
    return acc


def _layer0_tail_kernel(a_ref, wo_ref, r_ref, gf_ref, wgu_ref, wd_ref, gkv_ref, gq_ref, wkv_ref, wq_ref,
                        h_ref, kv_ref, q_ref):
    h1 = r_ref[...] + jnp.dot(a_ref[...], wo_ref[...], preferred_element_type=F32)
    xb = (h1 * _rms_scale(h1) * gf_ref[...]).astype(BF16)
    h2 = h1 + _swiglu(xb, wgu_ref, wd_ref)
    h_ref[...] = h2
    xs = h2 * _rms_scale(h2)
    kv_ref[...] = jnp.dot((xs * gkv_ref[...]).astype(BF16), wkv_ref[...], preferred_element_type=F32)
    q = jnp.dot((xs * gq_ref[...]).astype(BF16), wq_ref[...], preferred_element_type=F32)
    q = (q * ATTN_SCALE).astype(q_ref.dtype)
    for p in range(q_ref.shape[0]):
        q_ref[p] = q[:, p * LANES:(p + 1) * LANES]


def layer0_tail(a, w_out, r, g_ffn, w_gu, w_down, g_kv, g_q, w_kv, w_q):
    t, k = a.shape
    d = w_out.shape[1]
    tm = _row_tile(t)
    n_pairs = w_q.shape[1] // LANES

    def gain():
        return pl.BlockSpec((1, d), lambda i: (0, 0))

    return pl.pallas_call(
        _layer0_tail_kernel,
        grid=(t // tm,),
        in_specs=[pl.BlockSpec((tm, k), lambda i: (i, 0)), _resident(w_out.shape),
                  pl.BlockSpec((tm, d), lambda i: (i, 0)), gain(),
                  _resident(w_gu.shape), _resident(w_down.shape), gain(), gain(),
                  _resident(w_kv.shape), _resident(w_q.shape)],
        out_specs=[pl.BlockSpec((tm, d), lambda i: (i, 0)),
                   pl.BlockSpec((tm, w_kv.shape[1]), lambda i: (i, 0)),
                   pl.BlockSpec((n_pairs, tm, LANES), lambda i: (0, i, 0))],
        out_shape=[jax.ShapeDtypeStruct((t, d), F32),
                   jax.ShapeDtypeStruct((t, w_kv.shape[1]), F32),
                   jax.ShapeDtypeStruct((n_pairs, t, LANES), BF16)],
        compiler_params=_cparams("parallel"),
        name="layer0_tail",
    )(a, w_out, r, g_ffn.reshape(1, d), w_gu, w_down, g_kv.reshape(1, d), g_q.reshape(1, d), w_kv, w_q)


def _expert_ffn_kernel(te_ref, na_ref, x_ref, wgu_ref, wd_ref, o_ref):
    active = pl.program_id(0) < na_ref[0]

    @pl.when(active)
    def _():
        o_ref[...] = _swiglu(x_ref[...].astype(BF16), wgu_ref, wd_ref)

    @pl.when(jnp.logical_not(active))
    def _():
        o_ref[...] = jnp.zeros_like(o_ref)


def expert_ffn(xs, tile_expert, n_active, w_gu, w_down, tm):
    rows, d = xs.shape
    return pl.pallas_call(
        _expert_ffn_kernel,
        grid_spec=pltpu.PrefetchScalarGridSpec(
            num_scalar_prefetch=2,
            grid=(rows // tm,),
            in_specs=[pl.BlockSpec((tm, d), lambda i, te, na: (jnp.minimum(i, na[0] - 1), 0)),
                      pl.BlockSpec((1, d, 2 * D_FF), lambda i, te, na: (te[i], 0, 0)),
                      pl.BlockSpec((1, D_FF, d), lambda i, te, na: (te[i], 0, 0))],
            out_specs=pl.BlockSpec((tm, d), lambda i, te, na: (i, 0))),
        out_shape=jax.ShapeDtypeStruct((rows, d), F32),
        compiler_params=_cparams("arbitrary"),
        name="expert_ffn",
    )(tile_expert, n_active, xs, w_gu, w_down)


def _router_kernel(a_ref, wo_ref, r_ref, g_ref, wr_ref, h_ref, gate_ref, ids_ref, cnt_ref):
    x = r_ref[...] + jnp.dot(a_ref[...], wo_ref[...], preferred_element_type=F32)
    h_ref[...] = x
    tm = x.shape[0]
    xn = x * _rms_scale(x) * g_ref[...]
    x_hi = xn.astype(BF16)
    x_lo = (xn - x_hi.astype(F32)).astype(BF16)
    prod = jnp.dot(jnp.concatenate([x_hi, x_lo], axis=0), wr_ref[...], preferred_element_type=F32)
    hi_rows, lo_rows = prod[:tm], prod[tm:]
    logits = hi_rows + pltpu.roll(hi_rows, LANES - N_EXPERTS, axis=1) + lo_rows
    lane = lax.broadcasted_iota(I32, logits.shape, 1)
    logits = jnp.where(lane < N_EXPERTS, logits, -jnp.inf)
    m1 = jnp.max(logits, axis=-1, keepdims=True)
    i1 = jnp.min(jnp.where(logits == m1, lane, LANES), axis=-1, keepdims=True)
    rest = jnp.where(lane == i1, -jnp.inf, logits)
    m2 = jnp.max(rest, axis=-1, keepdims=True)
    i2 = jnp.min(jnp.where(rest == m2, lane, LANES), axis=-1, keepdims=True)
    e2 = jnp.exp(m2 - m1)
    w1 = 1.0 / (1.0 + e2)
    w2 = e2 / (1.0 + e2)
    gate_ref[...] = jnp.where(lane == 0, w1, jnp.where(lane == 1, w2, 0.0))
    ids_ref[...] = jnp.where(lane == 0, i1, jnp.where(lane == 1, i2, 0))
    chosen = jnp.where(jnp.logical_or(lane == i1, lane == i2), 1.0, 0.0)

    @pl.when(pl.program_id(0) == 0)
    def _():
        cnt_ref[...] = jnp.zeros_like(cnt_ref)

    n_tile = jnp.sum(chosen, axis=0, keepdims=True)
    n_pad = jnp.ceil(n_tile * (1.0 / RUN_ALIGN)) * RUN_ALIGN
    cnt_ref[...] += jnp.broadcast_to(n_pad, cnt_ref.shape)


def out_proj_router(a, w_o, r, g, w_router):
    t, k = a.shape
    d = w_o.shape[1]
    tm = _row_tile(t)
    w_hi = w_router.astype(BF16)
    w_lo = (w_router - w_hi.astype(F32)).astype(BF16)
    w_packed = jnp.concatenate([w_hi[:, :N_EXPERTS], w_lo[:, :N_EXPERTS],
                                jnp.zeros((d, LANES - 2 * N_EXPERTS), BF16)], axis=1)
    return pl.pallas_call(
        _router_kernel,
        grid=(t // tm,),
        in_specs=[pl.BlockSpec((tm, k), lambda i: (i, 0)),
                  _resident(w_o.shape),
                  pl.BlockSpec((tm, d), lambda i: (i, 0)),
                  pl.BlockSpec((1, d), lambda i: (0, 0)),
                  _resident((d, LANES))],
        out_specs=[pl.BlockSpec((tm, d), lambda i: (i, 0)),
                   pl.BlockSpec((tm, LANES), lambda i: (i, 0)),
                   pl.BlockSpec((tm, LANES), lambda i: (i, 0)),
                   pl.BlockSpec((SUBLANES, LANES), lambda i: (0, 0))],
        out_shape=[jax.ShapeDtypeStruct((t, d), F32),
                   jax.ShapeDtypeStruct((t, LANES), F32),
                   jax.ShapeDtypeStruct((t, LANES), I32),
                   jax.ShapeDtypeStruct((SUBLANES, LANES), F32)],
        compiler_params=_cparams("arbitrary"),
        name="out_proj_router",
    )(a, w_o, r, g.reshape(1, d), w_packed)


META_ROWS = 3


def _staging_rows(tt):
    need = TOP_K * tt + N_EXPERTS * (RUN_ALIGN - 1)
    return -(-need // LANES) * LANES


def _plan_kernel(ids_ref, base_ref, slot_ref, meta_ref, carry_ref):
    @pl.when(pl.program_id(0) == 0)
    def _():
        carry_ref[...] = jnp.zeros_like(carry_ref)

    ids = ids_ref[...]
    tt = ids.shape[0]
    lane = lax.broadcasted_iota(I32, ids.shape, 1)
    pick0 = lane == ids[:, 0:1]
    pick1 = lane == ids[:, 1:2]
    chosen = jnp.where(jnp.logical_or(pick0, pick1), 1.0, 0.0)
    ri = lax.broadcasted_iota(I32, (tt, tt), 0)
    ci = lax.broadcasted_iota(I32, (tt, tt), 1)
    before = jnp.where(ri > ci, 1.0, 0.0).astype(BF16)
    rank = jnp.dot(before, chosen.astype(BF16), preferred_element_type=F32)
    n_pad = jnp.ceil(jnp.sum(chosen, axis=0, keepdims=True) * (1.0 / RUN_ALIGN)) * RUN_ALIGN
    ei = lax.broadcasted_iota(I32, (LANES, LANES), 0)
    ej = lax.broadcasted_iota(I32, (LANES, LANES), 1)
    earlier = jnp.where(ei < ej, 1.0, 0.0).astype(BF16)
    n_pad8 = jnp.broadcast_to(n_pad, (SUBLANES, LANES))
    local = jnp.dot(n_pad8.astype(BF16), earlier, preferred_element_type=F32)[0:1, :]
    slot = local + rank
    s0 = jnp.sum(jnp.where(pick0, slot, 0.0), axis=-1, keepdims=True)
    s1 = jnp.sum(jnp.where(pick1, slot, 0.0), axis=-1, keepdims=True)
    slot_ref[...] = jnp.where(lane == 0, s0, jnp.where(lane == 1, s1, 0.0))
    row = lax.broadcasted_iota(I32, (SUBLANES, LANES), 0)
    sorted_start = base_ref[0:1, :] + carry_ref[0:1, :]
    meta = jnp.where(row == 0, sorted_start, jnp.where(row == 1, local, jnp.where(row == 2, n_pad, 0.0)))
    meta_ref[...] = meta.astype(I32)
    carry_ref[...] += n_pad8


def dispatch_plan(ids, base):
    t = ids.shape[0]
    tt = _row_tile(t)
    return pl.pallas_call(
        _plan_kernel,
        grid=(t // tt,),
        in_specs=[pl.BlockSpec((tt, LANES), lambda i: (i, 0)),
                  pl.BlockSpec((SUBLANES, LANES), lambda i: (0, 0))],
        out_specs=[pl.BlockSpec((tt, LANES), lambda i: (i, 0)),
                   pl.BlockSpec((SUBLANES, LANES), lambda i: (i, 0))],
        out_shape=[jax.ShapeDtypeStruct((t, LANES), F32),
                   jax.ShapeDtypeStruct((SUBLANES * (t // tt), LANES), I32)],
        scratch_shapes=[pltpu.VMEM((SUBLANES, LANES), F32)],
        compiler_params=_cparams("arbitrary"),
        name="dispatch_plan",
    )(ids, base)


def _run_copies(meta_ref, tile, staging, sorted_hbm, sem, to_sorted):
    first = tile * (META_ROWS * N_EXPERTS)
    out = []
    for e in range(N_EXPERTS):
        sorted_start = pl.multiple_of(meta_ref[first + e], RUN_ALIGN)
        local_start = pl.multiple_of(meta_ref[first + N_EXPERTS + e], RUN_ALIGN)
        n = pl.multiple_of(meta_ref[first + 2 * N_EXPERTS + e], RUN_ALIGN)
        a, b = staging.at[pl.ds(local_start, n)], sorted_hbm.at[pl.ds(sorted_start, n)]
        out.append((n, pltpu.make_async_copy(a, b, sem) if to_sorted else pltpu.make_async_copy(b, a, sem)))
    return out


def _start(copies):
    for n, copy in copies:
        pl.when(n > 0)(copy.start)


def _wait(copies):
    for n, copy in copies:
        pl.when(n > 0)(copy.wait)


def _start_then_wait(copies):
    _start(copies)
    _wait(copies)


def _scatter_kernel(meta_ref, tail_ref, x_ref, g_ref, slot_ref, xs_hbm, stage_scr, zero_scr, sem):
    i = pl.program_id(0)
    last = pl.num_programs(0) - 1
    buf = i % 2
    x = x_ref[...]
    xn = (x * _rms_scale(x) * g_ref[...]).astype(BF16)
    tt, rt = x.shape[0], stage_scr.shape[1]
    slot_t = slot_ref[...].T.astype(I32)
    r = lax.broadcasted_iota(I32, (rt, tt), 0)
    place = jnp.where(jnp.logical_or(r == slot_t[0:1, :], r == slot_t[1:2, :]), 1.0, 0.0).astype(BF16)
    stage_scr[buf] = jnp.dot(place, xn, preferred_element_type=F32)
    mine = _run_copies(meta_ref, i, stage_scr.at[buf], xs_hbm, sem.at[buf], to_sorted=True)
    _start(mine)

    @pl.when(i > 0)
    def _():
        _wait(_run_copies(meta_ref, i - 1, stage_scr.at[1 - buf], xs_hbm, sem.at[1 - buf], to_sorted=True))

    @pl.when(i == last)
    def _():
        _wait(mine)
        zero_scr[...] = jnp.zeros_like(zero_scr)
        tails = []
        for e in range(N_EXPERTS):
            start = pl.multiple_of(tail_ref[e], RUN_ALIGN)
            n = pl.multiple_of(tail_ref[N_EXPERTS + e], RUN_ALIGN)
            tails.append((n, pltpu.make_async_copy(zero_scr.at[pl.ds(0, n)], xs_hbm.at[pl.ds(start, n)], sem.at[0])))
        _start_then_wait(tails)

        tm = zero_scr.shape[0]

        def clear_tile(j, carry):
            copy = pltpu.make_async_copy(zero_scr, xs_hbm.at[pl.ds(pl.multiple_of(j * tm, tm), tm)], sem.at[0])
            copy.start()
            copy.wait()
            return carry

        lax.fori_loop(tail_ref[2 * N_EXPERTS], xs_hbm.shape[0] // tm, clear_tile, 0)


def scatter_rows(x, g, slot, meta, tails, n_rows, tm):
    t, d = x.shape
    tt = _row_tile(t)
    return pl.pallas_call(
        _scatter_kernel,
        grid_spec=pltpu.PrefetchScalarGridSpec(
            num_scalar_prefetch=2,
            grid=(t // tt,),
            in_specs=[pl.BlockSpec((tt, d), lambda i, m, tl: (i, 0)),
                      pl.BlockSpec((1, d), lambda i, m, tl: (0, 0)),
                      pl.BlockSpec((tt, LANES), lambda i, m, tl: (i, 0))],
            out_specs=pl.BlockSpec(memory_space=pl.ANY),
            scratch_shapes=[pltpu.VMEM((2, _staging_rows(tt), d), F32), pltpu.VMEM((tm, d), F32),
                            pltpu.SemaphoreType.DMA((2,))]),
        out_shape=jax.ShapeDtypeStruct((n_rows, d), F32),
        compiler_params=_cparams("arbitrary"),
        name="scatter_rows",
    )(meta, tails, x, g.reshape(1, d), slot)


def _combine_kernel(meta_ref, h_ref, gate_ref, slot_ref, fg_ref, ys_hbm, o_ref, stage_scr, sem):
    i = pl.program_id(0)
    buf = i % 2

    def fetch(tile, b):
        return _run_copies(meta_ref, tile, stage_scr.at[b], ys_hbm, sem.at[b], to_sorted=False)

    @pl.when(i == 0)
    def _():
        stage_scr[...] = jnp.zeros_like(stage_scr)
        _start(fetch(0, 0))

    @pl.when(i + 1 < pl.num_programs(0))
    def _():
        _start(fetch(i + 1, 1 - buf))

    tt, rt = h_ref.shape[0], stage_scr.shape[1]
    slot = slot_ref[...].astype(I32)
    gate = gate_ref[...]
    c = lax.broadcasted_iota(I32, (tt, rt), 1)
    weight = (jnp.where(c == slot[:, 0:1], gate[:, 0:1], 0.0)
              + jnp.where(c == slot[:, 1:2], gate[:, 1:2], 0.0))
    _wait(fetch(i, buf))
    moe = jnp.dot(weight.astype(BF16), stage_scr[buf].astype(BF16), preferred_element_type=F32)
    y = h_ref[...] + moe
    o_ref[...] = y * _rms_scale(y) * fg_ref[...]


def combine_rows(h, gates, slot, meta, final_gain, ys):
    t, d = h.shape
    tt = _row_tile(t)
    return pl.pallas_call(
        _combine_kernel,
        grid_spec=pltpu.PrefetchScalarGridSpec(
            num_scalar_prefetch=1,
            grid=(t // tt,),
            in_specs=[pl.BlockSpec((tt, d), lambda i, m: (i, 0)),
                      pl.BlockSpec((tt, LANES), lambda i, m: (i, 0)),
                      pl.BlockSpec((tt, LANES), lambda i, m: (i, 0)),
                      pl.BlockSpec((1, d), lambda i, m: (0, 0)),
                      pl.BlockSpec(memory_space=pl.ANY)],
            out_specs=pl.BlockSpec((tt, d), lambda i, m: (i, 0)),
            scratch_shapes=[pltpu.VMEM((2, _staging_rows(tt), d), F32), pltpu.SemaphoreType.DMA((2,))]),
        out_shape=jax.ShapeDtypeStruct((t, d), F32),
        compiler_params=_cparams("arbitrary"),
        name="combine_rows",
    )(meta, h, gates, slot, final_gain.reshape(1, d), ys)


def attn_out_moe_final_norm(o, w_o, r, g, w_router, w_gu, w_down, final_gain):
    t, d = r.shape
    tt = _row_tile(t)
    n_tok_tiles = t // tt
    tm = 512 if t >= 4096 else 128
    worst = TOP_K * t + n_tok_tiles * N_EXPERTS * (RUN_ALIGN - 1) + N_EXPERTS * tm
    n_rows = -(-worst // tm) * tm
    h, gates, ids, counts = out_proj_router(o, w_o, r, g, w_router)

    cnt = counts[0, :N_EXPERTS].astype(I32)
    tiles = (cnt + tm - 1) // tm
    ends = jnp.cumsum(tiles)
    starts = (ends - tiles) * tm
    base = jnp.zeros((SUBLANES, LANES), F32).at[:, :N_EXPERTS].set(starts.astype(F32))
    n_active = ends[-1:]
    tile_idx = jnp.arange(n_rows // tm, dtype=I32)
    owner = jnp.sum(tile_idx[:, None] >= ends[None, :], axis=1).astype(I32)
    last_owner = jnp.sum(n_active - 1 >= ends).astype(I32)
    tile_expert = jnp.where(tile_idx < n_active, owner, last_owner)
    tails = jnp.concatenate([starts + cnt, tiles * tm - cnt, n_active]).astype(I32)

    slot, meta = dispatch_plan(ids, base)
    meta_flat = meta.reshape(n_tok_tiles, SUBLANES, LANES)[:, :META_ROWS, :N_EXPERTS].reshape(-1)
    xs = scatter_rows(h, g, slot, meta_flat, tails, n_rows, tm)
    ys = expert_ffn(xs, tile_expert, n_active, w_gu, w_down, tm)
    return combine_rows(h, gates, slot, meta_flat, final_gain, ys)


def _split3(v):
    hi = v.astype(BF16)
    r1 = v - hi.astype(F32)
    mid = r1.astype(BF16)
    lo = (r1 - mid.astype(F32)).astype(BF16)
    return hi, mid, lo


def _ssd_block(rows, blk, xbc_ref, dt_ref, z_ref, cw_ref, cb_ref, dtb_ref, alog_ref, dskip_ref, gn_ref,
               expand_ref, yn_ref, h_scr, xext_scr, seq_valid):
    q = SSD_BLOCK
    halo = SUBLANES

    xext_scr[halo:halo + q, :] = xbc_ref[0, rows, :]
    xext = xext_scr[...]
    conv = cb_ref[...] + xext[halo:] * cw_ref[CONV_WIDTH - 1:CONV_WIDTH, :]
    for k in range(CONV_WIDTH - 1):
        delayed = pltpu.roll(xext, CONV_WIDTH - 1 - k, axis=0)
        conv = conv + delayed[halo:] * cw_ref[k:k + 1, :]
    xext_scr[0:halo, :] = xext_scr[q:q + halo, :]
    xbc = _silu(conv)
    xs = xbc[:, :D_INNER]
    b_all = xbc[:, D_INNER:D_INNER + SSM_GROUPS * D_STATE]
    c_all = xbc[:, D_INNER + SSM_GROUPS * D_STATE:]

    v = dt_ref[0, rows, :] + dtb_ref[...]
    dt = jnp.maximum(v, 0.0) + jnp.log1p(jnp.exp(-jnp.abs(v)))
    row = lax.broadcasted_iota(I32, (q, LANES), 0)
    if seq_valid is not None:
        dt = jnp.where(row + blk * q < seq_valid, dt, 0.0)
    a_neg = -jnp.exp(alog_ref[...])
    dta = dt * a_neg

    ti = lax.broadcasted_iota(I32, (q, q), 0)
    si = lax.broadcasted_iota(I32, (q, q), 1)
    causal = ti >= si
    tri = jnp.where(causal, 1.0, 0.0).astype(BF16)
    a_cs = sum(jnp.dot(tri, part, preferred_element_type=F32) for part in _split3(dta))
    a_last = a_cs[q - 1:q, :]
    a_cs_t = a_cs.T

    stacked = jnp.concatenate([dt, jnp.exp(a_cs), jnp.exp(a_last - a_cs),
                               jnp.broadcast_to(jnp.exp(a_last), (SUBLANES, LANES))], axis=0)
    s_hi = stacked.astype(BF16)
    s_lo = (stacked - s_hi.astype(F32)).astype(BF16)
    expand = expand_ref[...]
    expanded = (jnp.dot(s_hi, expand, preferred_element_type=F32)
                + jnp.dot(s_lo, expand, preferred_element_type=F32))
    dt_e = expanded[0:q]
    decay_in_e = expanded[q:2 * q]
    decay_out_e = expanded[2 * q:3 * q]
    decay_blk_e = expanded[3 * q:3 * q + 1]

    xdt = xs * dt_e
    xdt_bf = xdt.astype(BF16)
    wx = (xdt * decay_out_e).astype(BF16)
    low_half = lax.broadcasted_iota(I32, (q, LANES), 1) < SSM_HEAD_DIM

    y_groups = []
    for g in range(SSM_GROUPS):
        bg = b_all[:, g * D_STATE:(g + 1) * D_STATE]
        cg = c_all[:, g * D_STATE:(g + 1) * D_STATE].astype(BF16)
        bg_bf = bg.astype(BF16)
        cbm = lax.dot_general(cg, bg_bf, (((1,), (1,)), ((), ())), preferred_element_type=F32)
        cols = slice(g * GROUP_COLS, (g + 1) * GROUP_COLS)
        h_g = h_scr[g]
        y_g = jnp.dot(cg, h_g.astype(BF16), preferred_element_type=F32) * decay_in_e[:, cols]
        pair_out = []
        for pr in range(HEADS_PER_GROUP // 2):
            parts = []
            for par in range(2):
                h = g * HEADS_PER_GROUP + 2 * pr + par
                seg = (jnp.broadcast_to(a_cs[:, h:h + 1], (q, q))
                       - jnp.broadcast_to(a_cs_t[h:h + 1, :], (q, q)))
                m = (cbm * jnp.exp(jnp.where(causal, seg, -jnp.inf))).astype(BF16)
                pcols = slice((h // 2) * LANES, (h // 2 + 1) * LANES)
                parts.append(jnp.dot(m, xdt_bf[:, pcols], preferred_element_type=F32))
            pair_out.append(jnp.where(low_half, parts[0], parts[1]))
        y_groups.append(y_g + jnp.concatenate(pair_out, axis=1))
        h_scr[g] = (h_g * decay_blk_e[:, cols]
                    + jnp.dot(bg.T.astype(BF16), wx[:, cols], preferred_element_type=F32))

    y = jnp.concatenate(y_groups, axis=1) + dskip_ref[...] * xs
    yz = y * _silu(z_ref[0, rows, :])
    yn_ref[0, rows, :] = (yz * _rms_scale(yz) * gn_ref[...]).astype(yn_ref.dtype)


def _ssd_kernel(xbc_ref, dt_ref, z_ref, cbuf_ref, h0_ref, cw_ref, cb_ref, dtb_ref, alog_ref,
                dskip_ref, gn_ref, expand_ref, yn_ref, hfin_ref, h_scr, xext_scr, *, seq_valid, n_sub):
    c = pl.program_id(1)

    @pl.when(c == 0)
    def _():
        h_scr[...] = h0_ref[0]
        xext_scr[0:SUBLANES, :] = cbuf_ref[0]

    for sub in range(n_sub):
        _ssd_block(slice(sub * SSD_BLOCK, (sub + 1) * SSD_BLOCK), c * n_sub + sub, xbc_ref, dt_ref, z_ref,
                   cw_ref, cb_ref, dtb_ref, alog_ref, dskip_ref, gn_ref, expand_ref, yn_ref, h_scr, xext_scr,
                   seq_valid)

    @pl.when(c == pl.num_programs(1) - 1)
    def _():
        hfin_ref[0] = h_scr[...]


def _state_to_kernel_layout(h):
    b = h.shape[0]
    h = h.reshape(b, SSM_GROUPS, HEADS_PER_GROUP, SSM_HEAD_DIM, D_STATE)
    return h.transpose(0, 1, 4, 2, 3).reshape(b, SSM_GROUPS, D_STATE, GROUP_COLS)


def _state_from_kernel_layout(h):
    b = h.shape[0]
    h = h.reshape(b, SSM_GROUPS, D_STATE, HEADS_PER_GROUP, SSM_HEAD_DIM)
    return h.transpose(0, 1, 3, 4, 2).reshape(b, N_SSM_HEADS, SSM_HEAD_DIM, D_STATE)


def _pad_lanes(v):
    return jnp.pad(v.astype(F32), (0, LANES - v.shape[0])).reshape(1, LANES)


def ssd_mixer(xbc, dt_raw, z, conv_buf, h0, conv_w, conv_b, dt_bias, a_log, d_skip, gnorm, seq_valid):
    bsz, seq_pad, _ = xbc.shape
    q = SSD_BLOCK
    assert seq_pad % q == 0
    n_blocks = seq_pad // q
    n_sub = 4 if n_blocks % 4 == 0 else (2 if n_blocks % 2 == 0 else 1)
    rows_per_step = n_sub * q
    cbuf = jnp.pad(conv_buf, ((0, 0), (SUBLANES - (CONV_WIDTH - 1), 0), (0, 0)))
    expand = (jnp.arange(D_INNER)[None, :] // SSM_HEAD_DIM == jnp.arange(LANES)[:, None]).astype(BF16)
    d_e = jnp.repeat(d_skip.astype(F32), SSM_HEAD_DIM).reshape(1, D_INNER)
    state_spec = pl.BlockSpec((1, SSM_GROUPS, D_STATE, GROUP_COLS), lambda b, c: (b, 0, 0, 0))

    def const(shape):
        return pl.BlockSpec(shape, lambda b, c: (0,) * len(shape))

    yn, h_fin = pl.pallas_call(
        functools.partial(_ssd_kernel, seq_valid=seq_valid, n_sub=n_sub),
        grid=(bsz, seq_pad // rows_per_step),
        in_specs=[pl.BlockSpec((1, rows_per_step, CONV_DIM), lambda b, c: (b, c, 0)),
                  pl.BlockSpec((1, rows_per_step, LANES), lambda b, c: (b, c, 0)),
                  pl.BlockSpec((1, rows_per_step, D_INNER), lambda b, c: (b, c, 0)),
                  pl.BlockSpec((1, SUBLANES, CONV_DIM), lambda b, c: (b, 0, 0)),
                  state_spec,
                  const((CONV_WIDTH, CONV_DIM)), const((1, CONV_DIM)), const((1, LANES)), const((1, LANES)),
                  const((1, D_INNER)), const((1, D_INNER)), const((LANES, D_INNER))],
        out_specs=[pl.BlockSpec((1, rows_per_step, D_INNER), lambda b, c: (b, c, 0)), state_spec],
        out_shape=[jax.ShapeDtypeStruct((bsz, seq_pad, D_INNER), BF16),
                   jax.ShapeDtypeStruct((bsz, SSM_GROUPS, D_STATE, GROUP_COLS), F32)],
        scratch_shapes=[pltpu.VMEM((SSM_GROUPS, D_STATE, GROUP_COLS), F32),
                        pltpu.VMEM((q + SUBLANES, CONV_DIM), F32)],
        compiler_params=_cparams("parallel", "arbitrary"),
        name="ssd_mixer",
    )(xbc, dt_raw, z, cbuf, _state_to_kernel_layout(h0.astype(F32)), conv_w.astype(F32),
      conv_b.astype(F32).reshape(1, CONV_DIM), _pad_lanes(dt_bias), _pad_lanes(a_log), d_e,
      gnorm.astype(F32).reshape(1, D_INNER), expand)
    return yn, _state_from_kernel_layout(h_fin)


_ALIBI_SLOPES = (2.0 ** (-8.0 * np.arange(1, N_Q_HEADS + 1, dtype=np.float32) / N_Q_HEADS)).astype(np.float32)


def _attn_bias(lbq, chunk, past_rows):
    w = WINDOW + lbq
    qi = np.arange(lbq)[:, None]
    lo = (qi // chunk) * chunk
    variants = []
    for lead in (past_rows, WINDOW):
        si = np.arange(w)[None, :] + (WINDOW - lead)
        dist = np.abs(WINDOW + qi - si).astype(np.float32)
        valid = (si >= lo) & (si < lo + WINDOW + chunk)
        per_head = [np.where(valid, -(_ALIBI_SLOPES[h] * dist), np.float32(NEG_INF)) for h in range(N_Q_HEADS)]
        variants.append(np.stack([np.concatenate([per_head[4 * kv + par], per_head[4 * kv + 2 + par]], axis=0)
                                  for kv in range(N_KV_HEADS) for par in range(2)]))
    return np.stack(variants).astype(np.float32)


def _attn_kernel(sink_ref, q_ref, kv_ref, bias_ref, o_ref, *, lbq, n_sub, past_rows):
    for sub in range(n_sub):
        _attn_block(slice(sub * lbq, (sub + 1) * lbq), pl.program_id(1) * n_sub + sub,
                    sink_ref, q_ref, kv_ref, bias_ref, o_ref, lbq, past_rows)


def _attn_block(rows, n, sink_ref, q_ref, kv_ref, bias_ref, o_ref, lbq, past_rows):
    w = WINDOW + lbq
    base = pl.multiple_of(jnp.maximum(n * lbq + (past_rows - WINDOW), 0), SUBLANES)
    variant = jnp.minimum(n, 1)
    kband = kv_ref[0, pl.ds(base, w), :KV_DIM]
    vband = kv_ref[0, pl.ds(base, w), KV_DIM:]
    first_pair = lax.broadcasted_iota(I32, (2 * lbq, 1), 0) < lbq
    lane = lax.broadcasted_iota(I32, (w, LANES), 1)
    low_half = lane < ATTN_HEAD_DIM

    for kv in range(N_KV_HEADS):
        pcols = slice((kv // 2) * LANES, (kv // 2 + 1) * LANES)
        kpair, vpair = kband[:, pcols], vband[:, pcols]
        kswap = pltpu.roll(kpair, ATTN_HEAD_DIM, axis=1)
        vswap = pltpu.roll(vpair, ATTN_HEAD_DIM, axis=1)
        in_low = kv % 2 == 0
        k_lo = jnp.where(low_half, kpair if in_low else kswap, 0.0).astype(BF16)
        k_hi = jnp.where(low_half, 0.0, kswap if in_low else kpair).astype(BF16)
        v_lo = jnp.where(low_half, vpair if in_low else vswap, 0.0).astype(BF16)
        v_hi = jnp.where(low_half, 0.0, vswap if in_low else vpair).astype(BF16)
        qs = q_ref[2 * kv:2 * kv + 2, 0, rows, :].reshape(2 * lbq, LANES)
        out = None
        for par, (kx, vx) in enumerate(((k_lo, v_lo), (k_hi, v_hi))):
            h0, h1 = 4 * kv + par, 4 * kv + 2 + par
            sink = jnp.where(first_pair, sink_ref[h0], sink_ref[h1])
            s = lax.dot_general(qs, kx, (((1,), (1,)), ((), ())), preferred_element_type=F32)
            s = s + bias_ref[variant, 2 * kv + par]
            m = jnp.maximum(jnp.max(s, axis=-1, keepdims=True), sink)
            p = jnp.exp(s - m)
            denom = jnp.sum(p, axis=-1, keepdims=True) + jnp.exp(sink - m)
            pn = (p / denom).astype(BF16)
            part = jnp.dot(pn, vx, preferred_element_type=F32)
            out = part if out is None else out + part
        o_ref[0, rows, (2 * kv) * LANES:(2 * kv + 1) * LANES] = out[:lbq].astype(o_ref.dtype)
        o_ref[0, rows, (2 * kv + 1) * LANES:(2 * kv + 2) * LANES] = out[lbq:].astype(o_ref.dtype)


def window_attention(q_pairs, kv_all, sinks, *, lbq, chunk):
    n_pairs, bsz, seq, _ = q_pairs.shape
    rows = kv_all.shape[1]
    past_rows = rows - seq
    assert past_rows in (0, WINDOW) and rows >= WINDOW + lbq
    bias = jnp.asarray(_attn_bias(lbq, chunk, past_rows))
    n_blocks = seq // lbq
    n_sub = 4 if n_blocks % 4 == 0 else (2 if n_blocks % 2 == 0 else 1)
    step = n_sub * lbq
    return pl.pallas_call(
        functools.partial(_attn_kernel, lbq=lbq, n_sub=n_sub, past_rows=past_rows),
        grid_spec=pltpu.PrefetchScalarGridSpec(
            num_scalar_prefetch=1,
            grid=(bsz, seq // step),
            in_specs=[pl.BlockSpec((n_pairs, 1, step, LANES), lambda b, n, s: (0, b, n, 0)),
                      pl.BlockSpec((1, rows, 2 * KV_DIM), lambda b, n, s: (b, 0, 0)),
                      _resident(bias.shape)],
            out_specs=pl.BlockSpec((1, step, D_MODEL), lambda b, n, s: (b, n, 0))),
        out_shape=jax.ShapeDtypeStruct((bsz, seq, D_MODEL), BF16),
        compiler_params=_cparams("parallel", "arbitrary"),
        name="window_attention",
    )(sinks.astype(F32), q_pairs, kv_all, bias)


def _trunk(x, conv_buf, ssm_state, k_past, v_past, past_valid, p):
    bsz, seq, d = x.shape
    t = bsz * seq
    x2 = x.reshape(t, d)

    seq_pad = -(-seq // SSD_BLOCK) * SSD_BLOCK
    x_in = x2 if seq_pad == seq else jnp.pad(x, ((0, 0), (0, seq_pad - seq), (0, 0))).reshape(bsz * seq_pad, d)
    z, xbc, dt_raw = in_proj(x_in, p['norm_mix'][0], p['w_z'], p['w_xbc'], p['w_dt'])
    xbc3 = xbc.reshape(bsz, seq_pad, CONV_DIM)
    new_conv = jnp.concatenate([conv_buf.astype(F32), xbc3[:, :seq]], axis=1)[:, seq:]
    yn, new_ssm = ssd_mixer(xbc3, dt_raw.reshape(bsz, seq_pad, LANES), z.reshape(bsz, seq_pad, D_INNER),
                            conv_buf.astype(F32), ssm_state, p['ssm_conv_w'], p['ssm_conv_b'],
                            p['ssm_dt_bias'], p['ssm_a_log'], p['ssm_d'], p['ssm_norm'],
                            None if seq_pad == seq else seq)
    yn = yn[:, :seq]

    h, kv, q_pairs = layer0_tail(yn.reshape(t, D_INNER), p['ssm_out_w'], x2, p['norm_ffn'][0],
                                 p['ffn_w_gate_up'], p['ffn_w_down'], p['kv_norm'], p['norm_mix'][1],
                                 p['w_kv'], p['w_q'])
    kv = kv.reshape(bsz, seq, 2 * KV_DIM)
    k_new, v_new = kv[..., :KV_DIM], kv[..., KV_DIM:]
    if past_valid:
        past = jnp.concatenate([k_past.reshape(bsz, WINDOW, KV_DIM), v_past.reshape(bsz, WINDOW, KV_DIM)], axis=2)
        kv_all = jnp.concatenate([past.astype(F32), kv], axis=1)
    else:
        kv_all = kv

    chunk = min(ATTN_CHUNK, seq)
    lbq = 2 * chunk if seq % (2 * chunk) == 0 else chunk
    q_pairs = q_pairs.reshape(D_MODEL // LANES, bsz, seq, LANES)
    o = window_attention(q_pairs, kv_all, p['attn_sinks'], lbq=lbq, chunk=chunk)

    y = attn_out_moe_final_norm(o.reshape(t, D_MODEL), p['w_o'], h, p['norm_ffn'][1], p['moe_router'],
                                p['moe_w_gate_up'], p['moe_w_down'], p['final_norm'])

    k4 = k_new.reshape(bsz, seq, N_KV_HEADS, ATTN_HEAD_DIM)
    v4 = v_new.reshape(bsz, seq, N_KV_HEADS, ATTN_HEAD_DIM)
    return y.reshape(bsz, seq, d), new_conv[None], new_ssm[None], k4, v4


def kernel(x_prompt, x_sample, state_conv, state_ssm, cache_k, cache_v, norm_mix, norm_ffn, ssm_in_w, ssm_conv_w,
           ssm_conv_b, ssm_dt_bias, ssm_a_log, ssm_d, ssm_norm, ssm_out_w, kv_norm, w_kv, w_q, attn_sinks, w_o,
           ffn_w_gate_up, ffn_w_down, moe_router, moe_w_gate_up, moe_w_down, final_norm):
    in_w = ssm_in_w[0]
    n_dt = N_SSM_HEADS
    p = {
        'norm_mix': norm_mix, 'norm_ffn': norm_ffn,
        'w_z': in_w[:, :D_INNER].astype(BF16),
        'w_xbc': in_w[:, D_INNER:D_INNER + CONV_DIM].astype(BF16),
        'w_dt': jnp.pad(in_w[:, D_INNER + CONV_DIM:], ((0, 0), (0, LANES - n_dt))).astype(BF16),
        'ssm_conv_w': ssm_conv_w[0], 'ssm_conv_b': ssm_conv_b[0], 'ssm_dt_bias': ssm_dt_bias[0],
        'ssm_a_log': ssm_a_log[0], 'ssm_d': ssm_d[0], 'ssm_norm': ssm_norm[0],
        'ssm_out_w': ssm_out_w[0].astype(BF16),
        'kv_norm': kv_norm, 'w_kv': w_kv.astype(BF16), 'w_q': w_q[0].astype(BF16),
        'attn_sinks': attn_sinks[0], 'w_o': w_o[0].astype(BF16),
        'ffn_w_gate_up': ffn_w_gate_up.astype(BF16), 'ffn_w_down': ffn_w_down.astype(BF16),
        'moe_router': jnp.pad(moe_router[0].astype(F32), ((0, 0), (0, LANES - N_EXPERTS))),
        'moe_w_gate_up': moe_w_gate_up[0].astype(BF16), 'moe_w_down': moe_w_down[0].astype(BF16),
        'final_norm': final_norm,
    }
    bsz, seq_p = x_prompt.shape[:2]
    dt = x_prompt.dtype
    zero_conv = jnp.zeros((bsz, CONV_WIDTH - 1, CONV_DIM), dt)
    zero_ssm = jnp.zeros((bsz, N_SSM_HEADS, SSM_HEAD_DIM, D_STATE), dt)
    zero_kv = jnp.zeros((bsz, WINDOW, N_KV_HEADS, ATTN_HEAD_DIM), dt)
    y_p, conv_p, ssm_p, k_p, v_p = _trunk(x_prompt, zero_conv, zero_ssm, zero_kv, zero_kv, False, p)
    keep = min(WINDOW, seq_p)
    y_s, conv_s, ssm_s, k_s, v_s = _trunk(x_sample, state_conv[0], state_ssm[0], cache_k, cache_v, True, p)
    return (y_p, y_s, conv_p, ssm_p, k_p[:, seq_p - keep:], v_p[:, seq_p - keep:], conv_s, ssm_s, k_s, v_s)
```

```python
import functools
import math

import jax
import jax.numpy as jnp
import numpy as np
from jax import lax
from jax.experimental import pallas as pl
from jax.experimental.pallas import tpu as pltpu

F32 = jnp.float32
BF16 = jnp.bfloat16
I32 = jnp.int32

D_MODEL = 1024
EPS = 1e-6
D_INNER = 2048
SSM_HEAD_DIM = 64
N_SSM_HEADS = 32
SSM_GROUPS = 4
HEADS_PER_GROUP = N_SSM_HEADS // SSM_GROUPS
D_STATE = 128
CONV_WIDTH = 4
CONV_DIM = D_INNER + 2 * SSM_GROUPS * D_STATE
ATTN_HEAD_DIM = 64
N_Q_HEADS = 16
N_KV_HEADS = 4
KV_DIM = N_KV_HEADS * ATTN_HEAD_DIM
WINDOW = 128
ATTN_CHUNK = 64
ATTN_SCALE = 1.0 / math.sqrt(ATTN_HEAD_DIM)
NEG_INF = -1e30
D_FF = 2816
N_EXPERTS = 8
TOP_K = 2

LANES = 128
SUBLANES = 8
VMEM_LIMIT_BYTES = 56 * 1024 * 1024
RUN_ALIGN = SUBLANES

SSD_BLOCK = 128
GROUP_COLS = HEADS_PER_GROUP * SSM_HEAD_DIM
FF_CHUNK = 512
EXPERT_ROW_TILE = 512
SMALL_EXPERT_ROW_TILE = 128


def _cparams(*sem):
    return pltpu.CompilerParams(dimension_semantics=sem, vmem_limit_bytes=VMEM_LIMIT_BYTES)


def _row_tile(t, pref=512):
    return pref if t % pref == 0 else t


def _resident(shape):
    return pl.BlockSpec(shape, lambda *_: (0,) * len(shape), pipeline_mode=pl.Buffered(1))


def _rms_scale(x):
    return lax.rsqrt(jnp.mean(x * x, axis=-1, keepdims=True) + EPS)


def _silu(x):
    half = 0.5 * x
    return half + half * jnp.tanh(half)


def _in_proj_kernel(x_ref, g_ref, wz_ref, wx_ref, wd_ref, z_ref, xbc_ref, dt_ref):
    x = x_ref[...]
    xn = (x * _rms_scale(x) * g_ref[...]).astype(BF16)
    z_ref[...] = jnp.dot(xn, wz_ref[...], preferred_element_type=F32)
    xbc_ref[...] = jnp.dot(xn, wx_ref[...], preferred_element_type=F32)
    dt_ref[...] = jnp.dot(xn, wd_ref[...], preferred_element_type=F32)


def in_proj(x, g, w_z, w_xbc, w_dt):
    t, k = x.shape
    tm = _row_tile(t)
    ws = (w_z, w_xbc, w_dt)
    return pl.pallas_call(
        _in_proj_kernel,
        grid=(t // tm,),
        in_specs=[pl.BlockSpec((tm, k), lambda i: (i, 0)), pl.BlockSpec((1, k), lambda i: (0, 0))]
        + [_resident(w.shape) for w in ws],
        out_specs=[pl.BlockSpec((tm, w.shape[1]), lambda i: (i, 0)) for w in ws],
        out_shape=[jax.ShapeDtypeStruct((t, w.shape[1]), F32) for w in ws],
        compiler_params=_cparams("parallel"),
        name="in_proj",
    )(x, g.reshape(1, k), *ws)


def _swiglu(xb, wgu_ref, wd_ref):
    acc = None
    for c0 in range(0, D_FF, FF_CHUNK):
        cw = min(FF_CHUNK, D_FF - c0)
        gg = jnp.dot(xb, wgu_ref[0, :, c0:c0 + cw], preferred_element_type=F32)
        uu = jnp.dot(xb, wgu_ref[0, :, D_FF + c0:D_FF + c0 + cw], preferred_element_type=F32)
        act = (_silu(gg) * uu).astype(BF16)
        part = jnp.dot(act, wd_ref[0, c0:c0 + cw, :], preferred_element_type=F32)
        acc = part if acc is None else acc + part
    return acc


def _layer0_tail_kernel(a_ref, wo_ref, r_ref, gf_ref, wgu_ref, wd_ref, gkv_ref, gq_ref, wkv_ref, wq_ref,
                        h_ref, kv_ref, q_ref):
    h1 = r_ref[...] + jnp.dot(a_ref[...], wo_ref[...], preferred_element_type=F32)
    xb = (h1 * _rms_scale(h1) * gf_ref[...]).astype(BF16)
    h2 = h1 + _swiglu(xb, wgu_ref, wd_ref)
    h_ref[...] = h2
    xs = h2 * _rms_scale(h2)
    kv_ref[...] = jnp.dot((xs * gkv_ref[...]).astype(BF16), wkv_ref[...], preferred_element_type=F32)
    q = jnp.dot((xs * gq_ref[...]).astype(BF16), wq_ref[...], preferred_element_type=F32)
    q = (q * ATTN_SCALE).astype(q_ref.dtype)
    for p in range(q_ref.shape[0]):
        q_ref[p] = q[:, p * LANES:(p + 1) * LANES]


def layer0_tail(a, w_out, r, g_ffn, w_gu, w_down, g_kv, g_q, w_kv, w_q):
    t, k = a.shape
    d = w_out.shape[1]
    tm = _row_tile(t)
    n_pairs = w_q.shape[1] // LANES

    def gain():
        return pl.BlockSpec((1, d), lambda i: (0, 0))

    return pl.pallas_call(
        _layer0_tail_kernel,
        grid=(t // tm,),
        in_specs=[pl.BlockSpec((tm, k), lambda i: (i, 0)), _resident(w_out.shape),
                  pl.BlockSpec((tm, d), lambda i: (i, 0)), gain(),
                  _resident(w_gu.shape), _resident(w_down.shape), gain(), gain(),
                  _resident(w_kv.shape), _resident(w_q.shape)],
        out_specs=[pl.BlockSpec((tm, d), lambda i: (i, 0)),
                   pl.BlockSpec((tm, w_kv.shape[1]), lambda i: (i, 0)),
                   pl.BlockSpec((n_pairs, tm, LANES), lambda i: (0, i, 0))],
        out_shape=[jax.ShapeDtypeStruct((t, d), F32),
                   jax.ShapeDtypeStruct((t, w_kv.shape[1]), F32),
                   jax.ShapeDtypeStruct((n_pairs, t, LANES), BF16)],
        compiler_params=_cparams("parallel"),
        name="layer0_tail",
    )(a, w_out, r, g_ffn.reshape(1, d), w_gu, w_down, g_kv.reshape(1, d), g_q.reshape(1, d), w_kv, w_q)


def _expert_ffn_kernel(te_ref, na_ref, x_ref, wgu_ref, wd_ref, o_ref):
    active = pl.program_id(0) < na_ref[0]

    @pl.when(active)
    def _():
        o_ref[...] = _swiglu(x_ref[...].astype(BF16), wgu_ref, wd_ref)

    @pl.when(jnp.logical_not(active))
    def _():
        o_ref[...] = jnp.zeros_like(o_ref)


def expert_ffn(xs, tile_expert, n_active, w_gu, w_down, tm):
    rows, d = xs.shape
    return pl.pallas_call(
        _expert_ffn_kernel,
        grid_spec=pltpu.PrefetchScalarGridSpec(
            num_scalar_prefetch=2,
            grid=(rows // tm,),
            in_specs=[pl.BlockSpec((tm, d), lambda i, te, na: (jnp.minimum(i, na[0] - 1), 0)),
                      pl.BlockSpec((1, d, 2 * D_FF), lambda i, te, na: (te[i], 0, 0)),
                      pl.BlockSpec((1, D_FF, d), lambda i, te, na: (te[i], 0, 0))],
            out_specs=pl.BlockSpec((tm, d), lambda i, te, na: (i, 0))),
        out_shape=jax.ShapeDtypeStruct((rows, d), F32),
        compiler_params=_cparams("arbitrary"),
        name="expert_ffn",
    )(tile_expert, n_active, xs, w_gu, w_down)


def _router_kernel(a_ref, wo_ref, r_ref, g_ref, wr_ref, h_ref, gate_ref, slot_ref, meta_ref, cnt_ref):
    x = r_ref[...] + jnp.dot(a_ref[...], wo_ref[...], preferred_element_type=F32)
    h_ref[...] = x
    tm = x.shape[0]
    xn = x * _rms_scale(x) * g_ref[...]
    x_hi = xn.astype(BF16)
    x_lo = (xn - x_hi.astype(F32)).astype(BF16)
    prod = jnp.dot(jnp.concatenate([x_hi, x_lo], axis=0), wr_ref[...], preferred_element_type=F32)
    hi_rows, lo_rows = prod[:tm], prod[tm:]
    logits = hi_rows + pltpu.roll(hi_rows, LANES - N_EXPERTS, axis=1) + lo_rows
    lane = lax.broadcasted_iota(I32, logits.shape, 1)
    logits = jnp.where(lane < N_EXPERTS, logits, -jnp.inf)
    m1 = jnp.max(logits, axis=-1, keepdims=True)
    i1 = jnp.min(jnp.where(logits == m1, lane, LANES), axis=-1, keepdims=True)
    rest = jnp.where(lane == i1, -jnp.inf, logits)
    m2 = jnp.max(rest, axis=-1, keepdims=True)
    i2 = jnp.min(jnp.where(rest == m2, lane, LANES), axis=-1, keepdims=True)
    e2 = jnp.exp(m2 - m1)
    w1 = 1.0 / (1.0 + e2)
    w2 = e2 / (1.0 + e2)
    gate_ref[...] = jnp.where(lane == 0, w1, jnp.where(lane == 1, w2, 0.0))
    pick0 = lane == i1
    pick1 = lane == i2
    chosen = jnp.where(jnp.logical_or(pick0, pick1), 1.0, 0.0)

    @pl.when(pl.program_id(0) == 0)
    def _():
        cnt_ref[...] = jnp.zeros_like(cnt_ref)

    ri = lax.broadcasted_iota(I32, (tm, tm), 0)
    ci = lax.broadcasted_iota(I32, (tm, tm), 1)
    before = jnp.where(ri > ci, 1.0, 0.0).astype(BF16)
    rank = jnp.dot(before, chosen.astype(BF16), preferred_element_type=F32)
    n_pad = jnp.ceil(jnp.sum(chosen, axis=0, keepdims=True) * (1.0 / RUN_ALIGN)) * RUN_ALIGN
    ei = lax.broadcasted_iota(I32, (LANES, LANES), 0)
    ej = lax.broadcasted_iota(I32, (LANES, LANES), 1)
    earlier = jnp.where(ei < ej, 1.0, 0.0).astype(BF16)
    n_pad8 = jnp.broadcast_to(n_pad, (SUBLANES, LANES))
    local = jnp.dot(n_pad8.astype(BF16), earlier, preferred_element_type=F32)[0:1, :]
    slot = local + rank
    s0 = jnp.sum(jnp.where(pick0, slot, 0.0), axis=-1, keepdims=True)
    s1 = jnp.sum(jnp.where(pick1, slot, 0.0), axis=-1, keepdims=True)
    slot_ref[...] = jnp.where(lane == 0, s0, jnp.where(lane == 1, s1, 0.0))
    row = lax.broadcasted_iota(I32, (SUBLANES, LANES), 0)
    so_far = cnt_ref[0:1, :]
    meta = jnp.where(row == 0, so_far, jnp.where(row == 1, local, jnp.where(row == 2, n_pad, 0.0)))
    meta_ref[...] = meta.astype(I32)
    cnt_ref[...] += n_pad8


def out_proj_router(a, w_o, r, g, w_router):
    t, k = a.shape
    d = w_o.shape[1]
    tm = _row_tile(t)
    w_hi = w_router.astype(BF16)
    w_lo = (w_router - w_hi.astype(F32)).astype(BF16)
    w_packed = jnp.concatenate([w_hi[:, :N_EXPERTS], w_lo[:, :N_EXPERTS],
                                jnp.zeros((d, LANES - 2 * N_EXPERTS), BF16)], axis=1)
    return pl.pallas_call(
        _router_kernel,
        grid=(t // tm,),
        in_specs=[pl.BlockSpec((tm, k), lambda i: (i, 0)),
                  _resident(w_o.shape),
                  pl.BlockSpec((tm, d), lambda i: (i, 0)),
                  pl.BlockSpec((1, d), lambda i: (0, 0)),
                  _resident((d, LANES))],
        out_specs=[pl.BlockSpec((tm, d), lambda i: (i, 0)),
                   pl.BlockSpec((tm, LANES), lambda i: (i, 0)),
                   pl.BlockSpec((tm, LANES), lambda i: (i, 0)),
                   pl.BlockSpec((SUBLANES, LANES), lambda i: (i, 0)),
                   pl.BlockSpec((SUBLANES, LANES), lambda i: (0, 0))],
        out_shape=[jax.ShapeDtypeStruct((t, d), F32),
                   jax.ShapeDtypeStruct((t, LANES), F32),
                   jax.ShapeDtypeStruct((t, LANES), F32),
                   jax.ShapeDtypeStruct((SUBLANES * (t // tm), LANES), I32),
                   jax.ShapeDtypeStruct((SUBLANES, LANES), F32)],
        compiler_params=_cparams("arbitrary"),
        name="out_proj_router",
    )(a, w_o, r, g.reshape(1, d), w_packed)


META_ROWS = 3


def _staging_rows(tt):
    need = TOP_K * tt + N_EXPERTS * (RUN_ALIGN - 1)
    return -(-need // LANES) * LANES


def _run_copies(meta_ref, group_ref, tile, staging, sorted_hbm, sem, to_sorted):
    first = tile * (META_ROWS * N_EXPERTS)
    out = []
    for e in range(N_EXPERTS):
        sorted_start = pl.multiple_of(group_ref[e] + meta_ref[first + e], RUN_ALIGN)
        local_start = pl.multiple_of(meta_ref[first + N_EXPERTS + e], RUN_ALIGN)
        n = pl.multiple_of(meta_ref[first + 2 * N_EXPERTS + e], RUN_ALIGN)
        a, b = staging.at[pl.ds(local_start, n)], sorted_hbm.at[pl.ds(sorted_start, n)]
        out.append((n, pltpu.make_async_copy(a, b, sem) if to_sorted else pltpu.make_async_copy(b, a, sem)))
    return out


def _start(copies):
    for n, copy in copies:
        pl.when(n > 0)(copy.start)


def _wait(copies):
    for n, copy in copies:
        pl.when(n > 0)(copy.wait)


def _start_then_wait(copies):
    _start(copies)
    _wait(copies)


def _scatter_kernel(meta_ref, group_ref, tail_ref, x_ref, g_ref, slot_ref, xs_hbm, stage_scr, zero_scr, sem):
    i = pl.program_id(0)
    last = pl.num_programs(0) - 1
    buf = i % 2
    x = x_ref[...]
    xn = (x * _rms_scale(x) * g_ref[...]).astype(BF16)
    tt, rt = x.shape[0], stage_scr.shape[1]
    slot_t = slot_ref[...].T.astype(I32)
    r = lax.broadcasted_iota(I32, (rt, tt), 0)
    place = jnp.where(jnp.logical_or(r == slot_t[0:1, :], r == slot_t[1:2, :]), 1.0, 0.0).astype(BF16)
    stage_scr[buf] = jnp.dot(place, xn, preferred_element_type=F32)
    mine = _run_copies(meta_ref, group_ref, i, stage_scr.at[buf], xs_hbm, sem.at[buf], to_sorted=True)
    _start(mine)

    @pl.when(i > 0)
    def _():
        _wait(_run_copies(meta_ref, group_ref, i - 1, stage_scr.at[1 - buf], xs_hbm, sem.at[1 - buf],
                          to_sorted=True))

    @pl.when(i == last)
    def _():
        _wait(mine)
        zero_scr[...] = jnp.zeros_like(zero_scr)
        tails = []
        for e in range(N_EXPERTS):
            start = pl.multiple_of(tail_ref[e], RUN_ALIGN)
            n = pl.multiple_of(tail_ref[N_EXPERTS + e], RUN_ALIGN)
            tails.append((n, pltpu.make_async_copy(zero_scr.at[pl.ds(0, n)], xs_hbm.at[pl.ds(start, n)], sem.at[0])))
        _start_then_wait(tails)

        tm = zero_scr.shape[0]

        def clear_tile(j, carry):
            copy = pltpu.make_async_copy(zero_scr, xs_hbm.at[pl.ds(pl.multiple_of(j * tm, tm), tm)], sem.at[0])
            copy.start()
            copy.wait()
            return carry

        lax.fori_loop(tail_ref[2 * N_EXPERTS], xs_hbm.shape[0] // tm, clear_tile, 0)


def scatter_rows(x, g, slot, meta, group_starts, tails, n_rows, tm):
    t, d = x.shape
    tt = _row_tile(t)
    return pl.pallas_call(
        _scatter_kernel,
        grid_spec=pltpu.PrefetchScalarGridSpec(
            num_scalar_prefetch=3,
            grid=(t // tt,),
            in_specs=[pl.BlockSpec((tt, d), lambda i, *_: (i, 0)),
                      pl.BlockSpec((1, d), lambda i, *_: (0, 0)),
                      pl.BlockSpec((tt, LANES), lambda i, *_: (i, 0))],
            out_specs=pl.BlockSpec(memory_space=pl.ANY),
            scratch_shapes=[pltpu.VMEM((2, _staging_rows(tt), d), F32), pltpu.VMEM((tm, d), F32),
                            pltpu.SemaphoreType.DMA((2,))]),
        out_shape=jax.ShapeDtypeStruct((n_rows, d), F32),
        compiler_params=_cparams("arbitrary"),
        name="scatter_rows",
    )(meta, group_starts, tails, x, g.reshape(1, d), slot)


def _combine_kernel(meta_ref, group_ref, h_ref, gate_ref, slot_ref, fg_ref, ys_hbm, o_ref, stage_scr, sem):
    i = pl.program_id(0)
    buf = i % 2

    def fetch(tile, b):
        return _run_copies(meta_ref, group_ref, tile, stage_scr.at[b], ys_hbm, sem.at[b], to_sorted=False)

    @pl.when(i == 0)
    def _():
        stage_scr[...] = jnp.zeros_like(stage_scr)
        _start(fetch(0, 0))

    @pl.when(i + 1 < pl.num_programs(0))
    def _():
        _start(fetch(i + 1, 1 - buf))

    tt, rt = h_ref.shape[0], stage_scr.shape[1]
    slot = slot_ref[...].astype(I32)
    gate = gate_ref[...]
    c = lax.broadcasted_iota(I32, (tt, rt), 1)
    weight = (jnp.where(c == slot[:, 0:1], gate[:, 0:1], 0.0)
              + jnp.where(c == slot[:, 1:2], gate[:, 1:2], 0.0))
    _wait(fetch(i, buf))
    moe = jnp.dot(weight.astype(BF16), stage_scr[buf].astype(BF16), preferred_element_type=F32)
    y = h_ref[...] + moe
    o_ref[...] = y * _rms_scale(y) * fg_ref[...]


def combine_rows(h, gates, slot, meta, group_starts, final_gain, ys):
    t, d = h.shape
    tt = _row_tile(t)
    return pl.pallas_call(
        _combine_kernel,
        grid_spec=pltpu.PrefetchScalarGridSpec(
            num_scalar_prefetch=2,
            grid=(t // tt,),
            in_specs=[pl.BlockSpec((tt, d), lambda i, *_: (i, 0)),
                      pl.BlockSpec((tt, LANES), lambda i, *_: (i, 0)),
                      pl.BlockSpec((tt, LANES), lambda i, *_: (i, 0)),
                      pl.BlockSpec((1, d), lambda i, *_: (0, 0)),
                      pl.BlockSpec(memory_space=pl.ANY)],
            out_specs=pl.BlockSpec((tt, d), lambda i, *_: (i, 0)),
            scratch_shapes=[pltpu.VMEM((2, _staging_rows(tt), d), F32), pltpu.SemaphoreType.DMA((2,))]),
        out_shape=jax.ShapeDtypeStruct((t, d), F32),
        compiler_params=_cparams("arbitrary"),
        name="combine_rows",
    )(meta, group_starts, h, gates, slot, final_gain.reshape(1, d), ys)


def attn_out_moe_final_norm(o, w_o, r, g, w_router, w_gu, w_down, final_gain):
    t, d = r.shape
    tt = _row_tile(t)
    n_tok_tiles = t // tt
    tm = EXPERT_ROW_TILE if t >= N_EXPERTS * EXPERT_ROW_TILE else SMALL_EXPERT_ROW_TILE
    worst = TOP_K * t + n_tok_tiles * N_EXPERTS * (RUN_ALIGN - 1) + N_EXPERTS * tm
    n_rows = -(-worst // tm) * tm
    h, gates, slot, meta, counts = out_proj_router(o, w_o, r, g, w_router)

    cnt = counts[0, :N_EXPERTS].astype(I32)
    tiles = (cnt + tm - 1) // tm
    ends = jnp.cumsum(tiles)
    starts = ((ends - tiles) * tm).astype(I32)
    n_active = ends[-1:]
    tile_idx = jnp.arange(n_rows // tm, dtype=I32)
    owner = jnp.sum(tile_idx[:, None] >= ends[None, :], axis=1).astype(I32)
    last_owner = jnp.sum(n_active - 1 >= ends).astype(I32)
    tile_expert = jnp.where(tile_idx < n_active, owner, last_owner)
    tails = jnp.concatenate([starts + cnt, tiles * tm - cnt, n_active]).astype(I32)

    meta_flat = meta.reshape(n_tok_tiles, SUBLANES, LANES)[:, :META_ROWS, :N_EXPERTS].reshape(-1)
    xs = scatter_rows(h, g, slot, meta_flat, starts, tails, n_rows, tm)
    ys = expert_ffn(xs, tile_expert, n_active, w_gu, w_down, tm)
    return combine_rows(h, gates, slot, meta_flat, starts, final_gain, ys)


def _split3(v):
    hi = v.astype(BF16)
    r1 = v - hi.astype(F32)
    mid = r1.astype(BF16)
    lo = (r1 - mid.astype(F32)).astype(BF16)
    return hi, mid, lo


def _ssd_block(rows, blk, xbc_ref, dt_ref, z_ref, cw_ref, cb_ref, dtb_ref, alog_ref, dskip_ref, gn_ref,
               expand_ref, yn_ref, h_scr, xext_scr, seq_valid):
    q = SSD_BLOCK
    halo = SUBLANES

    xext_scr[halo:halo + q, :] = xbc_ref[0, rows, :]
    xext = xext_scr[...]
    conv = cb_ref[...] + xext[halo:] * cw_ref[CONV_WIDTH - 1:CONV_WIDTH, :]
    for k in range(CONV_WIDTH - 1):
        delayed = pltpu.roll(xext, CONV_WIDTH - 1 - k, axis=0)
        conv = conv + delayed[halo:] * cw_ref[k:k + 1, :]
    xext_scr[0:halo, :] = xext_scr[q:q + halo, :]
    xbc = _silu(conv)
    xs = xbc[:, :D_INNER]
    b_all = xbc[:, D_INNER:D_INNER + SSM_GROUPS * D_STATE]
    c_all = xbc[:, D_INNER + SSM_GROUPS * D_STATE:]

    v = dt_ref[0, rows, :] + dtb_ref[...]
    dt = jnp.maximum(v, 0.0) + jnp.log1p(jnp.exp(-jnp.abs(v)))
    row = lax.broadcasted_iota(I32, (q, LANES), 0)
    if seq_valid is not None:
        dt = jnp.where(row + blk * q < seq_valid, dt, 0.0)
    a_neg = -jnp.exp(alog_ref[...])
    dta = dt * a_neg

    ti = lax.broadcasted_iota(I32, (q, q), 0)
    si = lax.broadcasted_iota(I32, (q, q), 1)
    causal = ti >= si
    tri = jnp.where(causal, 1.0, 0.0).astype(BF16)
    a_cs = sum(jnp.dot(tri, part, preferred_element_type=F32) for part in _split3(dta))
    a_last = a_cs[q - 1:q, :]
    a_cs_t = a_cs.T

    stacked = jnp.concatenate([dt, jnp.exp(a_cs), jnp.exp(a_last - a_cs),
                               jnp.broadcast_to(jnp.exp(a_last), (SUBLANES, LANES))], axis=0)
    s_hi = stacked.astype(BF16)
    s_lo = (stacked - s_hi.astype(F32)).astype(BF16)
    expand = expand_ref[...]
    expanded = (jnp.dot(s_hi, expand, preferred_element_type=F32)
                + jnp.dot(s_lo, expand, preferred_element_type=F32))
    dt_e = expanded[0:q]
    decay_in_e = expanded[q:2 * q]
    decay_out_e = expanded[2 * q:3 * q]
    decay_blk_e = expanded[3 * q:3 * q + 1]

    xdt = xs * dt_e
    xdt_bf = xdt.astype(BF16)
    wx = (xdt * decay_out_e).astype(BF16)
    low_half = lax.broadcasted_iota(I32, (q, LANES), 1) < SSM_HEAD_DIM

    y_groups = []
    for g in range(SSM_GROUPS):
        bg = b_all[:, g * D_STATE:(g + 1) * D_STATE]
        cg = c_all[:, g * D_STATE:(g + 1) * D_STATE].astype(BF16)
        bg_bf = bg.astype(BF16)
        cbm = lax.dot_general(cg, bg_bf, (((1,), (1,)), ((), ())), preferred_element_type=F32)
        cols = slice(g * GROUP_COLS, (g + 1) * GROUP_COLS)
        h_g = h_scr[g]
        y_g = jnp.dot(cg, h_g.astype(BF16), preferred_element_type=F32) * decay_in_e[:, cols]
        pair_out = []
        for pr in range(HEADS_PER_GROUP // 2):
            parts = []
            for par in range(2):
                h = g * HEADS_PER_GROUP + 2 * pr + par
                seg = (jnp.broadcast_to(a_cs[:, h:h + 1], (q, q))
                       - jnp.broadcast_to(a_cs_t[h:h + 1, :], (q, q)))
                m = (cbm * jnp.exp(jnp.where(causal, seg, -jnp.inf))).astype(BF16)
                pcols = slice((h // 2) * LANES, (h // 2 + 1) * LANES)
                parts.append(jnp.dot(m, xdt_bf[:, pcols], preferred_element_type=F32))
            pair_out.append(jnp.where(low_half, parts[0], parts[1]))
        y_groups.append(y_g + jnp.concatenate(pair_out, axis=1))
        h_scr[g] = (h_g * decay_blk_e[:, cols]
                    + jnp.dot(bg.T.astype(BF16), wx[:, cols], preferred_element_type=F32))

    y = jnp.concatenate(y_groups, axis=1) + dskip_ref[...] * xs
    yz = y * _silu(z_ref[0, rows, :])
    yn_ref[0, rows, :] = (yz * _rms_scale(yz) * gn_ref[...]).astype(yn_ref.dtype)


def _ssd_kernel(xbc_ref, dt_ref, z_ref, cbuf_ref, h0_ref, cw_ref, cb_ref, dtb_ref, alog_ref,
                dskip_ref, gn_ref, expand_ref, yn_ref, hfin_ref, h_scr, xext_scr, *, seq_valid, n_sub):
    c = pl.program_id(1)

    @pl.when(c == 0)
    def _():
        h_scr[...] = h0_ref[0]
        xext_scr[0:SUBLANES, :] = cbuf_ref[0]

    for sub in range(n_sub):
        _ssd_block(slice(sub * SSD_BLOCK, (sub + 1) * SSD_BLOCK), c * n_sub + sub, xbc_ref, dt_ref, z_ref,
                   cw_ref, cb_ref, dtb_ref, alog_ref, dskip_ref, gn_ref, expand_ref, yn_ref, h_scr, xext_scr,
                   seq_valid)

    @pl.when(c == pl.num_programs(1) - 1)
    def _():
        hfin_ref[0] = h_scr[...]


def _state_to_kernel_layout(h):
    b = h.shape[0]
    h = h.reshape(b, SSM_GROUPS, HEADS_PER_GROUP, SSM_HEAD_DIM, D_STATE)
    return h.transpose(0, 1, 4, 2, 3).reshape(b, SSM_GROUPS, D_STATE, GROUP_COLS)


def _state_from_kernel_layout(h):
    b = h.shape[0]
    h = h.reshape(b, SSM_GROUPS, D_STATE, HEADS_PER_GROUP, SSM_HEAD_DIM)
    return h.transpose(0, 1, 3, 4, 2).reshape(b, N_SSM_HEADS, SSM_HEAD_DIM, D_STATE)


def _pad_lanes(v):
    return jnp.pad(v.astype(F32), (0, LANES - v.shape[0])).reshape(1, LANES)


def ssd_mixer(xbc, dt_raw, z, conv_buf, h0, conv_w, conv_b, dt_bias, a_log, d_skip, gnorm, seq_valid):
    bsz, seq_pad, _ = xbc.shape
    q = SSD_BLOCK
    assert seq_pad % q == 0
    n_blocks = seq_pad // q
    n_sub = 4 if n_blocks % 4 == 0 else (2 if n_blocks % 2 == 0 else 1)
    rows_per_step = n_sub * q
    cbuf = jnp.pad(conv_buf, ((0, 0), (SUBLANES - (CONV_WIDTH - 1), 0), (0, 0)))
    expand = (jnp.arange(D_INNER)[None, :] // SSM_HEAD_DIM == jnp.arange(LANES)[:, None]).astype(BF16)
    d_e = jnp.repeat(d_skip.astype(F32), SSM_HEAD_DIM).reshape(1, D_INNER)
    state_spec = pl.BlockSpec((1, SSM_GROUPS, D_STATE, GROUP_COLS), lambda b, c: (b, 0, 0, 0))

    def const(shape):
        return pl.BlockSpec(shape, lambda b, c: (0,) * len(shape))

    yn, h_fin = pl.pallas_call(
        functools.partial(_ssd_kernel, seq_valid=seq_valid, n_sub=n_sub),
        grid=(bsz, seq_pad // rows_per_step),
        in_specs=[pl.BlockSpec((1, rows_per_step, CONV_DIM), lambda b, c: (b, c, 0)),
                  pl.BlockSpec((1, rows_per_step, LANES), lambda b, c: (b, c, 0)),
                  pl.BlockSpec((1, rows_per_step, D_INNER), lambda b, c: (b, c, 0)),
                  pl.BlockSpec((1, SUBLANES, CONV_DIM), lambda b, c: (b, 0, 0)),
                  state_spec,
                  const((CONV_WIDTH, CONV_DIM)), const((1, CONV_DIM)), const((1, LANES)), const((1, LANES)),
                  const((1, D_INNER)), const((1, D_INNER)), const((LANES, D_INNER))],
        out_specs=[pl.BlockSpec((1, rows_per_step, D_INNER), lambda b, c: (b, c, 0)), state_spec],
        out_shape=[jax.ShapeDtypeStruct((bsz, seq_pad, D_INNER), BF16),
                   jax.ShapeDtypeStruct((bsz, SSM_GROUPS, D_STATE, GROUP_COLS), F32)],
        scratch_shapes=[pltpu.VMEM((SSM_GROUPS, D_STATE, GROUP_COLS), F32),
                        pltpu.VMEM((q + SUBLANES, CONV_DIM), F32)],
        compiler_params=_cparams("parallel", "arbitrary"),
        name="ssd_mixer",
    )(xbc, dt_raw, z, cbuf, _state_to_kernel_layout(h0.astype(F32)), conv_w.astype(F32),
      conv_b.astype(F32).reshape(1, CONV_DIM), _pad_lanes(dt_bias), _pad_lanes(a_log), d_e,
      gnorm.astype(F32).reshape(1, D_INNER), expand)
    return yn, _state_from_kernel_layout(h_fin)


_ALIBI_SLOPES = (2.0 ** (-8.0 * np.arange(1, N_Q_HEADS + 1, dtype=np.float32) / N_Q_HEADS)).astype(np.float32)


def _attn_bias(lbq, chunk, past_rows):
    w = WINDOW + lbq
    qi = np.arange(lbq)[:, None]
    lo = (qi // chunk) * chunk
    variants = []
    for lead in (past_rows, WINDOW):
        si = np.arange(w)[None, :] + (WINDOW - lead)
        dist = np.abs(WINDOW + qi - si).astype(np.float32)
        valid = (si >= lo) & (si < lo + WINDOW + chunk)
        per_head = [np.where(valid, -(_ALIBI_SLOPES[h] * dist), np.float32(NEG_INF)) for h in range(N_Q_HEADS)]
        variants.append(np.stack([np.concatenate([per_head[4 * kv + par], per_head[4 * kv + 2 + par]], axis=0)
                                  for kv in range(N_KV_HEADS) for par in range(2)]))
    return np.stack(variants).astype(np.float32)


def _attn_kernel(sink_ref, q_ref, kv_ref, bias_ref, o_ref, *, lbq, n_sub, past_rows):
    for sub in range(n_sub):
        _attn_block(slice(sub * lbq, (sub + 1) * lbq), pl.program_id(1) * n_sub + sub,
                    sink_ref, q_ref, kv_ref, bias_ref, o_ref, lbq, past_rows)


def _attn_block(rows, n, sink_ref, q_ref, kv_ref, bias_ref, o_ref, lbq, past_rows):
    w = WINDOW + lbq
    base = pl.multiple_of(jnp.maximum(n * lbq + (past_rows - WINDOW), 0), SUBLANES)
    variant = jnp.minimum(n, 1)
    kband = kv_ref[0, pl.ds(base, w), :KV_DIM]
    vband = kv_ref[0, pl.ds(base, w), KV_DIM:]
    first_pair = lax.broadcasted_iota(I32, (2 * lbq, 1), 0) < lbq
    lane = lax.broadcasted_iota(I32, (w, LANES), 1)
    low_half = lane < ATTN_HEAD_DIM

    for kv in range(N_KV_HEADS):
        pcols = slice((kv // 2) * LANES, (kv // 2 + 1) * LANES)
        kpair, vpair = kband[:, pcols], vband[:, pcols]
        kswap = pltpu.roll(kpair, ATTN_HEAD_DIM, axis=1)
        vswap = pltpu.roll(vpair, ATTN_HEAD_DIM, axis=1)
        in_low = kv % 2 == 0
        k_lo = jnp.where(low_half, kpair if in_low else kswap, 0.0).astype(BF16)
        k_hi = jnp.where(low_half, 0.0, kswap if in_low else kpair).astype(BF16)
        v_lo = jnp.where(low_half, vpair if in_low else vswap, 0.0).astype(BF16)
        v_hi = jnp.where(low_half, 0.0, vswap if in_low else vpair).astype(BF16)
        qs = q_ref[2 * kv:2 * kv + 2, 0, rows, :].reshape(2 * lbq, LANES)
        out = None
        for par, (kx, vx) in enumerate(((k_lo, v_lo), (k_hi, v_hi))):
            h0, h1 = 4 * kv + par, 4 * kv + 2 + par
            sink = jnp.where(first_pair, sink_ref[h0], sink_ref[h1])
            s = lax.dot_general(qs, kx, (((1,), (1,)), ((), ())), preferred_element_type=F32)
            s = s + bias_ref[variant, 2 * kv + par]
            m = jnp.maximum(jnp.max(s, axis=-1, keepdims=True), sink)
            p = jnp.exp(s - m)
            denom = jnp.sum(p, axis=-1, keepdims=True) + jnp.exp(sink - m)
            pn = (p / denom).astype(BF16)
            part = jnp.dot(pn, vx, preferred_element_type=F32)
            out = part if out is None else out + part
        o_ref[0, rows, (2 * kv) * LANES:(2 * kv + 1) * LANES] = out[:lbq].astype(o_ref.dtype)
        o_ref[0, rows, (2 * kv + 1) * LANES:(2 * kv + 2) * LANES] = out[lbq:].astype(o_ref.dtype)


def window_attention(q_pairs, kv_all, sinks, *, lbq, chunk):
    n_pairs, bsz, seq, _ = q_pairs.shape
    rows = kv_all.shape[1]
    past_rows = rows - seq
    assert past_rows in (0, WINDOW) and rows >= WINDOW + lbq
    bias = jnp.asarray(_attn_bias(lbq, chunk, past_rows))
    n_blocks = seq // lbq
    n_sub = 4 if n_blocks % 4 == 0 else (2 if n_blocks % 2 == 0 else 1)
    step = n_sub * lbq
    return pl.pallas_call(
        functools.partial(_attn_kernel, lbq=lbq, n_sub=n_sub, past_rows=past_rows),
        grid_spec=pltpu.PrefetchScalarGridSpec(
            num_scalar_prefetch=1,
            grid=(bsz, seq // step),
            in_specs=[pl.BlockSpec((n_pairs, 1, step, LANES), lambda b, n, s: (0, b, n, 0)),
                      pl.BlockSpec((1, rows, 2 * KV_DIM), lambda b, n, s: (b, 0, 0)),
                      _resident(bias.shape)],
            out_specs=pl.BlockSpec((1, step, D_MODEL), lambda b, n, s: (b, n, 0))),
        out_shape=jax.ShapeDtypeStruct((bsz, seq, D_MODEL), BF16),
        compiler_params=_cparams("parallel", "arbitrary"),
        name="window_attention",
    )(sinks.astype(F32), q_pairs, kv_all, bias)


def _trunk(x, conv_buf, ssm_state, k_past, v_past, past_valid, p):
    bsz, seq, d = x.shape
    t = bsz * seq
    x2 = x.reshape(t, d)

    seq_pad = -(-seq // SSD_BLOCK) * SSD_BLOCK
    x_in = x2 if seq_pad == seq else jnp.pad(x, ((0, 0), (0, seq_pad - seq), (0, 0))).reshape(bsz * seq_pad, d)
    z, xbc, dt_raw = in_proj(x_in, p['norm_mix'][0], p['w_z'], p['w_xbc'], p['w_dt'])
    xbc3 = xbc.reshape(bsz, seq_pad, CONV_DIM)
    new_conv = jnp.concatenate([conv_buf.astype(F32), xbc3[:, :seq]], axis=1)[:, seq:]
    yn, new_ssm = ssd_mixer(xbc3, dt_raw.reshape(bsz, seq_pad, LANES), z.reshape(bsz, seq_pad, D_INNER),
                            conv_buf.astype(F32), ssm_state, p['ssm_conv_w'], p['ssm_conv_b'],
                            p['ssm_dt_bias'], p['ssm_a_log'], p['ssm_d'], p['ssm_norm'],
                            None if seq_pad == seq else seq)
    yn = yn[:, :seq]

    h, kv, q_pairs = layer0_tail(yn.reshape(t, D_INNER), p['ssm_out_w'], x2, p['norm_ffn'][0],
                                 p['ffn_w_gate_up'], p['ffn_w_down'], p['kv_norm'], p['norm_mix'][1],
                                 p['w_kv'], p['w_q'])
    kv = kv.reshape(bsz, seq, 2 * KV_DIM)
    k_new, v_new = kv[..., :KV_DIM], kv[..., KV_DIM:]
    if past_valid:
        past = jnp.concatenate([k_past.reshape(bsz, WINDOW, KV_DIM), v_past.reshape(bsz, WINDOW, KV_DIM)], axis=2)
        kv_all = jnp.concatenate([past.astype(F32), kv], axis=1)
    else:
        kv_all = kv

    chunk = min(ATTN_CHUNK, seq)
    lbq = 2 * chunk if seq % (2 * chunk) == 0 else chunk
    q_pairs = q_pairs.reshape(D_MODEL // LANES, bsz, seq, LANES)
    o = window_attention(q_pairs, kv_all, p['attn_sinks'], lbq=lbq, chunk=chunk)

    y = attn_out_moe_final_norm(o.reshape(t, D_MODEL), p['w_o'], h, p['norm_ffn'][1], p['moe_router'],
                                p['moe_w_gate_up'], p['moe_w_down'], p['final_norm'])

    k4 = k_new.reshape(bsz, seq, N_KV_HEADS, ATTN_HEAD_DIM)
    v4 = v_new.reshape(bsz, seq, N_KV_HEADS, ATTN_HEAD_DIM)
    return y.reshape(bsz, seq, d), new_conv[None], new_ssm[None], k4, v4


def kernel(x_prompt, x_sample, state_conv, state_ssm, cache_k, cache_v, norm_mix, norm_ffn, ssm_in_w, ssm_conv_w,
           ssm_conv_b, ssm_dt_bias, ssm_a_log, ssm_d, ssm_norm, ssm_out_w, kv_norm, w_kv, w_q, attn_sinks, w_o,
           ffn_w_gate_up, ffn_w_down, moe_router, moe_w_gate_up, moe_w_down, final_norm):
    in_w = ssm_in_w[0]
    n_dt = N_SSM_HEADS
    p = {
        'norm_mix': norm_mix, 'norm_ffn': norm_ffn,
        'w_z': in_w[:, :D_INNER].astype(BF16),
        'w_xbc': in_w[:, D_INNER:D_INNER + CONV_DIM].astype(BF16),
        'w_dt': jnp.pad(in_w[:, D_INNER + CONV_DIM:], ((0, 0), (0, LANES - n_dt))).astype(BF16),
        'ssm_conv_w': ssm_conv_w[0], 'ssm_conv_b': ssm_conv_b[0], 'ssm_dt_bias': ssm_dt_bias[0],
        'ssm_a_log': ssm_a_log[0], 'ssm_d': ssm_d[0], 'ssm_norm': ssm_norm[0],
        'ssm_out_w': ssm_out_w[0].astype(BF16),
        'kv_norm': kv_norm, 'w_kv': w_kv.astype(BF16), 'w_q': w_q[0].astype(BF16),
        'attn_sinks': attn_sinks[0], 'w_o': w_o[0].astype(BF16),
        'ffn_w_gate_up': ffn_w_gate_up.astype(BF16), 'ffn_w_down': ffn_w_down.astype(BF16),
        'moe_router': jnp.pad(moe_router[0].astype(F32), ((0, 0), (0, LANES - N_EXPERTS))),
        'moe_w_gate_up': moe_w_gate_up[0].astype(BF16), 'moe_w_down': moe_w_down[0].astype(BF16),
        'final_norm': final_norm,
    }
    bsz, seq_p = x_prompt.shape[:2]
    dt = x_prompt.dtype
    zero_conv = jnp.zeros((bsz, CONV_WIDTH - 1, CONV_DIM), dt)
    zero_ssm = jnp.zeros((bsz, N_SSM_HEADS, SSM_HEAD_DIM, D_STATE), dt)
    zero_kv = jnp.zeros((bsz, WINDOW, N_KV_HEADS, ATTN_HEAD_DIM), dt)
    y_p, conv_p, ssm_p, k_p, v_p = _trunk(x_prompt, zero_conv, zero_ssm, zero_kv, zero_kv, False, p)
    keep = min(WINDOW, seq_p)
    y_s, conv_s, ssm_s, k_s, v_s = _trunk(x_sample, state_conv[0], state_ssm[0], cache_k, cache_v, True, p)
    return (y_p, y_s, conv_p, ssm_p, k_p[:, seq_p - keep:], v_p[:, seq_p - keep:], conv_s, ssm_s, k_s, v_s)
```

```python
import functools
import math

import jax
import jax.numpy as jnp
import numpy as np
from jax import lax
from jax.experimental import pallas as pl
from jax.experimental.pallas import tpu as pltpu

F32 = jnp.float32
BF16 = jnp.bfloat16
I32 = jnp.int32

D_MODEL = 1024
EPS = 1e-6
D_INNER = 2048
SSM_HEAD_DIM = 64
N_SSM_HEADS = 32
SSM_GROUPS = 4
HEADS_PER_GROUP = N_SSM_HEADS // SSM_GROUPS
D_STATE = 128
CONV_WIDTH = 4
CONV_DIM = D_INNER + 2 * SSM_GROUPS * D_STATE
ATTN_HEAD_DIM = 64
N_Q_HEADS = 16
N_KV_HEADS = 4
KV_DIM = N_KV_HEADS * ATTN_HEAD_DIM
WINDOW = 128
ATTN_CHUNK = 64
ATTN_SCALE = 1.0 / math.sqrt(ATTN_HEAD_DIM)
NEG_INF = -1e30
D_FF = 2816
N_EXPERTS = 8
TOP_K = 2

LANES = 128
SUBLANES = 8
VMEM_LIMIT_BYTES = 56 * 1024 * 1024
RUN_ALIGN = SUBLANES

SSD_BLOCK = 128
GROUP_COLS = HEADS_PER_GROUP * SSM_HEAD_DIM
FF_CHUNK = 512
EXPERT_ROW_TILE = 512
SMALL_EXPERT_ROW_TILE = 128


def _cparams(*sem):
    return pltpu.CompilerParams(dimension_semantics=sem, vmem_limit_bytes=VMEM_LIMIT_BYTES)


def _row_tile(t, pref=512):
    return pref if t % pref == 0 else t


def _resident(shape):
    return pl.BlockSpec(shape, lambda *_: (0,) * len(shape), pipeline_mode=pl.Buffered(1))


def _rms_scale(x):
    return lax.rsqrt(jnp.mean(x * x, axis=-1, keepdims=True) + EPS)


def _silu(x):
    half = 0.5 * x
    return half + half * jnp.tanh(half)


def _in_proj_kernel(x_ref, g_ref, wz_ref, wx_ref, wd_ref, z_ref, xbc_ref, dt_ref):
    x = x_ref[...]
    xn = (x * _rms_scale(x) * g_ref[...]).astype(BF16)
    z_ref[...] = jnp.dot(xn, wz_ref[...], preferred_element_type=F32)
    xbc_ref[...] = jnp.dot(xn, wx_ref[...], preferred_element_type=F32)
    dt_ref[...] = jnp.dot(xn, wd_ref[...], preferred_element_type=F32)


def in_proj(x, g, w_z, w_xbc, w_dt):
    t, k = x.shape
    tm = _row_tile(t)
    ws = (w_z, w_xbc, w_dt)
    return pl.pallas_call(
        _in_proj_kernel,
        grid=(t // tm,),
        in_specs=[pl.BlockSpec((tm, k), lambda i: (i, 0)), pl.BlockSpec((1, k), lambda i: (0, 0))]
        + [_resident(w.shape) for w in ws],
        out_specs=[pl.BlockSpec((tm, w.shape[1]), lambda i: (i, 0)) for w in ws],
        out_shape=[jax.ShapeDtypeStruct((t, w.shape[1]), F32) for w in ws],
        compiler_params=_cparams("parallel"),
        name="in_proj",
    )(x, g.reshape(1, k), *ws)


def _swiglu(xb, wgu_ref, wd_ref):
    acc = None
    for c0 in range(0, D_FF, FF_CHUNK):
        cw = min(FF_CHUNK, D_FF - c0)
        gg = jnp.dot(xb, wgu_ref[0, :, c0:c0 + cw], preferred_element_type=F32)
        uu = jnp.dot(xb, wgu_ref[0, :, D_FF + c0:D_FF + c0 + cw], preferred_element_type=F32)
        act = (_silu(gg) * uu).astype(BF16)
        part = jnp.dot(act, wd_ref[0, c0:c0 + cw, :], preferred_element_type=F32)
        acc = part if acc is None else acc + part
    return acc


def _layer0_tail_kernel(a_ref, wo_ref, r_ref, gf_ref, wgu_ref, wd_ref, gkv_ref, gq_ref, wkv_ref, wq_ref,
                        h_ref, kv_ref, q_ref):
    h1 = r_ref[...] + jnp.dot(a_ref[...], wo_ref[...], preferred_element_type=F32)
    xb = (h1 * _rms_scale(h1) * gf_ref[...]).astype(BF16)
    h2 = h1 + _swiglu(xb, wgu_ref, wd_ref)
    h_ref[...] = h2
    xs = h2 * _rms_scale(h2)
    kv_ref[...] = jnp.dot((xs * gkv_ref[...]).astype(BF16), wkv_ref[...], preferred_element_type=F32)
    q = jnp.dot((xs * gq_ref[...]).astype(BF16), wq_ref[...], preferred_element_type=F32)
    q = (q * ATTN_SCALE).astype(q_ref.dtype)
    for p in range(q_ref.shape[0]):
        q_ref[p] = q[:, p * LANES:(p + 1) * LANES]


def layer0_tail(a, w_out, r, g_ffn, w_gu, w_down, g_kv, g_q, w_kv, w_q):
    t, k = a.shape
    d = w_out.shape[1]
    tm = _row_tile(t)
    n_pairs = w_q.shape[1] // LANES

    def gain():
        return pl.BlockSpec((1, d), lambda i: (0, 0))

    return pl.pallas_call(
        _layer0_tail_kernel,
        grid=(t // tm,),
        in_specs=[pl.BlockSpec((tm, k), lambda i: (i, 0)), _resident(w_out.shape),
                  pl.BlockSpec((tm, d), lambda i: (i, 0)), gain(),
                  _resident(w_gu.shape), _resident(w_down.shape), gain(), gain(),
                  _resident(w_kv.shape), _resident(w_q.shape)],
        out_specs=[pl.BlockSpec((tm, d), lambda i: (i, 0)),
                   pl.BlockSpec((tm, w_kv.shape[1]), lambda i: (i, 0)),
                   pl.BlockSpec((n_pairs, tm, LANES), lambda i: (0, i, 0))],
        out_shape=[jax.ShapeDtypeStruct((t, d), F32),
                   jax.ShapeDtypeStruct((t, w_kv.shape[1]), F32),
                   jax.ShapeDtypeStruct((n_pairs, t, LANES), BF16)],
        compiler_params=_cparams("parallel"),
        name="layer0_tail",
    )(a, w_out, r, g_ffn.reshape(1, d), w_gu, w_down, g_kv.reshape(1, d), g_q.reshape(1, d), w_kv, w_q)


def _expert_ffn_kernel(te_ref, na_ref, x_ref, wgu_ref, wd_ref, o_ref):
    active = pl.program_id(0) < na_ref[0]

    @pl.when(active)
    def _():
        o_ref[...] = _swiglu(x_ref[...].astype(BF16), wgu_ref, wd_ref)

    @pl.when(jnp.logical_not(active))
    def _():
        o_ref[...] = jnp.zeros_like(o_ref)


def expert_ffn(xs, tile_expert, n_active, w_gu, w_down, tm):
    rows, d = xs.shape
    return pl.pallas_call(
        _expert_ffn_kernel,
        grid_spec=pltpu.PrefetchScalarGridSpec(
            num_scalar_prefetch=2,
            grid=(rows // tm,),
            in_specs=[pl.BlockSpec((tm, d), lambda i, te, na: (jnp.minimum(i, na[0] - 1), 0)),
                      pl.BlockSpec((1, d, 2 * D_FF), lambda i, te, na: (te[i], 0, 0)),
                      pl.BlockSpec((1, D_FF, d), lambda i, te, na: (te[i], 0, 0))],
            out_specs=pl.BlockSpec((tm, d), lambda i, te, na: (i, 0))),
        out_shape=jax.ShapeDtypeStruct((rows, d), F32),
        compiler_params=_cparams("arbitrary"),
        name="expert_ffn",
    )(tile_expert, n_active, xs, w_gu, w_down)


def _router_kernel(a_ref, wo_ref, r_ref, g_ref, wr_ref, h_ref, gate_ref, slot_ref, meta_ref, cnt_ref):
    x = r_ref[...] + jnp.dot(a_ref[...], wo_ref[...], preferred_element_type=F32)
    h_ref[...] = x
    tm = x.shape[0]
    xn = x * _rms_scale(x) * g_ref[...]
    x_hi = xn.astype(BF16)
    x_lo = (xn - x_hi.astype(F32)).astype(BF16)
    prod = jnp.dot(jnp.concatenate([x_hi, x_lo], axis=0), wr_ref[...], preferred_element_type=F32)
    hi_rows, lo_rows = prod[:tm], prod[tm:]
    logits = hi_rows + pltpu.roll(hi_rows, LANES - N_EXPERTS, axis=1) + lo_rows
    lane = lax.broadcasted_iota(I32, logits.shape, 1)
    logits = jnp.where(lane < N_EXPERTS, logits, -jnp.inf)
    m1 = jnp.max(logits, axis=-1, keepdims=True)
    i1 = jnp.min(jnp.where(logits == m1, lane, LANES), axis=-1, keepdims=True)
    rest = jnp.where(lane == i1, -jnp.inf, logits)
    m2 = jnp.max(rest, axis=-1, keepdims=True)
    i2 = jnp.min(jnp.where(rest == m2, lane, LANES), axis=-1, keepdims=True)
    e2 = jnp.exp(m2 - m1)
    w1 = 1.0 / (1.0 + e2)
    w2 = e2 / (1.0 + e2)
    gate_ref[...] = jnp.where(lane == 0, w1, jnp.where(lane == 1, w2, 0.0))
    pick0 = lane == i1
    pick1 = lane == i2
    chosen = jnp.where(jnp.logical_or(pick0, pick1), 1.0, 0.0)

    @pl.when(pl.program_id(0) == 0)
    def _():
        cnt_ref[...] = jnp.zeros_like(cnt_ref)

    ri = lax.broadcasted_iota(I32, (tm, tm), 0)
    ci = lax.broadcasted_iota(I32, (tm, tm), 1)
    before = jnp.where(ri > ci, 1.0, 0.0).astype(BF16)
    rank = jnp.dot(before, chosen.astype(BF16), preferred_element_type=F32)
    n_pad = jnp.ceil(jnp.sum(chosen, axis=0, keepdims=True) * (1.0 / RUN_ALIGN)) * RUN_ALIGN
    ei = lax.broadcasted_iota(I32, (LANES, LANES), 0)
    ej = lax.broadcasted_iota(I32, (LANES, LANES), 1)
    earlier = jnp.where(ei < ej, 1.0, 0.0).astype(BF16)
    n_pad8 = jnp.broadcast_to(n_pad, (SUBLANES, LANES))
    local = jnp.dot(n_pad8.astype(BF16), earlier, preferred_element_type=F32)[0:1, :]
    slot = local + rank
    s0 = jnp.sum(jnp.where(pick0, slot, 0.0), axis=-1, keepdims=True)
    s1 = jnp.sum(jnp.where(pick1, slot, 0.0), axis=-1, keepdims=True)
    slot_ref[...] = jnp.where(lane == 0, s0, jnp.where(lane == 1, s1, 0.0))
    row = lax.broadcasted_iota(I32, (SUBLANES, LANES), 0)
    so_far = cnt_ref[0:1, :]
    meta = jnp.where(row == 0, so_far, jnp.where(row == 1, local, jnp.where(row == 2, n_pad, 0.0)))
    meta_ref[...] = meta.astype(I32)
    cnt_ref[...] += n_pad8


def out_proj_router(a, w_o, r, g, w_router):
    t, k = a.shape
    d = w_o.shape[1]
    tm = _row_tile(t)
    w_hi = w_router.astype(BF16)
    w_lo = (w_router - w_hi.astype(F32)).astype(BF16)
    w_packed = jnp.concatenate([w_hi[:, :N_EXPERTS], w_lo[:, :N_EXPERTS],
                                jnp.zeros((d, LANES - 2 * N_EXPERTS), BF16)], axis=1)
    return pl.pallas_call(
        _router_kernel,
        grid=(t // tm,),
        in_specs=[pl.BlockSpec((tm, k), lambda i: (i, 0)),
                  _resident(w_o.shape),
                  pl.BlockSpec((tm, d), lambda i: (i, 0)),
                  pl.BlockSpec((1, d), lambda i: (0, 0)),
                  _resident((d, LANES))],
        out_specs=[pl.BlockSpec((tm, d), lambda i: (i, 0)),
                   pl.BlockSpec((tm, LANES), lambda i: (i, 0)),
                   pl.BlockSpec((tm, LANES), lambda i: (i, 0)),
                   pl.BlockSpec((SUBLANES, LANES), lambda i: (i, 0)),
                   pl.BlockSpec((SUBLANES, LANES), lambda i: (0, 0))],
        out_shape=[jax.ShapeDtypeStruct((t, d), F32),
                   jax.ShapeDtypeStruct((t, LANES), F32),
                   jax.ShapeDtypeStruct((t, LANES), F32),
                   jax.ShapeDtypeStruct((SUBLANES * (t // tm), LANES), I32),
                   jax.ShapeDtypeStruct((SUBLANES, LANES), F32)],
        compiler_params=_cparams("arbitrary"),
        name="out_proj_router",
    )(a, w_o, r, g.reshape(1, d), w_packed)


META_ROWS = 3


def _staging_rows(tt):
    need = TOP_K * tt + N_EXPERTS * (RUN_ALIGN - 1)
    return -(-need // LANES) * LANES


def _run_copies(meta_ref, group_ref, tile, staging, sorted_hbm, sem, to_sorted):
    first = tile * (META_ROWS * N_EXPERTS)
    out = []
    for e in range(N_EXPERTS):
        sorted_start = pl.multiple_of(group_ref[e] + meta_ref[first + e], RUN_ALIGN)
        local_start = pl.multiple_of(meta_ref[first + N_EXPERTS + e], RUN_ALIGN)
        n = pl.multiple_of(meta_ref[first + 2 * N_EXPERTS + e], RUN_ALIGN)
        a, b = staging.at[pl.ds(local_start, n)], sorted_hbm.at[pl.ds(sorted_start, n)]
        out.append((n, pltpu.make_async_copy(a, b, sem) if to_sorted else pltpu.make_async_copy(b, a, sem)))
    return out


def _start(copies):
    for n, copy in copies:
        pl.when(n > 0)(copy.start)


def _wait(copies):
    for n, copy in copies:
        pl.when(n > 0)(copy.wait)


def _start_then_wait(copies):
    _start(copies)
    _wait(copies)


def _scatter_kernel(meta_ref, group_ref, tail_ref, x_ref, g_ref, slot_ref, xs_hbm, stage_scr, zero_scr, sem):
    i = pl.program_id(0)
    last = pl.num_programs(0) - 1
    buf = i % 2
    x = x_ref[...]
    xn = (x * _rms_scale(x) * g_ref[...]).astype(BF16)
    tt, rt = x.shape[0], stage_scr.shape[1]
    slot_t = slot_ref[...].T.astype(I32)
    r = lax.broadcasted_iota(I32, (rt, tt), 0)
    place = jnp.where(jnp.logical_or(r == slot_t[0:1, :], r == slot_t[1:2, :]), 1.0, 0.0).astype(BF16)
    stage_scr[buf] = jnp.dot(place, xn, preferred_element_type=F32)
    mine = _run_copies(meta_ref, group_ref, i, stage_scr.at[buf], xs_hbm, sem.at[buf], to_sorted=True)
    _start(mine)

    @pl.when(i > 0)
    def _():
        _wait(_run_copies(meta_ref, group_ref, i - 1, stage_scr.at[1 - buf], xs_hbm, sem.at[1 - buf],
                          to_sorted=True))

    @pl.when(i == last)
    def _():
        _wait(mine)
        zero_scr[...] = jnp.zeros_like(zero_scr)
        tails = []
        for e in range(N_EXPERTS):
            start = pl.multiple_of(tail_ref[e], RUN_ALIGN)
            n = pl.multiple_of(tail_ref[N_EXPERTS + e], RUN_ALIGN)
            tails.append((n, pltpu.make_async_copy(zero_scr.at[pl.ds(0, n)], xs_hbm.at[pl.ds(start, n)], sem.at[0])))
        _start_then_wait(tails)

        tm = zero_scr.shape[0]

        def clear_tile(j, carry):
            copy = pltpu.make_async_copy(zero_scr, xs_hbm.at[pl.ds(pl.multiple_of(j * tm, tm), tm)], sem.at[0])
            copy.start()
            copy.wait()
            return carry

        lax.fori_loop(tail_ref[2 * N_EXPERTS], xs_hbm.shape[0] // tm, clear_tile, 0)


def scatter_rows(x, g, slot, meta, group_starts, tails, n_rows, tm):
    t, d = x.shape
    tt = _row_tile(t)
    return pl.pallas_call(
        _scatter_kernel,
        grid_spec=pltpu.PrefetchScalarGridSpec(
            num_scalar_prefetch=3,
            grid=(t // tt,),
            in_specs=[pl.BlockSpec((tt, d), lambda i, *_: (i, 0)),
                      pl.BlockSpec((1, d), lambda i, *_: (0, 0)),
                      pl.BlockSpec((tt, LANES), lambda i, *_: (i, 0))],
            out_specs=pl.BlockSpec(memory_space=pl.ANY),
            scratch_shapes=[pltpu.VMEM((2, _staging_rows(tt), d), F32), pltpu.VMEM((tm, d), F32),
                            pltpu.SemaphoreType.DMA((2,))]),
        out_shape=jax.ShapeDtypeStruct((n_rows, d), F32),
        compiler_params=_cparams("arbitrary"),
        name="scatter_rows",
    )(meta, group_starts, tails, x, g.reshape(1, d), slot)


def _combine_kernel(meta_ref, group_ref, h_ref, gate_ref, slot_ref, fg_ref, ys_hbm, o_ref, stage_scr, sem):
    i = pl.program_id(0)
    buf = i % 2

    def fetch(tile, b):
        return _run_copies(meta_ref, group_ref, tile, stage_scr.at[b], ys_hbm, sem.at[b], to_sorted=False)

    @pl.when(i == 0)
    def _():
        stage_scr[...] = jnp.zeros_like(stage_scr)
        _start(fetch(0, 0))

    @pl.when(i + 1 < pl.num_programs(0))
    def _():
        _start(fetch(i + 1, 1 - buf))

    tt, rt = h_ref.shape[0], stage_scr.shape[1]
    half = tt // 2

    def selection(rows):
        slot = slot_ref[rows, :].astype(I32)
        gate = gate_ref[rows, :]
        c = lax.broadcasted_iota(I32, (half, rt), 1)
        return (jnp.where(c == slot[:, 0:1], gate[:, 0:1], 0.0)
                + jnp.where(c == slot[:, 1:2], gate[:, 1:2], 0.0)).astype(BF16)

    first_rows, second_rows = slice(0, half), slice(half, tt)
    weight = selection(first_rows)
    _wait(fetch(i, buf))
    yb = stage_scr[buf].astype(BF16)
    for rows in (first_rows, second_rows):
        y = h_ref[rows, :] + jnp.dot(weight, yb, preferred_element_type=F32)
        if rows is first_rows:
            weight = selection(second_rows)
        o_ref[rows, :] = y * _rms_scale(y) * fg_ref[...]


def combine_rows(h, gates, slot, meta, group_starts, final_gain, ys):
    t, d = h.shape
    tt = _row_tile(t)
    return pl.pallas_call(
        _combine_kernel,
        grid_spec=pltpu.PrefetchScalarGridSpec(
            num_scalar_prefetch=2,
            grid=(t // tt,),
            in_specs=[pl.BlockSpec((tt, d), lambda i, *_: (i, 0)),
                      pl.BlockSpec((tt, LANES), lambda i, *_: (i, 0)),
                      pl.BlockSpec((tt, LANES), lambda i, *_: (i, 0)),
                      pl.BlockSpec((1, d), lambda i, *_: (0, 0)),
                      pl.BlockSpec(memory_space=pl.ANY)],
            out_specs=pl.BlockSpec((tt, d), lambda i, *_: (i, 0)),
            scratch_shapes=[pltpu.VMEM((2, _staging_rows(tt), d), F32), pltpu.SemaphoreType.DMA((2,))]),
        out_shape=jax.ShapeDtypeStruct((t, d), F32),
        compiler_params=_cparams("arbitrary"),
        name="combine_rows",
    )(meta, group_starts, h, gates, slot, final_gain.reshape(1, d), ys)


def attn_out_moe_final_norm(o, w_o, r, g, w_router, w_gu, w_down, final_gain):
    t, d = r.shape
    tt = _row_tile(t)
    n_tok_tiles = t // tt
    tm = EXPERT_ROW_TILE if t >= N_EXPERTS * EXPERT_ROW_TILE else SMALL_EXPERT_ROW_TILE
    worst = TOP_K * t + n_tok_tiles * N_EXPERTS * (RUN_ALIGN - 1) + N_EXPERTS * tm
    n_rows = -(-worst // tm) * tm
    h, gates, slot, meta, counts = out_proj_router(o, w_o, r, g, w_router)

    cnt = counts[0, :N_EXPERTS].astype(I32)
    tiles = (cnt + tm - 1) // tm
    ends = jnp.cumsum(tiles)
    starts = ((ends - tiles) * tm).astype(I32)
    n_active = ends[-1:]
    tile_idx = jnp.arange(n_rows // tm, dtype=I32)
    owner = jnp.sum(tile_idx[:, None] >= ends[None, :], axis=1).astype(I32)
    last_owner = jnp.sum(n_active - 1 >= ends).astype(I32)
    tile_expert = jnp.where(tile_idx < n_active, owner, last_owner)
    tails = jnp.concatenate([starts + cnt, tiles * tm - cnt, n_active]).astype(I32)

    meta_flat = meta.reshape(n_tok_tiles, SUBLANES, LANES)[:, :META_ROWS, :N_EXPERTS].reshape(-1)
    xs = scatter_rows(h, g, slot, meta_flat, starts, tails, n_rows, tm)
    ys = expert_ffn(xs, tile_expert, n_active, w_gu, w_down, tm)
    return combine_rows(h, gates, slot, meta_flat, starts, final_gain, ys)


def _split3(v):
    hi = v.astype(BF16)
    r1 = v - hi.astype(F32)
    mid = r1.astype(BF16)
    lo = (r1 - mid.astype(F32)).astype(BF16)
    return hi, mid, lo


def _ssd_block(rows, blk, xbc_ref, dt_ref, z_ref, cw_ref, cb_ref, dtb_ref, alog_ref, dskip_ref, gn_ref,
               expand_ref, yn_ref, h_scr, xext_scr, seq_valid):
    q = SSD_BLOCK
    halo = SUBLANES

    xext_scr[halo:halo + q, :] = xbc_ref[0, rows, :]
    xext = xext_scr[...]
    conv = cb_ref[...] + xext[halo:] * cw_ref[CONV_WIDTH - 1:CONV_WIDTH, :]
    for k in range(CONV_WIDTH - 1):
        delayed = pltpu.roll(xext, CONV_WIDTH - 1 - k, axis=0)
        conv = conv + delayed[halo:] * cw_ref[k:k + 1, :]
    xext_scr[0:halo, :] = xext_scr[q:q + halo, :]
    xbc = _silu(conv)
    xs = xbc[:, :D_INNER]
    b_all = xbc[:, D_INNER:D_INNER + SSM_GROUPS * D_STATE]
    c_all = xbc[:, D_INNER + SSM_GROUPS * D_STATE:]

    v = dt_ref[0, rows, :] + dtb_ref[...]
    dt = jnp.maximum(v, 0.0) + jnp.log1p(jnp.exp(-jnp.abs(v)))
    row = lax.broadcasted_iota(I32, (q, LANES), 0)
    if seq_valid is not None:
        dt = jnp.where(row + blk * q < seq_valid, dt, 0.0)
    a_neg = -jnp.exp(alog_ref[...])
    dta = dt * a_neg

    ti = lax.broadcasted_iota(I32, (q, q), 0)
    si = lax.broadcasted_iota(I32, (q, q), 1)
    causal = ti >= si
    tri = jnp.where(causal, 1.0, 0.0).astype(BF16)
    a_cs = sum(jnp.dot(tri, part, preferred_element_type=F32) for part in _split3(dta))
    a_last = a_cs[q - 1:q, :]
    a_cs_t = a_cs.T

    stacked = jnp.concatenate([dt, jnp.exp(a_cs), jnp.exp(a_last - a_cs),
                               jnp.broadcast_to(jnp.exp(a_last), (SUBLANES, LANES))], axis=0)
    s_hi = stacked.astype(BF16)
    s_lo = (stacked - s_hi.astype(F32)).astype(BF16)
    expand = expand_ref[...]
    expanded = (jnp.dot(s_hi, expand, preferred_element_type=F32)
                + jnp.dot(s_lo, expand, preferred_element_type=F32))
    dt_e = expanded[0:q]
    decay_in_e = expanded[q:2 * q]
    decay_out_e = expanded[2 * q:3 * q]
    decay_blk_e = expanded[3 * q:3 * q + 1]

    xdt = xs * dt_e
    xdt_bf = xdt.astype(BF16)
    wx = (xdt * decay_out_e).astype(BF16)
    low_half = lax.broadcasted_iota(I32, (q, LANES), 1) < SSM_HEAD_DIM

    y_groups = []
    for g in range(SSM_GROUPS):
        bg = b_all[:, g * D_STATE:(g + 1) * D_STATE]
        cg = c_all[:, g * D_STATE:(g + 1) * D_STATE].astype(BF16)
        bg_bf = bg.astype(BF16)
        cbm = lax.dot_general(cg, bg_bf, (((1,), (1,)), ((), ())), preferred_element_type=F32)
        cols = slice(g * GROUP_COLS, (g + 1) * GROUP_COLS)
        h_g = h_scr[g]
        y_g = jnp.dot(cg, h_g.astype(BF16), preferred_element_type=F32) * decay_in_e[:, cols]
        pair_out = []
        for pr in range(HEADS_PER_GROUP // 2):
            parts = []
            for par in range(2):
                h = g * HEADS_PER_GROUP + 2 * pr + par
                seg = (jnp.broadcast_to(a_cs[:, h:h + 1], (q, q))
                       - jnp.broadcast_to(a_cs_t[h:h + 1, :], (q, q)))
                m = (cbm * jnp.exp(jnp.where(causal, seg, -jnp.inf))).astype(BF16)
                pcols = slice((h // 2) * LANES, (h // 2 + 1) * LANES)
                parts.append(jnp.dot(m, xdt_bf[:, pcols], preferred_element_type=F32))
            pair_out.append(jnp.where(low_half, parts[0], parts[1]))
        y_groups.append(y_g + jnp.concatenate(pair_out, axis=1))
        h_scr[g] = (h_g * decay_blk_e[:, cols]
                    + jnp.dot(bg.T.astype(BF16), wx[:, cols], preferred_element_type=F32))

    y = jnp.concatenate(y_groups, axis=1) + dskip_ref[...] * xs
    yz = y * _silu(z_ref[0, rows, :])
    yn_ref[0, rows, :] = (yz * _rms_scale(yz) * gn_ref[...]).astype(yn_ref.dtype)


def _ssd_kernel(xbc_ref, dt_ref, z_ref, cbuf_ref, h0_ref, cw_ref, cb_ref, dtb_ref, alog_ref,
                dskip_ref, gn_ref, expand_ref, yn_ref, hfin_ref, h_scr, xext_scr, *, seq_valid, n_sub):
    c = pl.program_id(1)

    @pl.when(c == 0)
    def _():
        h_scr[...] = h0_ref[0]
        xext_scr[0:SUBLANES, :] = cbuf_ref[0]

    for sub in range(n_sub):
        _ssd_block(slice(sub * SSD_BLOCK, (sub + 1) * SSD_BLOCK), c * n_sub + sub, xbc_ref, dt_ref, z_ref,
                   cw_ref, cb_ref, dtb_ref, alog_ref, dskip_ref, gn_ref, expand_ref, yn_ref, h_scr, xext_scr,
                   seq_valid)

    @pl.when(c == pl.num_programs(1) - 1)
    def _():
        hfin_ref[0] = h_scr[...]


def _state_to_kernel_layout(h):
    b = h.shape[0]
    h = h.reshape(b, SSM_GROUPS, HEADS_PER_GROUP, SSM_HEAD_DIM, D_STATE)
    return h.transpose(0, 1, 4, 2, 3).reshape(b, SSM_GROUPS, D_STATE, GROUP_COLS)


def _state_from_kernel_layout(h):
    b = h.shape[0]
    h = h.reshape(b, SSM_GROUPS, D_STATE, HEADS_PER_GROUP, SSM_HEAD_DIM)
    return h.transpose(0, 1, 3, 4, 2).reshape(b, N_SSM_HEADS, SSM_HEAD_DIM, D_STATE)


def _pad_lanes(v):
    return jnp.pad(v.astype(F32), (0, LANES - v.shape[0])).reshape(1, LANES)


def ssd_mixer(xbc, dt_raw, z, conv_buf, h0, conv_w, conv_b, dt_bias, a_log, d_skip, gnorm, seq_valid):
    bsz, seq_pad, _ = xbc.shape
    q = SSD_BLOCK
    assert seq_pad % q == 0
    n_blocks = seq_pad // q
    n_sub = 4 if n_blocks % 4 == 0 else (2 if n_blocks % 2 == 0 else 1)
    rows_per_step = n_sub * q
    cbuf = jnp.pad(conv_buf, ((0, 0), (SUBLANES - (CONV_WIDTH - 1), 0), (0, 0)))
    expand = (jnp.arange(D_INNER)[None, :] // SSM_HEAD_DIM == jnp.arange(LANES)[:, None]).astype(BF16)
    d_e = jnp.repeat(d_skip.astype(F32), SSM_HEAD_DIM).reshape(1, D_INNER)
    state_spec = pl.BlockSpec((1, SSM_GROUPS, D_STATE, GROUP_COLS), lambda b, c: (b, 0, 0, 0))

    def const(shape):
        return pl.BlockSpec(shape, lambda b, c: (0,) * len(shape))

    yn, h_fin = pl.pallas_call(
        functools.partial(_ssd_kernel, seq_valid=seq_valid, n_sub=n_sub),
        grid=(bsz, seq_pad // rows_per_step),
        in_specs=[pl.BlockSpec((1, rows_per_step, CONV_DIM), lambda b, c: (b, c, 0)),
                  pl.BlockSpec((1, rows_per_step, LANES), lambda b, c: (b, c, 0)),
                  pl.BlockSpec((1, rows_per_step, D_INNER), lambda b, c: (b, c, 0)),
                  pl.BlockSpec((1, SUBLANES, CONV_DIM), lambda b, c: (b, 0, 0)),
                  state_spec,
                  const((CONV_WIDTH, CONV_DIM)), const((1, CONV_DIM)), const((1, LANES)), const((1, LANES)),
                  const((1, D_INNER)), const((1, D_INNER)), const((LANES, D_INNER))],
        out_specs=[pl.BlockSpec((1, rows_per_step, D_INNER), lambda b, c: (b, c, 0)), state_spec],
        out_shape=[jax.ShapeDtypeStruct((bsz, seq_pad, D_INNER), BF16),
                   jax.ShapeDtypeStruct((bsz, SSM_GROUPS, D_STATE, GROUP_COLS), F32)],
        scratch_shapes=[pltpu.VMEM((SSM_GROUPS, D_STATE, GROUP_COLS), F32),
                        pltpu.VMEM((q + SUBLANES, CONV_DIM), F32)],
        compiler_params=_cparams("parallel", "arbitrary"),
        name="ssd_mixer",
    )(xbc, dt_raw, z, cbuf, _state_to_kernel_layout(h0.astype(F32)), conv_w.astype(F32),
      conv_b.astype(F32).reshape(1, CONV_DIM), _pad_lanes(dt_bias), _pad_lanes(a_log), d_e,
      gnorm.astype(F32).reshape(1, D_INNER), expand)
    return yn, _state_from_kernel_layout(h_fin)


_ALIBI_SLOPES = (2.0 ** (-8.0 * np.arange(1, N_Q_HEADS + 1, dtype=np.float32) / N_Q_HEADS)).astype(np.float32)


def _attn_bias(lbq, chunk, past_rows):
    w = WINDOW + lbq
    qi = np.arange(lbq)[:, None]
    lo = (qi // chunk) * chunk
    variants = []
    for lead in (past_rows, WINDOW):
        si = np.arange(w)[None, :] + (WINDOW - lead)
        dist = np.abs(WINDOW + qi - si).astype(np.float32)
        valid = (si >= lo) & (si < lo + WINDOW + chunk)
        per_head = [np.where(valid, -(_ALIBI_SLOPES[h] * dist), np.float32(NEG_INF)) for h in range(N_Q_HEADS)]
        variants.append(np.stack([np.concatenate([per_head[4 * kv + par], per_head[4 * kv + 2 + par]], axis=0)
                                  for kv in range(N_KV_HEADS) for par in range(2)]))
    return np.stack(variants).astype(np.float32)


def _attn_kernel(sink_ref, q_ref, kv_ref, bias_ref, o_ref, *, lbq, n_sub, past_rows):
    for sub in range(n_sub):
        _attn_block(slice(sub * lbq, (sub + 1) * lbq), pl.program_id(1) * n_sub + sub,
                    sink_ref, q_ref, kv_ref, bias_ref, o_ref, lbq, past_rows)


def _attn_block(rows, n, sink_ref, q_ref, kv_ref, bias_ref, o_ref, lbq, past_rows):
    w = WINDOW + lbq
    base = pl.multiple_of(jnp.maximum(n * lbq + (past_rows - WINDOW), 0), SUBLANES)
    variant = jnp.minimum(n, 1)
    kband = kv_ref[0, pl.ds(base, w), :KV_DIM]
    vband = kv_ref[0, pl.ds(base, w), KV_DIM:]
    first_pair = lax.broadcasted_iota(I32, (2 * lbq, 1), 0) < lbq
    lane = lax.broadcasted_iota(I32, (w, LANES), 1)
    low_half = lane < ATTN_HEAD_DIM

    for kv in range(N_KV_HEADS):
        pcols = slice((kv // 2) * LANES, (kv // 2 + 1) * LANES)
        kpair, vpair = kband[:, pcols], vband[:, pcols]
        kswap = pltpu.roll(kpair, ATTN_HEAD_DIM, axis=1)
        vswap = pltpu.roll(vpair, ATTN_HEAD_DIM, axis=1)
        in_low = kv % 2 == 0
        k_lo = jnp.where(low_half, kpair if in_low else kswap, 0.0).astype(BF16)
        k_hi = jnp.where(low_half, 0.0, kswap if in_low else kpair).astype(BF16)
        v_lo = jnp.where(low_half, vpair if in_low else vswap, 0.0).astype(BF16)
        v_hi = jnp.where(low_half, 0.0, vswap if in_low else vpair).astype(BF16)
        qs = q_ref[2 * kv:2 * kv + 2, 0, rows, :].reshape(2 * lbq, LANES)
        out = None
        for par, (kx, vx) in enumerate(((k_lo, v_lo), (k_hi, v_hi))):
            h0, h1 = 4 * kv + par, 4 * kv + 2 + par
            sink = jnp.where(first_pair, sink_ref[h0], sink_ref[h1])
            s = lax.dot_general(qs, kx, (((1,), (1,)), ((), ())), preferred_element_type=F32)
            s = s + bias_ref[variant, 2 * kv + par]
            m = jnp.maximum(jnp.max(s, axis=-1, keepdims=True), sink)
            p = jnp.exp(s - m)
            denom = jnp.sum(p, axis=-1, keepdims=True) + jnp.exp(sink - m)
            pn = (p / denom).astype(BF16)
            part = jnp.dot(pn, vx, preferred_element_type=F32)
            out = part if out is None else out + part
        o_ref[0, rows, (2 * kv) * LANES:(2 * kv + 1) * LANES] = out[:lbq].astype(o_ref.dtype)
        o_ref[0, rows, (2 * kv + 1) * LANES:(2 * kv + 2) * LANES] = out[lbq:].astype(o_ref.dtype)


def window_attention(q_pairs, kv_all, sinks, *, lbq, chunk):
    n_pairs, bsz, seq, _ = q_pairs.shape
    rows = kv_all.shape[1]
    past_rows = rows - seq
    assert past_rows in (0, WINDOW) and rows >= WINDOW + lbq
    bias = jnp.asarray(_attn_bias(lbq, chunk, past_rows))
    n_blocks = seq // lbq
    n_sub = 4 if n_blocks % 4 == 0 else (2 if n_blocks % 2 == 0 else 1)
    step = n_sub * lbq
    return pl.pallas_call(
        functools.partial(_attn_kernel, lbq=lbq, n_sub=n_sub, past_rows=past_rows),
        grid_spec=pltpu.PrefetchScalarGridSpec(
            num_scalar_prefetch=1,
            grid=(bsz, seq // step),
            in_specs=[pl.BlockSpec((n_pairs, 1, step, LANES), lambda b, n, s: (0, b, n, 0)),
                      pl.BlockSpec((1, rows, 2 * KV_DIM), lambda b, n, s: (b, 0, 0)),
                      _resident(bias.shape)],
            out_specs=pl.BlockSpec((1, step, D_MODEL), lambda b, n, s: (b, n, 0))),
        out_shape=jax.ShapeDtypeStruct((bsz, seq, D_MODEL), BF16),
        compiler_params=_cparams("parallel", "arbitrary"),
        name="window_attention",
    )(sinks.astype(F32), q_pairs, kv_all, bias)


def _trunk(x, conv_buf, ssm_state, k_past, v_past, past_valid, p):
    bsz, seq, d = x.shape
    t = bsz * seq
    x2 = x.reshape(t, d)

    seq_pad = -(-seq // SSD_BLOCK) * SSD_BLOCK
    x_in = x2 if seq_pad == seq else jnp.pad(x, ((0, 0), (0, seq_pad - seq), (0, 0))).reshape(bsz * seq_pad, d)
    z, xbc, dt_raw = in_proj(x_in, p['norm_mix'][0], p['w_z'], p['w_xbc'], p['w_dt'])
    xbc3 = xbc.reshape(bsz, seq_pad, CONV_DIM)
    new_conv = jnp.concatenate([conv_buf.astype(F32), xbc3[:, :seq]], axis=1)[:, seq:]
    yn, new_ssm = ssd_mixer(xbc3, dt_raw.reshape(bsz, seq_pad, LANES), z.reshape(bsz, seq_pad, D_INNER),
                            conv_buf.astype(F32), ssm_state, p['ssm_conv_w'], p['ssm_conv_b'],
                            p['ssm_dt_bias'], p['ssm_a_log'], p['ssm_d'], p['ssm_norm'],
                            None if seq_pad == seq else seq)
    yn = yn[:, :seq]

    h, kv, q_pairs = layer0_tail(yn.reshape(t, D_INNER), p['ssm_out_w'], x2, p['norm_ffn'][0],
                                 p['ffn_w_gate_up'], p['ffn_w_down'], p['kv_norm'], p['norm_mix'][1],
                                 p['w_kv'], p['w_q'])
    kv = kv.reshape(bsz, seq, 2 * KV_DIM)
    k_new, v_new = kv[..., :KV_DIM], kv[..., KV_DIM:]
    if past_valid:
        past = jnp.concatenate([k_past.reshape(bsz, WINDOW, KV_DIM), v_past.reshape(bsz, WINDOW, KV_DIM)], axis=2)
        kv_all = jnp.concatenate([past.astype(F32), kv], axis=1)
    else:
        kv_all = kv

    chunk = min(ATTN_CHUNK, seq)
    lbq = 2 * chunk if seq % (2 * chunk) == 0 else chunk
    q_pairs = q_pairs.reshape(D_MODEL // LANES, bsz, seq, LANES)
    o = window_attention(q_pairs, kv_all, p['attn_sinks'], lbq=lbq, chunk=chunk)

    y = attn_out_moe_final_norm(o.reshape(t, D_MODEL), p['w_o'], h, p['norm_ffn'][1], p['moe_router'],
                                p['moe_w_gate_up'], p['moe_w_down'], p['final_norm'])

    k4 = k_new.reshape(bsz, seq, N_KV_HEADS, ATTN_HEAD_DIM)
    v4 = v_new.reshape(bsz, seq, N_KV_HEADS, ATTN_HEAD_DIM)
    return y.reshape(bsz, seq, d), new_conv[None], new_ssm[None], k4, v4


def kernel(x_prompt, x_sample, state_conv, state_ssm, cache_k, cache_v, norm_mix, norm_ffn, ssm_in_w, ssm_conv_w,
           ssm_conv_b, ssm_dt_bias, ssm_a_log, ssm_d, ssm_norm, ssm_out_w, kv_norm, w_kv, w_q, attn_sinks, w_o,
           ffn_w_gate_up, ffn_w_down, moe_router, moe_w_gate_up, moe_w_down, final_norm):
    in_w = ssm_in_w[0]
    n_dt = N_SSM_HEADS
    p = {
        'norm_mix': norm_mix, 'norm_ffn': norm_ffn,
        'w_z': in_w[:, :D_INNER].astype(BF16),
        'w_xbc': in_w[:, D_INNER:D_INNER + CONV_DIM].astype(BF16),
        'w_dt': jnp.pad(in_w[:, D_INNER + CONV_DIM:], ((0, 0), (0, LANES - n_dt))).astype(BF16),
        'ssm_conv_w': ssm_conv_w[0], 'ssm_conv_b': ssm_conv_b[0], 'ssm_dt_bias': ssm_dt_bias[0],
        'ssm_a_log': ssm_a_log[0], 'ssm_d': ssm_d[0], 'ssm_norm': ssm_norm[0],
        'ssm_out_w': ssm_out_w[0].astype(BF16),
        'kv_norm': kv_norm, 'w_kv': w_kv.astype(BF16), 'w_q': w_q[0].astype(BF16),
        'attn_sinks': attn_sinks[0], 'w_o': w_o[0].astype(BF16),
        'ffn_w_gate_up': ffn_w_gate_up.astype(BF16), 'ffn_w_down': ffn_w_down.astype(BF16),
        'moe_router': jnp.pad(moe_router[0].astype(F32), ((0, 0), (0, LANES - N_EXPERTS))),
        'moe_w_gate_up': moe_w_gate_up[0].astype(BF16), 'moe_w_down': moe_w_down[0].astype(BF16),
        'final_norm': final_norm,
    }
    bsz, seq_p = x_prompt.shape[:2]
    dt = x_prompt.dtype
    zero_conv = jnp.zeros((bsz, CONV_WIDTH - 1, CONV_DIM), dt)
    zero_ssm = jnp.zeros((bsz, N_SSM_HEADS, SSM_HEAD_DIM, D_STATE), dt)
    zero_kv = jnp.zeros((bsz, WINDOW, N_KV_HEADS, ATTN_HEAD_DIM), dt)
    y_p, conv_p, ssm_p, k_p, v_p = _trunk(x_prompt, zero_conv, zero_ssm, zero_kv, zero_kv, False, p)
    keep = min(WINDOW, seq_p)
    y_s, conv_s, ssm_s, k_s, v_s = _trunk(x_sample, state_conv[0], state_ssm[0], cache_k, cache_v, True, p)
    return (y_p, y_s, conv_p, ssm_p, k_p[:, seq_p - keep:], v_p[:, seq_p - keep:], conv_s, ssm_s, k_s, v_s)
```

```python
import functools
import math

import jax
import jax.numpy as jnp
import numpy as np
from jax import lax
from jax.experimental import pallas as pl
from jax.experimental.pallas import tpu as pltpu

F32 = jnp.float32
BF16 = jnp.bfloat16
I32 = jnp.int32

D_MODEL = 1024
EPS = 1e-6
D_INNER = 2048
SSM_HEAD_DIM = 64
N_SSM_HEADS = 32
SSM_GROUPS = 4
HEADS_PER_GROUP = N_SSM_HEADS // SSM_GROUPS
D_STATE = 128
CONV_WIDTH = 4
CONV_DIM = D_INNER + 2 * SSM_GROUPS * D_STATE
ATTN_HEAD_DIM = 64
N_Q_HEADS = 16
N_KV_HEADS = 4
KV_DIM = N_KV_HEADS * ATTN_HEAD_DIM
WINDOW = 128
ATTN_CHUNK = 64
ATTN_SCALE = 1.0 / math.sqrt(ATTN_HEAD_DIM)
NEG_INF = -1e30
D_FF = 2816
N_EXPERTS = 8
TOP_K = 2

LANES = 128
SUBLANES = 8
VMEM_LIMIT_BYTES = 56 * 1024 * 1024
RUN_ALIGN = SUBLANES

SSD_BLOCK = 128
GROUP_COLS = HEADS_PER_GROUP * SSM_HEAD_DIM
FF_CHUNK = 512
EXPERT_ROW_TILE = 512
SMALL_EXPERT_ROW_TILE = 128


def _cparams(*sem):
    return pltpu.CompilerParams(dimension_semantics=sem, vmem_limit_bytes=VMEM_LIMIT_BYTES)


def _row_tile(t, pref=512):
    return pref if t % pref == 0 else t


def _resident(shape):
    return pl.BlockSpec(shape, lambda *_: (0,) * len(shape), pipeline_mode=pl.Buffered(1))


def _rms_scale(x):
    return lax.rsqrt(jnp.mean(x * x, axis=-1, keepdims=True) + EPS)


def _silu(x):
    half = 0.5 * x
    return half + half * jnp.tanh(half)


def _in_proj_kernel(x_ref, g_ref, wz_ref, wx_ref, wd_ref, z_ref, xbc_ref, dt_ref):
    x = x_ref[...]
    xn = (x * _rms_scale(x) * g_ref[...]).astype(BF16)
    z_ref[...] = jnp.dot(xn, wz_ref[...], preferred_element_type=F32)
    xbc_ref[...] = jnp.dot(xn, wx_ref[...], preferred_element_type=F32)
    dt_ref[...] = jnp.dot(xn, wd_ref[...], preferred_element_type=F32)


def in_proj(x, g, w_z, w_xbc, w_dt):
    t, k = x.shape
    tm = _row_tile(t)
    ws = (w_z, w_xbc, w_dt)
    return pl.pallas_call(
        _in_proj_kernel,
        grid=(t // tm,),
        in_specs=[pl.BlockSpec((tm, k), lambda i: (i, 0)), pl.BlockSpec((1, k), lambda i: (0, 0))]
        + [_resident(w.shape) for w in ws],
        out_specs=[pl.BlockSpec((tm, w.shape[1]), lambda i: (i, 0)) for w in ws],
        out_shape=[jax.ShapeDtypeStruct((t, w.shape[1]), F32) for w in ws],
        compiler_params=_cparams("parallel"),
        name="in_proj",
    )(x, g.reshape(1, k), *ws)


def _swiglu(xb, wgu_ref, wd_ref):
    acc = None
    for c0 in range(0, D_FF, FF_CHUNK):
        cw = min(FF_CHUNK, D_FF - c0)
        gg = jnp.dot(xb, wgu_ref[0, :, c0:c0 + cw], preferred_element_type=F32)
        uu = jnp.dot(xb, wgu_ref[0, :, D_FF + c0:D_FF + c0 + cw], preferred_element_type=F32)
        act = (_silu(gg) * uu).astype(BF16)
        part = jnp.dot(act, wd_ref[0, c0:c0 + cw, :], preferred_element_type=F32)
        acc = part if acc is None else acc + part
    return acc


def _layer0_tail_kernel(a_ref, wo_ref, r_ref, gf_ref, wgu_ref, wd_ref, gkv_ref, gq_ref, wkv_ref, wq_ref,
                        h_ref, kv_ref, q_ref):
    h1 = r_ref[...] + jnp.dot(a_ref[...], wo_ref[...], preferred_element_type=F32)
    xb = (h1 * _rms_scale(h1) * gf_ref[...]).astype(BF16)
    h2 = h1 + _swiglu(xb, wgu_ref, wd_ref)
    h_ref[...] = h2
    xs = h2 * _rms_scale(h2)
    kv_ref[...] = jnp.dot((xs * gkv_ref[...]).astype(BF16), wkv_ref[...], preferred_element_type=F32)
    q = jnp.dot((xs * gq_ref[...]).astype(BF16), wq_ref[...], preferred_element_type=F32)
    q = (q * ATTN_SCALE).astype(q_ref.dtype)
    for p in range(q_ref.shape[0]):
        q_ref[p] = q[:, p * LANES:(p + 1) * LANES]


def layer0_tail(a, w_out, r, g_ffn, w_gu, w_down, g_kv, g_q, w_kv, w_q):
    t, k = a.shape
    d = w_out.shape[1]
    tm = _row_tile(t)
    n_pairs = w_q.shape[1] // LANES

    def gain():
        return pl.BlockSpec((1, d), lambda i: (0, 0))

    return pl.pallas_call(
        _layer0_tail_kernel,
        grid=(t // tm,),
        in_specs=[pl.BlockSpec((tm, k), lambda i: (i, 0)), _resident(w_out.shape),
                  pl.BlockSpec((tm, d), lambda i: (i, 0)), gain(),
                  _resident(w_gu.shape), _resident(w_down.shape), gain(), gain(),
                  _resident(w_kv.shape), _resident(w_q.shape)],
        out_specs=[pl.BlockSpec((tm, d), lambda i: (i, 0)),
                   pl.BlockSpec((tm, w_kv.shape[1]), lambda i: (i, 0)),
                   pl.BlockSpec((n_pairs, tm, LANES), lambda i: (0, i, 0))],
        out_shape=[jax.ShapeDtypeStruct((t, d), F32),
                   jax.ShapeDtypeStruct((t, w_kv.shape[1]), F32),
                   jax.ShapeDtypeStruct((n_pairs, t, LANES), BF16)],
        compiler_params=_cparams("parallel"),
        name="layer0_tail",
    )(a, w_out, r, g_ffn.reshape(1, d), w_gu, w_down, g_kv.reshape(1, d), g_q.reshape(1, d), w_kv, w_q)


def _expert_ffn_kernel(te_ref, na_ref, x_ref, wgu_ref, wd_ref, o_ref):
    active = pl.program_id(0) < na_ref[0]

    @pl.when(active)
    def _():
        o_ref[...] = _swiglu(x_ref[...].astype(BF16), wgu_ref, wd_ref)

    @pl.when(jnp.logical_not(active))
    def _():
        o_ref[...] = jnp.zeros_like(o_ref)


def expert_ffn(xs, tile_expert, n_active, w_gu, w_down, tm):
    rows, d = xs.shape
    return pl.pallas_call(
        _expert_ffn_kernel,
        grid_spec=pltpu.PrefetchScalarGridSpec(
            num_scalar_prefetch=2,
            grid=(rows // tm,),
            in_specs=[pl.BlockSpec((tm, d), lambda i, te, na: (jnp.minimum(i, na[0] - 1), 0)),
                      pl.BlockSpec((1, d, 2 * D_FF), lambda i, te, na: (te[i], 0, 0)),
                      pl.BlockSpec((1, D_FF, d), lambda i, te, na: (te[i], 0, 0))],
            out_specs=pl.BlockSpec((tm, d), lambda i, te, na: (i, 0))),
        out_shape=jax.ShapeDtypeStruct((rows, d), F32),
        compiler_params=_cparams("arbitrary"),
        name="expert_ffn",
    )(tile_expert, n_active, xs, w_gu, w_down)


def _router_kernel(a_ref, wo_ref, r_ref, g_ref, wr_ref, h_ref, gate_ref, slot_ref, meta_ref, cnt_ref):
    x = r_ref[...] + jnp.dot(a_ref[...], wo_ref[...], preferred_element_type=F32)
    h_ref[...] = x
    tm = x.shape[0]
    xn = x * _rms_scale(x) * g_ref[...]
    x_hi = xn.astype(BF16)
    x_lo = (xn - x_hi.astype(F32)).astype(BF16)
    prod = jnp.dot(jnp.concatenate([x_hi, x_lo], axis=0), wr_ref[...], preferred_element_type=F32)
    hi_rows, lo_rows = prod[:tm], prod[tm:]
    logits = hi_rows + pltpu.roll(hi_rows, LANES - N_EXPERTS, axis=1) + lo_rows
    lane = lax.broadcasted_iota(I32, logits.shape, 1)
    logits = jnp.where(lane < N_EXPERTS, logits, -jnp.inf)
    m1 = jnp.max(logits, axis=-1, keepdims=True)
    i1 = jnp.min(jnp.where(logits == m1, lane, LANES), axis=-1, keepdims=True)
    rest = jnp.where(lane == i1, -jnp.inf, logits)
    m2 = jnp.max(rest, axis=-1, keepdims=True)
    i2 = jnp.min(jnp.where(rest == m2, lane, LANES), axis=-1, keepdims=True)
    e2 = jnp.exp(m2 - m1)
    w1 = 1.0 / (1.0 + e2)
    w2 = e2 / (1.0 + e2)
    gate_ref[...] = jnp.where(lane == 0, w1, jnp.where(lane == 1, w2, 0.0))
    pick0 = lane == i1
    pick1 = lane == i2
    chosen = jnp.where(jnp.logical_or(pick0, pick1), 1.0, 0.0)

    @pl.when(pl.program_id(0) == 0)
    def _():
        cnt_ref[...] = jnp.zeros_like(cnt_ref)

    ri = lax.broadcasted_iota(I32, (tm, tm), 0)
    ci = lax.broadcasted_iota(I32, (tm, tm), 1)
    before = jnp.where(ri > ci, 1.0, 0.0).astype(BF16)
    rank = jnp.dot(before, chosen.astype(BF16), preferred_element_type=F32)
    n_pad = jnp.ceil(jnp.sum(chosen, axis=0, keepdims=True) * (1.0 / RUN_ALIGN)) * RUN_ALIGN
    ei = lax.broadcasted_iota(I32, (LANES, LANES), 0)
    ej = lax.broadcasted_iota(I32, (LANES, LANES), 1)
    earlier = jnp.where(ei < ej, 1.0, 0.0).astype(BF16)
    n_pad8 = jnp.broadcast_to(n_pad, (SUBLANES, LANES))
    local = jnp.dot(n_pad8.astype(BF16), earlier, preferred_element_type=F32)[0:1, :]
    slot = local + rank
    s0 = jnp.sum(jnp.where(pick0, slot, 0.0), axis=-1, keepdims=True)
    s1 = jnp.sum(jnp.where(pick1, slot, 0.0), axis=-1, keepdims=True)
    slot_ref[...] = jnp.where(lane == 0, s0, jnp.where(lane == 1, s1, 0.0))
    row = lax.broadcasted_iota(I32, (SUBLANES, LANES), 0)
    so_far = cnt_ref[0:1, :]
    meta = jnp.where(row == 0, so_far, jnp.where(row == 1, local, jnp.where(row == 2, n_pad, 0.0)))
    meta_ref[...] = meta.astype(I32)
    cnt_ref[...] += n_pad8


def out_proj_router(a, w_o, r, g, w_router):
    t, k = a.shape
    d = w_o.shape[1]
    tm = _row_tile(t)
    w_hi = w_router.astype(BF16)
    w_lo = (w_router - w_hi.astype(F32)).astype(BF16)
    w_packed = jnp.concatenate([w_hi[:, :N_EXPERTS], w_lo[:, :N_EXPERTS],
                                jnp.zeros((d, LANES - 2 * N_EXPERTS), BF16)], axis=1)
    return pl.pallas_call(
        _router_kernel,
        grid=(t // tm,),
        in_specs=[pl.BlockSpec((tm, k), lambda i: (i, 0)),
                  _resident(w_o.shape),
                  pl.BlockSpec((tm, d), lambda i: (i, 0)),
                  pl.BlockSpec((1, d), lambda i: (0, 0)),
                  _resident((d, LANES))],
        out_specs=[pl.BlockSpec((tm, d), lambda i: (i, 0)),
                   pl.BlockSpec((tm, LANES), lambda i: (i, 0)),
                   pl.BlockSpec((tm, LANES), lambda i: (i, 0)),
                   pl.BlockSpec((SUBLANES, LANES), lambda i: (i, 0)),
                   pl.BlockSpec((SUBLANES, LANES), lambda i: (0, 0))],
        out_shape=[jax.ShapeDtypeStruct((t, d), F32),
                   jax.ShapeDtypeStruct((t, LANES), F32),
                   jax.ShapeDtypeStruct((t, LANES), F32),
                   jax.ShapeDtypeStruct((SUBLANES * (t // tm), LANES), I32),
                   jax.ShapeDtypeStruct((SUBLANES, LANES), F32)],
        compiler_params=_cparams("arbitrary"),
        name="out_proj_router",
    )(a, w_o, r, g.reshape(1, d), w_packed)


META_ROWS = 3


def _staging_rows(tt):
    need = TOP_K * tt + N_EXPERTS * (RUN_ALIGN - 1)
    return -(-need // LANES) * LANES


def _run_copies(meta_ref, group_ref, tile, staging, sorted_hbm, sem, to_sorted):
    first = tile * (META_ROWS * N_EXPERTS)
    out = []
    for e in range(N_EXPERTS):
        sorted_start = pl.multiple_of(group_ref[e] + meta_ref[first + e], RUN_ALIGN)
        local_start = pl.multiple_of(meta_ref[first + N_EXPERTS + e], RUN_ALIGN)
        n = pl.multiple_of(meta_ref[first + 2 * N_EXPERTS + e], RUN_ALIGN)
        a, b = staging.at[pl.ds(local_start, n)], sorted_hbm.at[pl.ds(sorted_start, n)]
        out.append((n, pltpu.make_async_copy(a, b, sem) if to_sorted else pltpu.make_async_copy(b, a, sem)))
    return out


def _start(copies):
    for n, copy in copies:
        pl.when(n > 0)(copy.start)


def _wait(copies):
    for n, copy in copies:
        pl.when(n > 0)(copy.wait)


def _start_then_wait(copies):
    _start(copies)
    _wait(copies)


def _scatter_kernel(meta_ref, group_ref, tail_ref, x_ref, g_ref, slot_ref, xs_hbm, stage_scr, zero_scr, sem):
    i = pl.program_id(0)
    last = pl.num_programs(0) - 1
    buf = i % 2
    x = x_ref[...]
    xn = (x * _rms_scale(x) * g_ref[...]).astype(BF16)
    tt, rt = x.shape[0], stage_scr.shape[1]
    slot_t = slot_ref[...].T.astype(I32)
    r = lax.broadcasted_iota(I32, (rt, tt), 0)
    place = jnp.where(jnp.logical_or(r == slot_t[0:1, :], r == slot_t[1:2, :]), 1.0, 0.0).astype(BF16)
    stage_scr[buf] = jnp.dot(place, xn, preferred_element_type=F32)
    mine = _run_copies(meta_ref, group_ref, i, stage_scr.at[buf], xs_hbm, sem.at[buf], to_sorted=True)
    _start(mine)

    @pl.when(i > 0)
    def _():
        _wait(_run_copies(meta_ref, group_ref, i - 1, stage_scr.at[1 - buf], xs_hbm, sem.at[1 - buf],
                          to_sorted=True))

    @pl.when(i == last)
    def _():
        _wait(mine)
        zero_scr[...] = jnp.zeros_like(zero_scr)
        tails = []
        for e in range(N_EXPERTS):
            start = pl.multiple_of(tail_ref[e], RUN_ALIGN)
            n = pl.multiple_of(tail_ref[N_EXPERTS + e], RUN_ALIGN)
            tails.append((n, pltpu.make_async_copy(zero_scr.at[pl.ds(0, n)], xs_hbm.at[pl.ds(start, n)], sem.at[0])))
        _start_then_wait(tails)

        tm = zero_scr.shape[0]

        def clear_tile(j, carry):
            copy = pltpu.make_async_copy(zero_scr, xs_hbm.at[pl.ds(pl.multiple_of(j * tm, tm), tm)], sem.at[0])
            copy.start()
            copy.wait()
            return carry

        lax.fori_loop(tail_ref[2 * N_EXPERTS], xs_hbm.shape[0] // tm, clear_tile, 0)


def scatter_rows(x, g, slot, meta, group_starts, tails, n_rows, tm):
    t, d = x.shape
    tt = _row_tile(t)
    return pl.pallas_call(
        _scatter_kernel,
        grid_spec=pltpu.PrefetchScalarGridSpec(
            num_scalar_prefetch=3,
            grid=(t // tt,),
            in_specs=[pl.BlockSpec((tt, d), lambda i, *_: (i, 0)),
                      pl.BlockSpec((1, d), lambda i, *_: (0, 0)),
                      pl.BlockSpec((tt, LANES), lambda i, *_: (i, 0))],
            out_specs=pl.BlockSpec(memory_space=pl.ANY),
            scratch_shapes=[pltpu.VMEM((2, _staging_rows(tt), d), F32), pltpu.VMEM((tm, d), F32),
                            pltpu.SemaphoreType.DMA((2,))]),
        out_shape=jax.ShapeDtypeStruct((n_rows, d), F32),
        compiler_params=_cparams("arbitrary"),
        name="scatter_rows",
    )(meta, group_starts, tails, x, g.reshape(1, d), slot)


def _combine_kernel(meta_ref, group_ref, h_ref, gate_ref, slot_ref, fg_ref, ys_hbm, o_ref, stage_scr, sem):
    i = pl.program_id(0)
    buf = i % 2

    def fetch(tile, b):
        return _run_copies(meta_ref, group_ref, tile, stage_scr.at[b], ys_hbm, sem.at[b], to_sorted=False)

    @pl.when(i == 0)
    def _():
        stage_scr[...] = jnp.zeros_like(stage_scr)
        _start(fetch(0, 0))

    @pl.when(i + 1 < pl.num_programs(0))
    def _():
        _start(fetch(i + 1, 1 - buf))

    tt, rt = h_ref.shape[0], stage_scr.shape[1]
    slot = slot_ref[...].astype(I32)
    gate = gate_ref[...]
    c = lax.broadcasted_iota(I32, (tt, rt), 1)
    weight = (jnp.where(c == slot[:, 0:1], gate[:, 0:1], 0.0)
              + jnp.where(c == slot[:, 1:2], gate[:, 1:2], 0.0))
    _wait(fetch(i, buf))
    moe = jnp.dot(weight.astype(BF16), stage_scr[buf].astype(BF16), preferred_element_type=F32)
    y = h_ref[...] + moe
    o_ref[...] = y * _rms_scale(y) * fg_ref[...]


def combine_rows(h, gates, slot, meta, group_starts, final_gain, ys):
    t, d = h.shape
    tt = _row_tile(t)
    return pl.pallas_call(
        _combine_kernel,
        grid_spec=pltpu.PrefetchScalarGridSpec(
            num_scalar_prefetch=2,
            grid=(t // tt,),
            in_specs=[pl.BlockSpec((tt, d), lambda i, *_: (i, 0)),
                      pl.BlockSpec((tt, LANES), lambda i, *_: (i, 0)),
                      pl.BlockSpec((tt, LANES), lambda i, *_: (i, 0)),
                      pl.BlockSpec((1, d), lambda i, *_: (0, 0)),
                      pl.BlockSpec(memory_space=pl.ANY)],
            out_specs=pl.BlockSpec((tt, d), lambda i, *_: (i, 0)),
            scratch_shapes=[pltpu.VMEM((2, _staging_rows(tt), d), F32), pltpu.SemaphoreType.DMA((2,))]),
        out_shape=jax.ShapeDtypeStruct((t, d), F32),
        compiler_params=_cparams("arbitrary"),
        name="combine_rows",
    )(meta, group_starts, h, gates, slot, final_gain.reshape(1, d), ys)


def attn_out_moe_final_norm(o, w_o, r, g, w_router, w_gu, w_down, final_gain):
    t, d = r.shape
    tt = _row_tile(t)
    n_tok_tiles = t // tt
    tm = EXPERT_ROW_TILE if t >= N_EXPERTS * EXPERT_ROW_TILE else SMALL_EXPERT_ROW_TILE
    worst = TOP_K * t + n_tok_tiles * N_EXPERTS * (RUN_ALIGN - 1) + N_EXPERTS * tm
    n_rows = -(-worst // tm) * tm
    h, gates, slot, meta, counts = out_proj_router(o, w_o, r, g, w_router)

    cnt = counts[0, :N_EXPERTS].astype(I32)
    tiles = (cnt + tm - 1) // tm
    ends = jnp.cumsum(tiles)
    starts = ((ends - tiles) * tm).astype(I32)
    n_active = ends[-1:]
    tile_idx = jnp.arange(n_rows // tm, dtype=I32)
    owner = jnp.sum(tile_idx[:, None] >= ends[None, :], axis=1).astype(I32)
    last_owner = jnp.sum(n_active - 1 >= ends).astype(I32)
    tile_expert = jnp.where(tile_idx < n_active, owner, last_owner)
    tails = jnp.concatenate([starts + cnt, tiles * tm - cnt, n_active]).astype(I32)

    meta_flat = meta.reshape(n_tok_tiles, SUBLANES, LANES)[:, :META_ROWS, :N_EXPERTS].reshape(-1)
    xs = scatter_rows(h, g, slot, meta_flat, starts, tails, n_rows, tm)
    ys = expert_ffn(xs, tile_expert, n_active, w_gu, w_down, tm)
    return combine_rows(h, gates, slot, meta_flat, starts, final_gain, ys)


def _split3(v):
    hi = v.astype(BF16)
    r1 = v - hi.astype(F32)
    mid = r1.astype(BF16)
    lo = (r1 - mid.astype(F32)).astype(BF16)
    return hi, mid, lo


def _ssd_block(rows, blk, xbc_ref, dt_ref, z_ref, cw_ref, cb_ref, dtb_ref, alog_ref, dskip_ref, gn_ref,
               expand_ref, yn_ref, h_scr, xext_scr, seq_valid):
    q = SSD_BLOCK
    halo = SUBLANES

    xext_scr[halo:halo + q, :] = xbc_ref[0, rows, :]
    xext = xext_scr[...]
    conv = cb_ref[...] + xext[halo:] * cw_ref[CONV_WIDTH - 1:CONV_WIDTH, :]
    for k in range(CONV_WIDTH - 1):
        delayed = pltpu.roll(xext, CONV_WIDTH - 1 - k, axis=0)
        conv = conv + delayed[halo:] * cw_ref[k:k + 1, :]
    xext_scr[0:halo, :] = xext_scr[q:q + halo, :]
    xbc = _silu(conv)
    xs = xbc[:, :D_INNER]
    b_all = xbc[:, D_INNER:D_INNER + SSM_GROUPS * D_STATE]
    c_all = xbc[:, D_INNER + SSM_GROUPS * D_STATE:]

    v = dt_ref[0, rows, :] + dtb_ref[...]
    dt = jnp.maximum(v, 0.0) + jnp.log1p(jnp.exp(-jnp.abs(v)))
    row = lax.broadcasted_iota(I32, (q, LANES), 0)
    if seq_valid is not None:
        dt = jnp.where(row + blk * q < seq_valid, dt, 0.0)
    a_neg = -jnp.exp(alog_ref[...])
    dta = dt * a_neg

    ti = lax.broadcasted_iota(I32, (q, q), 0)
    si = lax.broadcasted_iota(I32, (q, q), 1)
    causal = ti >= si
    tri = jnp.where(causal, 1.0, 0.0).astype(BF16)
    a_cs = sum(jnp.dot(tri, part, preferred_element_type=F32) for part in _split3(dta))
    a_last = a_cs[q - 1:q, :]
    a_cs_t = a_cs.T

    stacked = jnp.concatenate([dt, jnp.exp(a_cs), jnp.exp(a_last - a_cs),
                               jnp.broadcast_to(jnp.exp(a_last), (SUBLANES, LANES))], axis=0)
    s_hi = stacked.astype(BF16)
    s_lo = (stacked - s_hi.astype(F32)).astype(BF16)
    expand = expand_ref[...]
    expanded = (jnp.dot(s_hi, expand, preferred_element_type=F32)
                + jnp.dot(s_lo, expand, preferred_element_type=F32))
    dt_e = expanded[0:q]
    decay_in_e = expanded[q:2 * q]
    decay_out_e = expanded[2 * q:3 * q]
    decay_blk_e = expanded[3 * q:3 * q + 1]

    xdt = xs * dt_e
    xdt_bf = xdt.astype(BF16)
    wx = (xdt * decay_out_e).astype(BF16)
    low_half = lax.broadcasted_iota(I32, (q, LANES), 1) < SSM_HEAD_DIM

    y_groups = []
    for g in range(SSM_GROUPS):
        bg = b_all[:, g * D_STATE:(g + 1) * D_STATE]
        cg = c_all[:, g * D_STATE:(g + 1) * D_STATE].astype(BF16)
        bg_bf = bg.astype(BF16)
        cbm = lax.dot_general(cg, bg_bf, (((1,), (1,)), ((), ())), preferred_element_type=F32)
        cols = slice(g * GROUP_COLS, (g + 1) * GROUP_COLS)
        h_g = h_scr[g]
        y_g = jnp.dot(cg, h_g.astype(BF16), preferred_element_type=F32) * decay_in_e[:, cols]
        pair_out = []
        for pr in range(HEADS_PER_GROUP // 2):
            parts = []
            for par in range(2):
                h = g * HEADS_PER_GROUP + 2 * pr + par
                seg = (jnp.broadcast_to(a_cs[:, h:h + 1], (q, q))
                       - jnp.broadcast_to(a_cs_t[h:h + 1, :], (q, q)))
                m = (cbm * jnp.exp(jnp.where(causal, seg, -jnp.inf))).astype(BF16)
                pcols = slice((h // 2) * LANES, (h // 2 + 1) * LANES)
                parts.append(jnp.dot(m, xdt_bf[:, pcols], preferred_element_type=F32))
            pair_out.append(jnp.where(low_half, parts[0], parts[1]))
        y_groups.append(y_g + jnp.concatenate(pair_out, axis=1))
        h_scr[g] = (h_g * decay_blk_e[:, cols]
                    + jnp.dot(bg.T.astype(BF16), wx[:, cols], preferred_element_type=F32))

    y = jnp.concatenate(y_groups, axis=1) + dskip_ref[...] * xs
    yz = y * _silu(z_ref[0, rows, :])
    yn_ref[0, rows, :] = (yz * _rms_scale(yz) * gn_ref[...]).astype(yn_ref.dtype)


def _ssd_kernel(xbc_ref, dt_ref, z_ref, cbuf_ref, h0_ref, cw_ref, cb_ref, dtb_ref, alog_ref,
                dskip_ref, gn_ref, expand_ref, yn_ref, hfin_ref, h_scr, xext_scr, *, seq_valid, n_sub):
    c = pl.program_id(1)

    @pl.when(c == 0)
    def _():
        h_scr[...] = h0_ref[0]
        xext_scr[0:SUBLANES, :] = cbuf_ref[0]

    for sub in range(n_sub):
        _ssd_block(slice(sub * SSD_BLOCK, (sub + 1) * SSD_BLOCK), c * n_sub + sub, xbc_ref, dt_ref, z_ref,
                   cw_ref, cb_ref, dtb_ref, alog_ref, dskip_ref, gn_ref, expand_ref, yn_ref, h_scr, xext_scr,
                   seq_valid)

    @pl.when(c == pl.num_programs(1) - 1)
    def _():
        hfin_ref[0] = h_scr[...]


def _state_to_kernel_layout(h):
    b = h.shape[0]
    h = h.reshape(b, SSM_GROUPS, HEADS_PER_GROUP, SSM_HEAD_DIM, D_STATE)
    return h.transpose(0, 1, 4, 2, 3).reshape(b, SSM_GROUPS, D_STATE, GROUP_COLS)


def _state_from_kernel_layout(h):
    b = h.shape[0]
    h = h.reshape(b, SSM_GROUPS, D_STATE, HEADS_PER_GROUP, SSM_HEAD_DIM)
    return h.transpose(0, 1, 3, 4, 2).reshape(b, N_SSM_HEADS, SSM_HEAD_DIM, D_STATE)


def _pad_lanes(v):
    return jnp.pad(v.astype(F32), (0, LANES - v.shape[0])).reshape(1, LANES)


def ssd_mixer(xbc, dt_raw, z, conv_buf, h0, conv_w, conv_b, dt_bias, a_log, d_skip, gnorm, seq_valid):
    bsz, seq_pad, _ = xbc.shape
    q = SSD_BLOCK
    assert seq_pad % q == 0
    n_blocks = seq_pad // q
    n_sub = 4 if n_blocks % 4 == 0 else (2 if n_blocks % 2 == 0 else 1)
    rows_per_step = n_sub * q
    cbuf = jnp.pad(conv_buf, ((0, 0), (SUBLANES - (CONV_WIDTH - 1), 0), (0, 0)))
    expand = (jnp.arange(D_INNER)[None, :] // SSM_HEAD_DIM == jnp.arange(LANES)[:, None]).astype(BF16)
    d_e = jnp.repeat(d_skip.astype(F32), SSM_HEAD_DIM).reshape(1, D_INNER)
    state_spec = pl.BlockSpec((1, SSM_GROUPS, D_STATE, GROUP_COLS), lambda b, c: (b, 0, 0, 0))

    def const(shape):
        return pl.BlockSpec(shape, lambda b, c: (0,) * len(shape))

    yn, h_fin = pl.pallas_call(
        functools.partial(_ssd_kernel, seq_valid=seq_valid, n_sub=n_sub),
        grid=(bsz, seq_pad // rows_per_step),
        in_specs=[pl.BlockSpec((1, rows_per_step, CONV_DIM), lambda b, c: (b, c, 0)),
                  pl.BlockSpec((1, rows_per_step, LANES), lambda b, c: (b, c, 0)),
                  pl.BlockSpec((1, rows_per_step, D_INNER), lambda b, c: (b, c, 0)),
                  pl.BlockSpec((1, SUBLANES, CONV_DIM), lambda b, c: (b, 0, 0)),
                  state_spec,
                  const((CONV_WIDTH, CONV_DIM)), const((1, CONV_DIM)), const((1, LANES)), const((1, LANES)),
                  const((1, D_INNER)), const((1, D_INNER)), const((LANES, D_INNER))],
        out_specs=[pl.BlockSpec((1, rows_per_step, D_INNER), lambda b, c: (b, c, 0)), state_spec],
        out_shape=[jax.ShapeDtypeStruct((bsz, seq_pad, D_INNER), BF16),
                   jax.ShapeDtypeStruct((bsz, SSM_GROUPS, D_STATE, GROUP_COLS), F32)],
        scratch_shapes=[pltpu.VMEM((SSM_GROUPS, D_STATE, GROUP_COLS), F32),
                        pltpu.VMEM((q + SUBLANES, CONV_DIM), F32)],
        compiler_params=_cparams("parallel", "arbitrary"),
        name="ssd_mixer",
    )(xbc, dt_raw, z, cbuf, _state_to_kernel_layout(h0.astype(F32)), conv_w.astype(F32),
      conv_b.astype(F32).reshape(1, CONV_DIM), _pad_lanes(dt_bias), _pad_lanes(a_log), d_e,
      gnorm.astype(F32).reshape(1, D_INNER), expand)
    return yn, _state_from_kernel_layout(h_fin)


_ALIBI_SLOPES = (2.0 ** (-8.0 * np.arange(1, N_Q_HEADS + 1, dtype=np.float32) / N_Q_HEADS)).astype(np.float32)


def _attn_bias(lbq, chunk, past_rows):
    w = WINDOW + lbq
    qi = np.arange(lbq)[:, None]
    lo = (qi // chunk) * chunk
    variants = []
    for lead in (past_rows, WINDOW):
        si = np.arange(w)[None, :] + (WINDOW - lead)
        dist = np.abs(WINDOW + qi - si).astype(np.float32)
        valid = (si >= lo) & (si < lo + WINDOW + chunk)
        per_head = [np.where(valid, -(_ALIBI_SLOPES[h] * dist), np.float32(NEG_INF)) for h in range(N_Q_HEADS)]
        variants.append(np.stack([np.concatenate([per_head[4 * kv + par], per_head[4 * kv + 2 + par]], axis=0)
                                  for kv in range(N_KV_HEADS) for par in range(2)]))
    return np.stack(variants).astype(np.float32)


def _attn_kernel(sink_ref, q_ref, kv_ref, bias_ref, o_ref, *, lbq, n_sub, past_rows):
    for sub in range(n_sub):
        _attn_block(slice(sub * lbq, (sub + 1) * lbq), pl.program_id(1) * n_sub + sub,
                    sink_ref, q_ref, kv_ref, bias_ref, o_ref, lbq, past_rows)


def _attn_block(rows, n, sink_ref, q_ref, kv_ref, bias_ref, o_ref, lbq, past_rows):
    w = WINDOW + lbq
    base = pl.multiple_of(jnp.maximum(n * lbq + (past_rows - WINDOW), 0), SUBLANES)
    variant = jnp.minimum(n, 1)
    kband = kv_ref[0, pl.ds(base, w), :KV_DIM]
    vband = kv_ref[0, pl.ds(base, w), KV_DIM:]
    first_pair = lax.broadcasted_iota(I32, (2 * lbq, 1), 0) < lbq
    lane = lax.broadcasted_iota(I32, (w, LANES), 1)
    low_half = lane < ATTN_HEAD_DIM

    for kv in range(N_KV_HEADS):
        pcols = slice((kv // 2) * LANES, (kv // 2 + 1) * LANES)
        kpair, vpair = kband[:, pcols], vband[:, pcols]
        kswap = pltpu.roll(kpair, ATTN_HEAD_DIM, axis=1)
        vswap = pltpu.roll(vpair, ATTN_HEAD_DIM, axis=1)
        in_low = kv % 2 == 0
        k_lo = jnp.where(low_half, kpair if in_low else kswap, 0.0).astype(BF16)
        k_hi = jnp.where(low_half, 0.0, kswap if in_low else kpair).astype(BF16)
        v_lo = jnp.where(low_half, vpair if in_low else vswap, 0.0).astype(BF16)
        v_hi = jnp.where(low_half, 0.0, vswap if in_low else vpair).astype(BF16)
        qs = q_ref[2 * kv:2 * kv + 2, 0, rows, :].reshape(2 * lbq, LANES)
        out = None
        for par, (kx, vx) in enumerate(((k_lo, v_lo), (k_hi, v_hi))):
            h0, h1 = 4 * kv + par, 4 * kv + 2 + par
            sink = jnp.where(first_pair, sink_ref[h0], sink_ref[h1])
            s = lax.dot_general(qs, kx, (((1,), (1,)), ((), ())), preferred_element_type=F32)
            s = s + bias_ref[variant, 2 * kv + par]
            m = jnp.maximum(jnp.max(s, axis=-1, keepdims=True), sink)
            p = jnp.exp(s - m)
            denom = jnp.sum(p, axis=-1, keepdims=True) + jnp.exp(sink - m)
            pn = (p / denom).astype(BF16)
            part = jnp.dot(pn, vx, preferred_element_type=F32)
            out = part if out is None else out + part
        o_ref[0, rows, (2 * kv) * LANES:(2 * kv + 1) * LANES] = out[:lbq].astype(o_ref.dtype)
        o_ref[0, rows, (2 * kv + 1) * LANES:(2 * kv + 2) * LANES] = out[lbq:].astype(o_ref.dtype)


def window_attention(q_pairs, kv_all, sinks, *, lbq, chunk):
    n_pairs, bsz, seq, _ = q_pairs.shape
    rows = kv_all.shape[1]
    past_rows = rows - seq
    assert past_rows in (0, WINDOW) and rows >= WINDOW + lbq
    bias = jnp.asarray(_attn_bias(lbq, chunk, past_rows))
    n_blocks = seq // lbq
    n_sub = 4 if n_blocks % 4 == 0 else (2 if n_blocks % 2 == 0 else 1)
    step = n_sub * lbq
    return pl.pallas_call(
        functools.partial(_attn_kernel, lbq=lbq, n_sub=n_sub, past_rows=past_rows),
        grid_spec=pltpu.PrefetchScalarGridSpec(
            num_scalar_prefetch=1,
            grid=(bsz, seq // step),
            in_specs=[pl.BlockSpec((n_pairs, 1, step, LANES), lambda b, n, s: (0, b, n, 0)),
                      pl.BlockSpec((1, rows, 2 * KV_DIM), lambda b, n, s: (b, 0, 0)),
                      _resident(bias.shape)],
            out_specs=pl.BlockSpec((1, step, D_MODEL), lambda b, n, s: (b, n, 0))),
        out_shape=jax.ShapeDtypeStruct((bsz, seq, D_MODEL), BF16),
        compiler_params=_cparams("parallel", "arbitrary"),
        name="window_attention",
    )(sinks.astype(F32), q_pairs, kv_all, bias)


def _trunk(x, conv_buf, ssm_state, k_past, v_past, past_valid, kv_keep, p):
    bsz, seq, d = x.shape
    t = bsz * seq
    x2 = x.reshape(t, d)

    seq_pad = -(-seq // SSD_BLOCK) * SSD_BLOCK
    x_in = x2 if seq_pad == seq else jnp.pad(x, ((0, 0), (0, seq_pad - seq), (0, 0))).reshape(bsz * seq_pad, d)
    z, xbc, dt_raw = in_proj(x_in, p['norm_mix'][0], p['w_z'], p['w_xbc'], p['w_dt'])
    xbc3 = xbc.reshape(bsz, seq_pad, CONV_DIM)
    new_conv = jnp.concatenate([conv_buf.astype(F32), xbc3[:, :seq]], axis=1)[:, seq:]
    yn, new_ssm = ssd_mixer(xbc3, dt_raw.reshape(bsz, seq_pad, LANES), z.reshape(bsz, seq_pad, D_INNER),
                            conv_buf.astype(F32), ssm_state, p['ssm_conv_w'], p['ssm_conv_b'],
                            p['ssm_dt_bias'], p['ssm_a_log'], p['ssm_d'], p['ssm_norm'],
                            None if seq_pad == seq else seq)
    yn = yn[:, :seq]

    h, kv, q_pairs = layer0_tail(yn.reshape(t, D_INNER), p['ssm_out_w'], x2, p['norm_ffn'][0],
                                 p['ffn_w_gate_up'], p['ffn_w_down'], p['kv_norm'], p['norm_mix'][1],
                                 p['w_kv'], p['w_q'])
    kv = kv.reshape(bsz, seq, 2 * KV_DIM)
    if past_valid:
        past = jnp.concatenate([k_past.reshape(bsz, WINDOW, KV_DIM), v_past.reshape(bsz, WINDOW, KV_DIM)], axis=2)
        kv_all = jnp.concatenate([past.astype(F32), kv], axis=1)
    else:
        kv_all = kv

    chunk = min(ATTN_CHUNK, seq)
    lbq = 2 * chunk if seq % (2 * chunk) == 0 else chunk
    q_pairs = q_pairs.reshape(D_MODEL // LANES, bsz, seq, LANES)
    o = window_attention(q_pairs, kv_all, p['attn_sinks'], lbq=lbq, chunk=chunk)

    y = attn_out_moe_final_norm(o.reshape(t, D_MODEL), p['w_o'], h, p['norm_ffn'][1], p['moe_router'],
                                p['moe_w_gate_up'], p['moe_w_down'], p['final_norm'])

    kv_tail = kv[:, seq - kv_keep:]
    k4 = kv_tail[..., :KV_DIM].reshape(bsz, kv_keep, N_KV_HEADS, ATTN_HEAD_DIM)
    v4 = kv_tail[..., KV_DIM:].reshape(bsz, kv_keep, N_KV_HEADS, ATTN_HEAD_DIM)
    return y.reshape(bsz, seq, d), new_conv[None], new_ssm[None], k4, v4


def kernel(x_prompt, x_sample, state_conv, state_ssm, cache_k, cache_v, norm_mix, norm_ffn, ssm_in_w, ssm_conv_w,
           ssm_conv_b, ssm_dt_bias, ssm_a_log, ssm_d, ssm_norm, ssm_out_w, kv_norm, w_kv, w_q, attn_sinks, w_o,
           ffn_w_gate_up, ffn_w_down, moe_router, moe_w_gate_up, moe_w_down, final_norm):
    in_w = ssm_in_w[0]
    n_dt = N_SSM_HEADS
    p = {
        'norm_mix': norm_mix, 'norm_ffn': norm_ffn,
        'w_z': in_w[:, :D_INNER].astype(BF16),
        'w_xbc': in_w[:, D_INNER:D_INNER + CONV_DIM].astype(BF16),
        'w_dt': jnp.pad(in_w[:, D_INNER + CONV_DIM:], ((0, 0), (0, LANES - n_dt))).astype(BF16),
        'ssm_conv_w': ssm_conv_w[0], 'ssm_conv_b': ssm_conv_b[0], 'ssm_dt_bias': ssm_dt_bias[0],
        'ssm_a_log': ssm_a_log[0], 'ssm_d': ssm_d[0], 'ssm_norm': ssm_norm[0],
        'ssm_out_w': ssm_out_w[0].astype(BF16),
        'kv_norm': kv_norm, 'w_kv': w_kv.astype(BF16), 'w_q': w_q[0].astype(BF16),
        'attn_sinks': attn_sinks[0], 'w_o': w_o[0].astype(BF16),
        'ffn_w_gate_up': ffn_w_gate_up.astype(BF16), 'ffn_w_down': ffn_w_down.astype(BF16),
        'moe_router': jnp.pad(moe_router[0].astype(F32), ((0, 0), (0, LANES - N_EXPERTS))),
        'moe_w_gate_up': moe_w_gate_up[0].astype(BF16), 'moe_w_down': moe_w_down[0].astype(BF16),
        'final_norm': final_norm,
    }
    bsz, seq_p = x_prompt.shape[:2]
    dt = x_prompt.dtype
    zero_conv = jnp.zeros((bsz, CONV_WIDTH - 1, CONV_DIM), dt)
    zero_ssm = jnp.zeros((bsz, N_SSM_HEADS, SSM_HEAD_DIM, D_STATE), dt)
    zero_kv = jnp.zeros((bsz, WINDOW, N_KV_HEADS, ATTN_HEAD_DIM), dt)
    keep = min(WINDOW, seq_p)
    y_p, conv_p, ssm_p, k_p, v_p = _trunk(x_prompt, zero_conv, zero_ssm, zero_kv, zero_kv, False, keep, p)
    y_s, conv_s, ssm_s, k_s, v_s = _trunk(x_sample, state_conv[0], state_ssm[0], cache_k, cache_v, True,
                                          x_sample.shape[1], p)
    return (y_p, y_s, conv_p, ssm_p, k_p, v_p, conv_s, ssm_s, k_s, v_s)
```
